```python
import math
import jax, jax.numpy as jnp
from jax import lax
import numpy as np

D_MODEL = 2048
BATCH = 4
SEQ = 2048
DEPTH = 4
DEC_BATCH = 8
DEC_SEQ = 8
PAST_LEN = 16384
PAGE_SIZE = 128

N_A_LAYERS = DEPTH // 2
N_B_LAYERS = DEPTH - N_A_LAYERS
N_META = 16
M_EXPAND = 2
M_D_INNER = M_EXPAND * D_MODEL
M_HEADDIM = 64
M_NHEADS = M_D_INNER // M_HEADDIM
M_NGROUPS = 8
M_DSTATE = 128
M_CONV = 4
M_CHUNK = 128
M_CONV_DIM = M_D_INNER + 2 * M_NGROUPS * M_DSTATE
M_IN_DIM = M_D_INNER + M_CONV_DIM + M_NHEADS
A_HEADS = 8
A_HEAD_DIM = D_MODEL // (2 * A_HEADS)
ROPE_THETA = 10000.0
Q_BLOCK = 128
N_EXPERTS = 32
TOP_K = 4
D_FF = D_MODEL // 4
MOE_BLOCK = 64
SWIGLU_LIMIT = 7.0
SWIGLU_ALPHA = 1.702
DN_ALPHA = (2 * DEPTH) ** 0.25
DN_BETA = (8 * DEPTH) ** -0.25
NORM_EPS = 1e-5

kernel_name = 'hybrid_ssd_diffattn_moe_step'


def layer_norm(x, g, b):
    xf = x.astype(jnp.float32)
    mu = jnp.mean(xf, -1, keepdims=True)
    var = jnp.mean(jnp.square(xf - mu), -1, keepdims=True)
    return ((xf - mu) * lax.rsqrt(var + NORM_EPS) * g + b).astype(x.dtype)


def rms_norm(x, w):
    xf = x.astype(jnp.float32)
    return (xf * lax.rsqrt(jnp.mean(jnp.square(xf), -1, keepdims=True) + NORM_EPS) * w).astype(x.dtype)


def rope(x, pos):
    half = x.shape[-1] // 2
    inv = ROPE_THETA ** (-jnp.arange(half, dtype=jnp.float32) / half)
    ang = pos.astype(jnp.float32)[:, None] * inv[None, :]
    cos = jnp.cos(ang)[None, :, None, :]
    sin = jnp.sin(ang)[None, :, None, :]
    xf = x.astype(jnp.float32)
    x1, x2 = xf[..., :half], xf[..., half:]
    return jnp.concatenate([x1 * cos - x2 * sin, x2 * cos + x1 * sin], -1).astype(x.dtype)


def causal_conv(u, buf, w, b):
    t = u.shape[1]
    up = jnp.concatenate([buf.astype(u.dtype), u], 1)
    y = b
    for k in range(M_CONV):
        y = y + up[:, k:k + t] * w[k]
    return y, up[:, -(M_CONV - 1):]


def ssd_chunked(x, dt, a, bm, cm, h0):
    bsz, t, nh, hp = x.shape
    ng, ns = bm.shape[2], bm.shape[3]
    rep = nh // ng
    pad = (-t) % M_CHUNK
    nc = (t + pad) // M_CHUNK
    f32 = jnp.float32
    xdt = x.astype(f32) * dt[..., None]
    da = dt * a

    def front(z):
        return jnp.pad(z, ((0, 0), (pad, 0)) + ((0, 0),) * (z.ndim - 2))

    xdt = front(xdt).reshape(bsz, nc, M_CHUNK, ng, rep, hp)
    da = front(da).reshape(bsz, nc, M_CHUNK, ng, rep)
    bm = front(bm.astype(f32)).reshape(bsz, nc, M_CHUNK, ng, ns)
    cm = front(cm.astype(f32)).reshape(bsz, nc, M_CHUNK, ng, ns)
    cs = jnp.cumsum(da, axis=2)
    causal = jnp.tril(jnp.ones((M_CHUNK, M_CHUNK), bool))[None, None, :, :, None, None]
    seg = cs[:, :, :, None] - cs[:, :, None, :]
    decay = jnp.where(causal, jnp.exp(jnp.where(causal, seg, 0.0)), 0.0)
    cb = jnp.einsum('bclgn,bcsgn->bclsg', cm, bm)
    y_diag = jnp.einsum('bclsg,bclsgr,bcsgrp->bclgrp', cb, decay, xdt)
    decay_to_end = jnp.exp(cs[:, :, -1:] - cs)
    chunk_states = jnp.einsum('bclgn,bclgr,bclgrp->bcgrpn', bm, decay_to_end, xdt)
    chunk_decay = jnp.exp(cs[:, :, -1])

    def step(h, inp):
        st, dec = inp
        return h * dec[..., None, None] + st, h

    h_init = h0.astype(f32).reshape(bsz, ng, rep, hp, ns)
    h_fin, h_in = lax.scan(step, h_init, (jnp.moveaxis(chunk_states, 1, 0), jnp.moveaxis(chunk_decay, 1, 0)))
    h_in = jnp.moveaxis(h_in, 0, 1)
    y_off = jnp.einsum('bclgn,bcgrpn,bclgr->bclgrp', cm, h_in, jnp.exp(cs))
    y = (y_diag + y_off).reshape(bsz, nc * M_CHUNK, nh, hp)[:, pad:]
    return y, h_fin.reshape(bsz, nh, hp, ns)


def mamba2(u, ssm0, conv0, w_in, conv_w, conv_b, dt_bias, a_log, d_skip, norm_w, w_out):
    bsz, t, _ = u.shape
    zxbcdt = u @ w_in
    z = zxbcdt[..., :M_D_INNER]
    xbc = zxbcdt[..., M_D_INNER:M_D_INNER + M_CONV_DIM]
    dt = zxbcdt[..., M_D_INNER + M_CONV_DIM:]
    xbc, conv_new = causal_conv(xbc, conv0, conv_w, conv_b)
    xbc = jax.nn.silu(xbc)
    xs = xbc[..., :M_D_INNER].reshape(bsz, t, M_NHEADS, M_HEADDIM)
    bm = xbc[..., M_D_INNER:M_D_INNER + M_NGROUPS * M_DSTATE].reshape(bsz, t, M_NGROUPS, M_DSTATE)
    cm = xbc[..., M_D_INNER + M_NGROUPS * M_DSTATE:].reshape(bsz, t, M_NGROUPS, M_DSTATE)
    dt = jax.nn.softplus(dt.astype(jnp.float32) + dt_bias.astype(jnp.float32))
    a = -jnp.exp(a_log.astype(jnp.float32))
    y, ssm_new = ssd_chunked(xs, dt, a, bm, cm, ssm0)
    y = y + xs.astype(jnp.float32) * d_skip.astype(jnp.float32)[:, None]
    y = y.reshape(bsz, t, M_D_INNER) * jax.nn.silu(z.astype(jnp.float32))
    yg = y.reshape(bsz, t, M_NGROUPS, M_D_INNER // M_NGROUPS)
    yg = yg * lax.rsqrt(jnp.mean(jnp.square(yg), -1, keepdims=True) + NORM_EPS)
    y = yg.reshape(bsz, t, M_D_INNER) * norm_w.astype(jnp.float32)
    out = y.astype(u.dtype) @ w_out
    return out, ssm_new.astype(ssm0.dtype), conv_new


def diff_softmax_attention(q, q_pos, k, v, k_pos, lam):
    bsz, t = q.shape[:2]
    blk = min(Q_BLOCK, t)
    nb = -(-t // blk)
    pad = nb * blk - t
    qp = jnp.pad(q, ((0, 0), (0, pad), (0, 0), (0, 0)))
    pp = jnp.pad(q_pos, (0, pad), mode='edge')
    qb = jnp.moveaxis(qp.reshape(bsz, nb, blk, 2 * A_HEADS, A_HEAD_DIM), 1, 0)
    pb = pp.reshape(nb, blk)

    def one(args):
        qi, pi = args
        s = jnp.einsum('bqhd,bkhd->bhqk', qi, k, preferred_element_type=jnp.float32)
        s = jnp.where(k_pos[None, None, None, :] <= pi[None, None, :, None], s, -jnp.inf)
        p = jax.nn.softmax(s, axis=-1).reshape(bsz, A_HEADS, 2, blk, -1)
        w = p[:, :, 0] - lam * p[:, :, 1]
        return jnp.einsum('bhqk,bkhe->bqhe', w.astype(v.dtype), v, preferred_element_type=jnp.float32)

    o = lax.map(one, (qb, pb))
    o = jnp.moveaxis(o, 0, 1).reshape(bsz, nb * blk, A_HEADS, 2 * A_HEAD_DIM)[:, :t]
    return o.astype(q.dtype)


def diff_attn(h, pos, k_all, v_all, k_pos, w_q, lam_p, subln_w, w_o, lambda_init):
    bsz, t, _ = h.shape
    q = (h @ w_q).reshape(bsz, t, 2 * A_HEADS, A_HEAD_DIM)
    q = rope(q, pos) * (A_HEAD_DIM ** -0.5)
    lp = lam_p.astype(jnp.float32)
    lam = jnp.exp(jnp.sum(lp[0] * lp[1])) - jnp.exp(jnp.sum(lp[2] * lp[3])) + lambda_init
    o = diff_softmax_attention(q, pos, k_all, v_all, k_pos, lam)
    o = rms_norm(o, subln_w) * (1.0 - lambda_init)
    return o.reshape(bsz, t, 2 * A_HEADS * A_HEAD_DIM) @ w_o


def moe(x, w_router, b_router, w_gu, b_gu, w_down, b_down):
    bsz, t, d = x.shape
    xf = x.reshape(-1, d)
    n = xf.shape[0]
    nk = n * TOP_K
    logits = (xf @ w_router).astype(jnp.float32) + b_router.astype(jnp.float32)
    top_v, top_i = lax.top_k(logits, TOP_K)
    gates = jax.nn.softmax(top_v, axis=-1)
    flat_e = top_i.reshape(-1)
    order = jnp.argsort(flat_e)
    e_sorted = flat_e[order]
    tok = (order // TOP_K).astype(jnp.int32)
    g_sorted = gates.reshape(-1)[order]
    sizes = jnp.bincount(flat_e, length=N_EXPERTS).astype(jnp.int32)
    start = jnp.cumsum(sizes) - sizes
    psizes = (sizes + MOE_BLOCK - 1) // MOE_BLOCK * MOE_BLOCK
    pend = jnp.cumsum(psizes)
    pstart = pend - psizes
    dest = pstart[e_sorted] + jnp.arange(nk, dtype=jnp.int32) - start[e_sorted]
    nb = min(-(-nk // MOE_BLOCK) + N_EXPERTS, nk)
    rows = nb * MOE_BLOCK
    tok_pad = jnp.full((rows,), n, jnp.int32).at[dest].set(tok)
    gate_pad = jnp.zeros((rows,), jnp.float32).at[dest].set(g_sorted)
    blk_e = jnp.minimum(jnp.searchsorted(pend, jnp.arange(nb, dtype=jnp.int32) * MOE_BLOCK, side='right'),
                        N_EXPERTS - 1)
    xpad = jnp.concatenate([xf, jnp.zeros((1, d), xf.dtype)], 0)

    def one(args):
        tb, gb, e = args
        xb = xpad[tb]
        gu = xb @ w_gu[e] + b_gu[e]
        gate = jnp.minimum(gu[:, :D_FF], SWIGLU_LIMIT)
        up = jnp.clip(gu[:, D_FF:], -SWIGLU_LIMIT, SWIGLU_LIMIT)
        act = (up + 1.0) * gate * jax.nn.sigmoid(SWIGLU_ALPHA * gate)
        out = act @ w_down[e] + b_down[e]
        return out * gb[:, None].astype(out.dtype)

    out = lax.map(one, (tok_pad.reshape(nb, MOE_BLOCK), gate_pad.reshape(nb, MOE_BLOCK), blk_e))
    y = jax.ops.segment_sum(out.reshape(rows, d), tok_pad, num_segments=n + 1)[:n]
    return y.reshape(bsz, t, d).astype(x.dtype)


def run_trunk(h, pos, ssm0, conv0, k_past, v_past, p):
    bsz, t, _ = h.shape
    ssm_new, conv_new = [], []
    k_new = v_new = k_all = v_all = k_pos = None
    for layer in range(DEPTH):
        if layer < N_A_LAYERS:
            mix, s, c = mamba2(h, ssm0[layer], conv0[layer], p['m_w_in'][layer], p['m_conv_w'][layer],
                               p['m_conv_b'][layer], p['m_dt_bias'][layer], p['m_a_log'][layer],
                               p['m_d'][layer], p['m_norm_w'][layer], p['m_w_out'][layer])
            ssm_new.append(s)
            conv_new.append(c)
        else:
            j = layer - N_A_LAYERS
            if j == 0:
                k_new = rope((h @ p['a_w_k']).reshape(bsz, t, 2 * A_HEADS, A_HEAD_DIM), pos)
                v_new = (h @ p['a_w_v']).reshape(bsz, t, A_HEADS, 2 * A_HEAD_DIM)
                if k_past is None:
                    k_all, v_all, k_pos = k_new, v_new, pos
                else:
                    k_all = jnp.concatenate([k_past.astype(k_new.dtype), k_new], 1)
                    v_all = jnp.concatenate([v_past.astype(v_new.dtype), v_new], 1)
                    k_pos = jnp.concatenate([jnp.arange(k_past.shape[1], dtype=pos.dtype), pos])
            lambda_init = 0.8 - 0.6 * math.exp(-0.3 * layer)
            mix = diff_attn(h, pos, k_all, v_all, k_pos, p['a_w_q'][j], p['a_lambda'][j],
                            p['a_subln_w'][j], p['a_w_o'][j], lambda_init)
        h = layer_norm(DN_ALPHA * h + mix, p['ln_g'][layer, 0], p['ln_b'][layer, 0])
        f = moe(h, p['moe_w_router'][layer], p['moe_b_router'][layer], p['moe_w_gu'][layer],
                p['moe_b_gu'][layer], p['moe_w_down'][layer], p['moe_b_down'][layer])
        h = layer_norm(DN_ALPHA * h + f, p['ln_g'][layer, 1], p['ln_b'][layer, 1])
    return h, k_new, v_new, jnp.stack(ssm_new), jnp.stack(conv_new)


def setup_inputs(seed: int = 0) -> dict:
    key = jax.random.key(seed)
    keys = jax.random.split(key, 30)
    f32 = jnp.float32

    def nrm(i, shape, scale):
        return jax.random.normal(keys[i], shape, f32) * scale

    n_pages = PAST_LEN // PAGE_SIZE
    n_pool = (5 * DEC_BATCH * n_pages) // 4
    page_table = jax.random.permutation(keys[0], n_pool)[:DEC_BATCH * n_pages]
    page_table = page_table.reshape(DEC_BATCH, n_pages).astype(jnp.int32)
    dt0 = jnp.exp(jax.random.uniform(keys[1], (N_A_LAYERS, M_NHEADS), f32)
                  * (math.log(0.1) - math.log(0.001)) + math.log(0.001))
    sd = D_MODEL ** -0.5
    return {
        'x_prompt': nrm(2, (BATCH, SEQ, D_MODEL), 1.0),
        'x_sample': nrm(3, (DEC_BATCH, DEC_SEQ, D_MODEL), 1.0),
        'cache_k': nrm(4, (n_pool, PAGE_SIZE, 2 * A_HEADS, A_HEAD_DIM), 1.0),
        'cache_v': nrm(5, (n_pool, PAGE_SIZE, A_HEADS, 2 * A_HEAD_DIM), 1.0),
        'page_table': page_table,
        'state_ssm': nrm(6, (N_A_LAYERS, DEC_BATCH, M_NHEADS, M_HEADDIM, M_DSTATE), 0.1),
        'state_conv': nrm(7, (N_A_LAYERS, DEC_BATCH, M_CONV - 1, M_CONV_DIM), 1.0),
        'meta_tokens': nrm(8, (N_META, D_MODEL), 1.0),
        'm_w_in': nrm(9, (N_A_LAYERS, D_MODEL, M_IN_DIM), sd),
        'm_conv_w': nrm(10, (N_A_LAYERS, M_CONV, M_CONV_DIM), M_CONV ** -0.5),
        'm_conv_b': nrm(11, (N_A_LAYERS, M_CONV_DIM), 0.01),
        'm_dt_bias': dt0 + jnp.log(-jnp.expm1(-dt0)),
        'm_a_log': jnp.log(jax.random.uniform(keys[12], (N_A_LAYERS, M_NHEADS), f32, 1.0, 16.0)),
        'm_d': 1.0 + nrm(13, (N_A_LAYERS, M_NHEADS), 0.1),
        'm_norm_w': 1.0 + nrm(14, (N_A_LAYERS, M_D_INNER), 0.02),
        'm_w_out': nrm(15, (N_A_LAYERS, M_D_INNER, D_MODEL), M_D_INNER ** -0.5 * DN_BETA),
        'a_w_q': nrm(16, (N_B_LAYERS, D_MODEL, 2 * A_HEADS * A_HEAD_DIM), sd),
        'a_w_k': nrm(17, (D_MODEL, 2 * A_HEADS * A_HEAD_DIM), sd),
        'a_w_v': nrm(18, (D_MODEL, 2 * A_HEADS * A_HEAD_DIM), sd),
        'a_lambda': nrm(19, (N_B_LAYERS, 4, A_HEAD_DIM), 0.1),
        'a_subln_w': 1.0 + nrm(20, (N_B_LAYERS, 2 * A_HEAD_DIM), 0.02),
        'a_w_o': nrm(21, (N_B_LAYERS, 2 * A_HEADS * A_HEAD_DIM, D_MODEL), (2 * A_HEADS * A_HEAD_DIM) ** -0.5 * DN_BETA),
        'moe_w_router': nrm(22, (DEPTH, D_MODEL, N_EXPERTS), sd),
        'moe_b_router': nrm(23, (DEPTH, N_EXPERTS), 0.01),
        'moe_w_gu': nrm(24, (DEPTH, N_EXPERTS, D_MODEL, 2 * D_FF), sd),
        'moe_b_gu': nrm(25, (DEPTH, N_EXPERTS, 2 * D_FF), 0.01),
        'moe_w_down': nrm(26, (DEPTH, N_EXPERTS, D_FF, D_MODEL), D_FF ** -0.5 * DN_BETA),
        'moe_b_down': nrm(27, (DEPTH, N_EXPERTS, D_MODEL), 0.01),
        'ln_g': 1.0 + nrm(28, (DEPTH, 2, D_MODEL), 0.02),
        'ln_b': nrm(29, (DEPTH, 2, D_MODEL), 0.02),
    }


def reference(x_prompt, x_sample, cache_k, cache_v, page_table, state_ssm, state_conv, meta_tokens,
              m_w_in, m_conv_w, m_conv_b, m_dt_bias, m_a_log, m_d, m_norm_w, m_w_out,
              a_w_q, a_w_k, a_w_v, a_lambda, a_subln_w, a_w_o,
              moe_w_router, moe_b_router, moe_w_gu, moe_b_gu, moe_w_down, moe_b_down, ln_g, ln_b):
    p = dict(m_w_in=m_w_in, m_conv_w=m_conv_w, m_conv_b=m_conv_b, m_dt_bias=m_dt_bias, m_a_log=m_a_log,
             m_d=m_d, m_norm_w=m_norm_w, m_w_out=m_w_out, a_w_q=a_w_q, a_w_k=a_w_k, a_w_v=a_w_v,
             a_lambda=a_lambda, a_subln_w=a_subln_w, a_w_o=a_w_o, moe_w_router=moe_w_router,
             moe_b_router=moe_b_router, moe_w_gu=moe_w_gu, moe_b_gu=moe_b_gu, moe_w_down=moe_w_down,
             moe_b_down=moe_b_down, ln_g=ln_g, ln_b=ln_b)
    bsz = x_prompt.shape[0]
    meta = jnp.broadcast_to(meta_tokens[None].astype(x_prompt.dtype), (bsz, N_META, D_MODEL))
    h_p = jnp.concatenate([meta, x_prompt], 1)
    pos_p = jnp.arange(h_p.shape[1], dtype=jnp.int32)
    ssm0 = jnp.zeros((N_A_LAYERS, bsz, M_NHEADS, M_HEADDIM, M_DSTATE), state_ssm.dtype)
    conv0 = jnp.zeros((N_A_LAYERS, bsz, M_CONV - 1, M_CONV_DIM), x_prompt.dtype)
    y_p, k_prompt, v_prompt, ssm_prompt, conv_prompt = run_trunk(h_p, pos_p, ssm0, conv0, None, None, p)
    y_prompt = y_p[:, N_META:]
    dec_b, n_pages = page_table.shape
    past = n_pages * cache_k.shape[1]
    k_past = cache_k[page_table].reshape(dec_b, past, 2 * A_HEADS, A_HEAD_DIM)
    v_past = cache_v[page_table].reshape(dec_b, past, A_HEADS, 2 * A_HEAD_DIM)
    pos_s = past + jnp.arange(x_sample.shape[1], dtype=jnp.int32)
    y_sample, k_sample, v_sample, ssm_sample, conv_sample = run_trunk(
        x_sample, pos_s, state_ssm, state_conv, k_past, v_past, p)
    return (y_prompt, y_sample, k_prompt, v_prompt, ssm_prompt, conv_prompt, k_sample, v_sample, ssm_sample, conv_sample)
```

```python
import functools
import math

import jax
import jax.numpy as jnp
from jax import lax
from jax.experimental import pallas as pl
from jax.experimental.pallas import tpu as pltpu

F32 = jnp.float32
BF16 = jnp.bfloat16

D_MODEL = 2048
BATCH = 4
SEQ = 2048
DEPTH = 4
DEC_BATCH = 8
DEC_SEQ = 8
PAGE_SIZE = 128
N_A_LAYERS = DEPTH // 2
N_META = 16
M_D_INNER = 2 * D_MODEL
M_HEADDIM = 64
M_NHEADS = M_D_INNER // M_HEADDIM
M_NGROUPS = 8
M_DSTATE = 128
M_CONV = 4
M_CONV_DIM = M_D_INNER + 2 * M_NGROUPS * M_DSTATE
A_HEADS = 8
A_HEAD_DIM = D_MODEL // (2 * A_HEADS)
ROPE_THETA = 10000.0
N_EXPERTS = 32
TOP_K = 4
D_FF = D_MODEL // 4
SWIGLU_LIMIT = 7.0
SWIGLU_ALPHA = 1.702
DN_ALPHA = (2 * DEPTH) ** 0.25
NORM_EPS = 1e-5

ROWS_X = BATCH * SEQ
ROW_META = ROWS_X
ROWS_META = BATCH * N_META
ROW_SAMPLE = ROW_META + ROWS_META
ROWS_SAMPLE = DEC_BATCH * DEC_SEQ
N_ROWS = ROW_SAMPLE + ROWS_SAMPLE

LANES = 128
VMEM_LIMIT = 56 * 1024 * 1024

SSD_CHUNK = 128
MOE_TM = 256
MOE_BLOCKS = N_ROWS * TOP_K // MOE_TM + N_EXPERTS
MOE_ROWS = MOE_BLOCKS * MOE_TM
MM_TM = 640
LN_TR = 320
FLASH_T = 512


def _cparams(n_axes):
    return pltpu.CompilerParams(dimension_semantics=("arbitrary",) * n_axes,
                                vmem_limit_bytes=VMEM_LIMIT)


def _dot(a, b):
    return jnp.dot(a, b, preferred_element_type=F32)


def _dot_nt(a, b):
    return lax.dot_general(a, b, (((1,), (1,)), ((), ())), preferred_element_type=F32)


def _split2(x):
    hi = x.astype(BF16)
    lo = (x - hi.astype(F32)).astype(BF16)
    return hi, lo


def _split3(x):
    hi = x.astype(BF16)
    r = x - hi.astype(F32)
    mid = r.astype(BF16)
    lo = (r - mid.astype(F32)).astype(BF16)
    return hi, mid, lo


def _mm_body(*refs, rope, scale):
    if rope:
        x_ref, w_ref, cos_ref, sin_ref, o_ref, wbf_ref = refs
    else:
        x_ref, w_ref, o_ref, wbf_ref = refs

    @pl.when(pl.program_id(1) == 0)
    def _cast_weight():
        wbf_ref[...] = w_ref[...].astype(BF16)

    acc = _dot(x_ref[...].astype(BF16), wbf_ref[...])
    if rope:
        cos = cos_ref[...]
        sin = sin_ref[...]
        for j in range(acc.shape[1] // LANES):
            blk = acc[:, j * LANES:(j + 1) * LANES]
            out = (blk * cos + pltpu.roll(blk, LANES // 2, 1) * sin) * scale
            o_ref[:, j * LANES:(j + 1) * LANES] = out.astype(o_ref.dtype)
    else:
        o_ref[...] = acc.astype(o_ref.dtype)


def _matmul(x, w, *, layer=None, col0=0, n_out, tn, out_dtype=F32, rope=None, scale=1.0):
    m, k = x.shape
    tm = MM_TM
    assert m % tm == 0 and n_out % tn == 0 and col0 % tn == 0
    jb = col0 // tn
    if layer is None:
        w_spec = pl.BlockSpec((k, tn), lambda j, i: (0, j + jb))
    else:
        w_spec = pl.BlockSpec((None, k, tn), lambda j, i: (layer, 0, j + jb))
    in_specs = [pl.BlockSpec((tm, k), lambda j, i: (i, 0)), w_spec]
    args = [x, w]
    if rope is not None:
        in_specs += [pl.BlockSpec((tm, LANES), lambda j, i: (i, 0))] * 2
        args += list(rope)
    return pl.pallas_call(
        functools.partial(_mm_body, rope=rope is not None, scale=scale),
        grid=(n_out // tn, m // tm),
        in_specs=in_specs,
        out_specs=pl.BlockSpec((tm, tn), lambda j, i: (i, j)),
        out_shape=jax.ShapeDtypeStruct((m, n_out), out_dtype),
        scratch_shapes=[pltpu.VMEM((k, tn), BF16)],
        compiler_params=_cparams(2),
        name="dense_matmul",
    )(*args)


def _router_body(x_ref, w_ref, b_ref, o_ref):
    xh, xl = _split2(x_ref[...])
    wh, wl = _split2(w_ref[...])
    o_ref[...] = _dot(xh, wh) + _dot(xh, wl) + _dot(xl, wh) + b_ref[...]


def _router_logits(h, w_router, b_router, layer):
    tm = MM_TM
    return pl.pallas_call(
        _router_body,
        grid=(N_ROWS // tm,),
        in_specs=[pl.BlockSpec((tm, D_MODEL), lambda i: (i, 0)),
                  pl.BlockSpec((None, D_MODEL, N_EXPERTS), lambda i: (layer, 0, 0)),
                  pl.BlockSpec((None, 1, N_EXPERTS), lambda i: (layer, 0, 0))],
        out_specs=pl.BlockSpec((tm, N_EXPERTS), lambda i: (i, 0)),
        out_shape=jax.ShapeDtypeStruct((N_ROWS, N_EXPERTS), F32),
        compiler_params=_cparams(1),
        name="router_logits",
    )(h, w_router, b_router.reshape(DEPTH, 1, N_EXPERTS))


def _layer_norm_rows(x, g, b):
    mu = jnp.mean(x, axis=-1, keepdims=True)
    xc = x - mu
    var = jnp.mean(xc * xc, axis=-1, keepdims=True)
    return xc * lax.rsqrt(var + NORM_EPS) * g + b


def _ln_body(h_ref, m_ref, g_ref, b_ref, o_ref, obf_ref):
    y = _layer_norm_rows(DN_ALPHA * h_ref[...] + m_ref[...], g_ref[...], b_ref[...])
    o_ref[...] = y
    obf_ref[...] = y.astype(BF16)


def _ln_moe_body(h_ref, e_ref, gate_ref, g_ref, b_ref, o_ref, obf_ref):
    gates = gate_ref[...]
    f = e_ref[0] * gates[:, 0:1]
    for k in range(1, TOP_K):
        f = f + e_ref[k] * gates[:, k:k + 1]
    y = _layer_norm_rows(DN_ALPHA * h_ref[...] + f, g_ref[...], b_ref[...])
    o_ref[...] = y
    obf_ref[...] = y.astype(BF16)


def _ln_outs():
    return dict(
        out_shape=(jax.ShapeDtypeStruct((N_ROWS, D_MODEL), F32),
                   jax.ShapeDtypeStruct((N_ROWS, D_MODEL), BF16)))


def _ln_residual(h, mix, ln_g, ln_b, layer, sub):
    tr = LN_TR
    row = pl.BlockSpec((tr, D_MODEL), lambda i: (i, 0))
    par = pl.BlockSpec((None, None, 1, D_MODEL), lambda i: (layer, sub, 0, 0))
    return pl.pallas_call(
        _ln_body, grid=(N_ROWS // tr,),
        in_specs=[row, row, par, par], out_specs=(row, row),
        compiler_params=_cparams(1), name="ln_residual", **_ln_outs(),
    )(h, mix, ln_g.reshape(DEPTH, 2, 1, D_MODEL), ln_b.reshape(DEPTH, 2, 1, D_MODEL))


def _ln_moe(h, eg, gates, ln_g, ln_b, layer):
    tr = LN_TR // 2
    row = pl.BlockSpec((tr, D_MODEL), lambda i: (i, 0))
    par = pl.BlockSpec((None, None, 1, D_MODEL), lambda i: (layer, 1, 0, 0))
    return pl.pallas_call(
        _ln_moe_body, grid=(N_ROWS // tr,),
        in_specs=[row, pl.BlockSpec((TOP_K, tr, D_MODEL), lambda i: (0, i, 0)),
                  pl.BlockSpec((tr, TOP_K), lambda i: (i, 0)), par, par],
        out_specs=(row, row),
        compiler_params=_cparams(1), name="ln_moe_combine", **_ln_outs(),
    )(h, eg, gates, ln_g.reshape(DEPTH, 2, 1, D_MODEL), ln_b.reshape(DEPTH, 2, 1, D_MODEL))


def _moe_body(be_ref, nb_ref, x_ref, wgu_ref, bgu_ref, wd_ref, bd_ref, o_ref, wgu_bf, wd_bf):
    i = pl.program_id(0)
    e = be_ref[i]
    prev = be_ref[jnp.maximum(i - 1, 0)]

    @pl.when(jnp.logical_or(i == 0, e != prev))
    def _cast_weights():
        wgu_bf[...] = wgu_ref[...].astype(BF16)
        wd_bf[...] = wd_ref[...].astype(BF16)

    @pl.when(i < nb_ref[0])
    def _compute():
        gu = _dot(x_ref[...], wgu_bf[...]) + bgu_ref[...]
        gate = jnp.minimum(gu[:, :D_FF], SWIGLU_LIMIT)
        up = jnp.clip(gu[:, D_FF:], -SWIGLU_LIMIT, SWIGLU_LIMIT)
        act = (up + 1.0) * gate * jax.nn.sigmoid(SWIGLU_ALPHA * gate)
        o_ref[...] = _dot(act.astype(BF16), wd_bf[...]) + bd_ref[...]

    @pl.when(i >= nb_ref[0])
    def _unused_block():
        o_ref[...] = jnp.zeros(o_ref.shape, o_ref.dtype)


def _moe_experts(xs, blk_e, n_used, w_gu, b_gu, w_down, b_down, layer):
    grid_spec = pltpu.PrefetchScalarGridSpec(
        num_scalar_prefetch=2,
        grid=(MOE_BLOCKS,),
        in_specs=[
            pl.BlockSpec((MOE_TM, D_MODEL), lambda i, be, nb: (i, 0)),
            pl.BlockSpec((None, None, D_MODEL, 2 * D_FF), lambda i, be, nb: (layer, be[i], 0, 0)),
            pl.BlockSpec((None, None, 1, 2 * D_FF), lambda i, be, nb: (layer, be[i], 0, 0)),
            pl.BlockSpec((None, None, D_FF, D_MODEL), lambda i, be, nb: (layer, be[i], 0, 0)),
            pl.BlockSpec((None, None, 1, D_MODEL), lambda i, be, nb: (layer, be[i], 0, 0)),
        ],
        out_specs=pl.BlockSpec((MOE_TM, D_MODEL), lambda i, be, nb: (i, 0)),
        scratch_shapes=[pltpu.VMEM((D_MODEL, 2 * D_FF), BF16), pltpu.VMEM((D_FF, D_MODEL), BF16)],
    )
    return pl.pallas_call(
        _moe_body, grid_spec=grid_spec,
        out_shape=jax.ShapeDtypeStruct((MOE_ROWS, D_MODEL), F32),
        compiler_params=_cparams(1), name="moe_experts",
    )(blk_e, n_used, xs, w_gu, b_gu.reshape(DEPTH, N_EXPERTS, 1, 2 * D_FF),
      w_down, b_down.reshape(DEPTH, N_EXPERTS, 1, D_MODEL))


def _moe_layer(h, hb, p, layer):
    logits = _router_logits(h, p['moe_w_router'], p['moe_b_router'], layer)
    top_v, top_i = lax.top_k(logits, TOP_K)
    gates = jax.nn.softmax(top_v, axis=-1)
    sel = jnp.sum((top_i[:, :, None] == jnp.arange(N_EXPERTS, dtype=jnp.int32)).astype(jnp.int32), axis=1)
    counts = jnp.sum(sel, axis=0)
    rank = jnp.cumsum(sel, axis=0) - sel
    psz = (counts + MOE_TM - 1) // MOE_TM * MOE_TM
    pend = jnp.cumsum(psz)
    pstart = pend - psz
    dest = (pstart[top_i] + jnp.take_along_axis(rank, top_i, axis=1)).astype(jnp.int32)
    n_used = (pend[-1] // MOE_TM).astype(jnp.int32)
    blk = jnp.arange(MOE_BLOCKS, dtype=jnp.int32)
    blk_e = jnp.minimum(jnp.searchsorted(pend, blk * MOE_TM, side='right'), N_EXPERTS - 1).astype(jnp.int32)
    blk_e = jnp.where(blk < n_used, blk_e, blk_e[jnp.maximum(n_used - 1, 0)])
    tok = jnp.repeat(jnp.arange(N_ROWS, dtype=jnp.int32), TOP_K)
    tok_pad = jnp.full((MOE_ROWS,), N_ROWS, jnp.int32).at[dest.reshape(-1)].set(tok)
    xs = jnp.concatenate([hb, jnp.zeros((1, D_MODEL), BF16)], axis=0)[tok_pad]
    eo = _moe_experts(xs, blk_e, n_used.reshape(1), p['moe_w_gu'], p['moe_b_gu'],
                      p['moe_w_down'], p['moe_b_down'], layer)
    eg = eo[dest.T]
    return _ln_moe(h, eg, gates, p['ln_g'], p['ln_b'], layer)


def _ssd_body(z_ref, xbc_ref, dt_ref, h0_ref, c0_ref, cw_ref, cb_ref, dtb_ref, alog_ref, de_ref, nw_ref,
              eexp_ref, y_ref, hfin_ref, cfin_ref, ht_ref, ext_ref, act_ref, ybuf_ref, *, lv, nc):
    L = SSD_CHUNK
    N = M_DSTATE
    c = pl.program_id(1)
    n_pairs = M_NHEADS // 2
    x_cols = M_D_INNER
    b_col0 = M_D_INNER
    c_col0 = M_D_INNER + M_NGROUPS * M_DSTATE

    @pl.when(c == 0)
    def _init():
        for i in range(n_pairs):
            ht_ref[i] = h0_ref[2 * i:2 * i + 2].reshape(2 * M_HEADDIM, N).T
        ext_ref[0:8, :] = jnp.zeros((8, M_CONV_DIM), F32)
        ext_ref[5:8, :] = c0_ref[...]

    raw = xbc_ref[...]
    if lv < L:
        raw = jnp.concatenate([raw, jnp.zeros((L - lv, M_CONV_DIM), F32)], axis=0)
    ext_ref[8:8 + L, :] = raw
    cw_step = 256
    for j in range(M_CONV_DIM // cw_step):
        cs_ = slice(j * cw_step, (j + 1) * cw_step)
        acc = cb_ref[:, cs_] + cw_ref[0:1, cs_] * ext_ref[5:5 + L, cs_]
        acc = acc + cw_ref[1:2, cs_] * ext_ref[6:6 + L, cs_]
        acc = acc + cw_ref[2:3, cs_] * ext_ref[7:7 + L, cs_]
        acc = acc + cw_ref[3:4, cs_] * ext_ref[8:8 + L, cs_]
        act_ref[:, cs_] = jax.nn.silu(acc)
    carry = ext_ref[5 + lv:8 + lv, :]
    ext_ref[5:8, :] = carry
    cfin_ref[...] = carry

    dtr = dt_ref[...]
    if lv < L:
        dtr = jnp.concatenate([dtr, jnp.zeros((L - lv, LANES), F32)], axis=0)
    dt = jax.nn.softplus(dtr + dtb_ref[...])
    row = lax.broadcasted_iota(jnp.int32, (L, L), 0)
    col = lax.broadcasted_iota(jnp.int32, (L, L), 1)
    if lv < L:
        dt = jnp.where(row < lv, dt, 0.0)
    da = dt * (-jnp.exp(alog_ref[...]))
    causal = row >= col
    tri = jnp.where(causal, 1.0, 0.0).astype(BF16)
    d_hi, d_mid, d_lo = _split3(da)
    cs = _dot(tri, d_hi) + _dot(tri, d_mid) + _dot(tri, d_lo)
    cs_end = cs[L - 1:L, :]
    cst = cs.T
    dtt = dt.T
    wt = (dt * jnp.exp(cs_end - cs)).T
    e_hi, e_lo = _split2(jnp.broadcast_to(jnp.exp(cs_end), (8, LANES)))
    dend = _dot(e_hi, eexp_ref[...]) + _dot(e_lo, eexp_ref[...])
    lane = lax.broadcasted_iota(jnp.int32, (L, LANES), 1)
    first_head = lane < M_HEADDIM

    for g in range(M_NGROUPS):
        bg = act_ref[:, b_col0 + g * N:b_col0 + (g + 1) * N]
        cg = act_ref[:, c_col0 + g * N:c_col0 + (g + 1) * N]
        btg = bg.T
        cb = _dot(cg.astype(BF16), btg.astype(BF16))
        for pr in range(M_NHEADS // M_NGROUPS // 2):
            i = g * (M_NHEADS // M_NGROUPS // 2) + pr
            ps = slice(i * LANES, (i + 1) * LANES)
            xs_pair = act_ref[:, ps]
            ht_pair = ht_ref[i]
            lhs_rows = []
            bts_rows = []
            for j in range(2):
                hd = 2 * i + j
                colb = jnp.broadcast_to(cs[:, hd:hd + 1], (L, L))
                seg = colb - cst[hd:hd + 1, :]
                decay = jnp.exp(jnp.where(causal, seg, -jnp.inf))
                m_h = cb * decay * dtt[hd:hd + 1, :]
                c_h = cg * jnp.exp(colb)
                lhs_rows.append(jnp.concatenate([m_h, c_h], axis=1))
                bts_rows.append(btg * wt[hd:hd + 1, :])
            lhs = jnp.concatenate(lhs_rows, axis=0).astype(BF16)
            rhs = jnp.concatenate([xs_pair, ht_pair], axis=0).astype(BF16)
            out = _dot(lhs, rhs)
            y_pair = jnp.where(first_head, out[:L], out[L:]) + xs_pair * de_ref[:, ps]
            ybuf_ref[:, ps] = y_pair
            bts = jnp.concatenate(bts_rows, axis=0).astype(BF16)
            upd = _dot(bts, xs_pair.astype(BF16))
            ht_ref[i] = dend[0:1, ps] * ht_pair + jnp.where(first_head, upd[:N], upd[N:])

    gw = M_D_INNER // M_NGROUPS
    for g in range(M_NGROUPS):
        gs = slice(g * gw, (g + 1) * gw)
        yg = ybuf_ref[0:lv, gs] * jax.nn.silu(z_ref[:, gs])
        ms = jnp.mean(yg * yg, axis=-1, keepdims=True)
        y_ref[:, gs] = (yg * lax.rsqrt(ms + NORM_EPS) * nw_ref[:, gs]).astype(y_ref.dtype)

    @pl.when(c == nc - 1)
    def _final():
        for i in range(n_pairs):
            hfin_ref[2 * i:2 * i + 2] = ht_ref[i].T.reshape(2, M_HEADDIM, N)


def _ssd(z, xbc, dtr, h0, h0_layer, c0, c0_layer, prm, *, row0, lv, nb, nc, out_dtype):
    rb = row0 // lv

    def rows(width):
        return pl.BlockSpec((lv, width), lambda b, c: (rb + b * nc + c, 0))

    def par(shape):
        return pl.BlockSpec(shape, lambda b, c: (0, 0))

    if h0_layer is None:
        h0_spec = pl.BlockSpec((None, M_NHEADS, M_HEADDIM, M_DSTATE), lambda b, c: (b, 0, 0, 0))
        c0_spec = pl.BlockSpec((None, M_CONV - 1, M_CONV_DIM), lambda b, c: (b, 0, 0))
    else:
        h0_spec = pl.BlockSpec((None, None, M_NHEADS, M_HEADDIM, M_DSTATE),
                               lambda b, c: (h0_layer, b, 0, 0, 0))
        c0_spec = pl.BlockSpec((None, None, M_CONV - 1, M_CONV_DIM), lambda b, c: (c0_layer, b, 0, 0))
    out_rows = nb * nc * lv
    return pl.pallas_call(
        functools.partial(_ssd_body, lv=lv, nc=nc),
        grid=(nb, nc),
        in_specs=[rows(M_D_INNER), rows(M_CONV_DIM), rows(LANES), h0_spec, c0_spec,
                  par((M_CONV, M_CONV_DIM)), par((1, M_CONV_DIM)), par((1, LANES)), par((1, LANES)),
                  par((1, M_D_INNER)), par((1, M_D_INNER)), par((LANES, M_D_INNER))],
        out_specs=(pl.BlockSpec((lv, M_D_INNER), lambda b, c: (b * nc + c, 0)),
                   pl.BlockSpec((None, M_NHEADS, M_HEADDIM, M_DSTATE), lambda b, c: (b, 0, 0, 0)),
                   pl.BlockSpec((None, M_CONV - 1, M_CONV_DIM), lambda b, c: (b, 0, 0))),
        out_shape=(jax.ShapeDtypeStruct((out_rows, M_D_INNER), out_dtype),
                   jax.ShapeDtypeStruct((nb, M_NHEADS, M_HEADDIM, M_DSTATE), F32),
                   jax.ShapeDtypeStruct((nb, M_CONV - 1, M_CONV_DIM), F32)),
        scratch_shapes=[pltpu.VMEM((M_NHEADS // 2, M_DSTATE, LANES), F32),
                        pltpu.VMEM((8 + SSD_CHUNK, M_CONV_DIM), F32),
                        pltpu.VMEM((SSD_CHUNK, M_CONV_DIM), F32),
                        pltpu.VMEM((SSD_CHUNK, M_D_INNER), F32)],
        compiler_params=_cparams(2), name="ssd_mixer",
    )(z, xbc, dtr, h0, c0, *prm)


def _mamba_layer(h, hb, state_ssm, state_conv, p, layer):
    w_in = p['m_w_in']
    z = _matmul(hb, w_in, layer=layer, col0=0, n_out=M_D_INNER, tn=1024)
    xbc = _matmul(hb, w_in, layer=layer, col0=M_D_INNER, n_out=M_CONV_DIM, tn=1024)
    w_dt = jnp.pad(w_in[layer, :, M_D_INNER + M_CONV_DIM:], ((0, 0), (0, LANES - M_NHEADS)))
    dtr = _matmul(hb, w_dt, n_out=LANES, tn=LANES)

    def pad_heads(v):
        return jnp.pad(v[layer].astype(F32), (0, LANES - M_NHEADS)).reshape(1, LANES)

    head_of_col = jnp.arange(M_D_INNER, dtype=jnp.int32) // M_HEADDIM
    expand = (jnp.arange(LANES, dtype=jnp.int32)[:, None] == head_of_col[None, :]).astype(BF16)
    prm = (p['m_conv_w'][layer], p['m_conv_b'][layer].reshape(1, M_CONV_DIM),
           pad_heads(p['m_dt_bias']), pad_heads(p['m_a_log']),
           jnp.repeat(p['m_d'][layer].astype(F32), M_HEADDIM).reshape(1, M_D_INNER),
           p['m_norm_w'][layer].reshape(1, M_D_INNER), expand)
    zeros_h = jnp.zeros((BATCH, M_NHEADS, M_HEADDIM, M_DSTATE), F32)
    zeros_c = jnp.zeros((BATCH, M_CONV - 1, M_CONV_DIM), F32)
    y_m, h_m, c_m = _ssd(z, xbc, dtr, zeros_h, None, zeros_c, None, prm,
                         row0=ROW_META, lv=N_META, nb=BATCH, nc=1, out_dtype=BF16)
    y_x, h_x, c_x = _ssd(z, xbc, dtr, h_m, None, c_m, None, prm,
                         row0=0, lv=SSD_CHUNK, nb=BATCH, nc=SEQ // SSD_CHUNK, out_dtype=BF16)
    y_s, h_s, c_s = _ssd(z, xbc, dtr, state_ssm, layer, state_conv, layer, prm,
                         row0=ROW_SAMPLE, lv=DEC_SEQ, nb=DEC_BATCH, nc=1, out_dtype=F32)
    yn = jnp.concatenate([y_x, y_m, y_s.astype(BF16)], axis=0)
    mix = _matmul(yn, p['m_w_out'], layer=layer, n_out=D_MODEL, tn=512)
    h, hb = _ln_residual(h, mix, p['ln_g'], p['ln_b'], layer, 0)
    return h, hb, (h_x, c_x, h_s, c_s)


def _lambda_value(lam_ref, lam_init):
    lp = lam_ref[...]
    s1 = jnp.sum(lp[0:1, :] * lp[1:2, :], axis=-1, keepdims=True)
    s2 = jnp.sum(lp[2:3, :] * lp[3:4, :], axis=-1, keepdims=True)
    return jnp.exp(s1) - jnp.exp(s2) + lam_init


def _subln(o, sw, lam_init):
    ms = jnp.mean(o * o, axis=-1, keepdims=True)
    return o * lax.rsqrt(ms + NORM_EPS) * sw * (1.0 - lam_init)


def _rep(x, n):
    return x if n == 1 else jnp.concatenate([x] * n, axis=1)


def _flash_body(q_ref, k_ref, v_ref, km_ref, vm_ref, lam_ref, sw_ref, o_ref,
                m_ref, l_ref, acc_ref, kpad_ref, vpad_ref, *, lam_init):
    t = FLASH_T
    dh = A_HEAD_DIM
    qi = pl.program_id(2)
    kj = pl.program_id(3)

    def process(kb, vb, mask):
        for j in range(2):
            qj = q_ref[:, j * dh:(j + 1) * dh].astype(BF16)
            s = _dot_nt(qj, kb[:, j * dh:(j + 1) * dh])
            if mask is not None:
                s = jnp.where(mask, s, -jnp.inf)
            m_prev = m_ref[j]
            m_new = jnp.maximum(m_prev, jnp.max(s, axis=1, keepdims=True))
            alpha = jnp.exp(m_prev - m_new)
            pmat = jnp.exp(s - _rep(m_new, s.shape[1] // LANES))
            l_ref[j] = alpha * l_ref[j] + jnp.sum(pmat, axis=1, keepdims=True)
            acc_ref[j] = _rep(alpha, 2) * acc_ref[j] + _dot(pmat.astype(BF16), vb)
            m_ref[j] = m_new

    @pl.when(kj == 0)
    def _start():
        m_ref[...] = jnp.full(m_ref.shape, -jnp.inf, F32)
        l_ref[...] = jnp.zeros(l_ref.shape, F32)
        acc_ref[...] = jnp.zeros(acc_ref.shape, F32)
        kpad_ref[...] = jnp.zeros(kpad_ref.shape, BF16)
        vpad_ref[...] = jnp.zeros(vpad_ref.shape, BF16)
        kpad_ref[0:N_META, :] = km_ref[...].astype(BF16)
        vpad_ref[0:N_META, :] = vm_ref[...].astype(BF16)
        colm = lax.broadcasted_iota(jnp.int32, (t, LANES), 1)
        process(kpad_ref[...], vpad_ref[...], colm < N_META)

    @pl.when(kj < qi)
    def _full_block():
        process(k_ref[...].astype(BF16), v_ref[...].astype(BF16), None)

    @pl.when(kj == qi)
    def _diag_block():
        r = lax.broadcasted_iota(jnp.int32, (t, t), 0)
        cc = lax.broadcasted_iota(jnp.int32, (t, t), 1)
        process(k_ref[...].astype(BF16), v_ref[...].astype(BF16), cc <= r)
        lam = _lambda_value(lam_ref, lam_init)
        o1 = acc_ref[0] / _rep(l_ref[0], 2)
        o2 = acc_ref[1] / _rep(l_ref[1], 2)
        o_ref[...] = _subln(o1 - lam * o2, sw_ref[...], lam_init).astype(o_ref.dtype)


def _flash_prompt(q, k, v, lam_p, sw, j_layer, lam_init):
    t = FLASH_T
    nq = SEQ // t
    w2 = 2 * A_HEAD_DIM
    qspec = pl.BlockSpec((t, w2), lambda b, p, qi, kj: (b * nq + qi, p))
    kspec = pl.BlockSpec((t, w2), lambda b, p, qi, kj: (b * nq + jnp.minimum(kj, qi), p))
    mspec = pl.BlockSpec((N_META, w2), lambda b, p, qi, kj: (ROW_META // N_META + b, p))
    return pl.pallas_call(
        functools.partial(_flash_body, lam_init=lam_init),
        grid=(BATCH, A_HEADS, nq, nq),
        in_specs=[qspec, kspec, kspec, mspec, mspec,
                  pl.BlockSpec((None, 4, A_HEAD_DIM), lambda b, p, qi, kj: (j_layer, 0, 0)),
                  pl.BlockSpec((None, 1, w2), lambda b, p, qi, kj: (j_layer, 0, 0))],
        out_specs=pl.BlockSpec((t, w2), lambda b, p, qi, kj: (b * nq + qi, p)),
        out_shape=jax.ShapeDtypeStruct((ROWS_X, D_MODEL), BF16),
        scratch_shapes=[pltpu.VMEM((2, t, LANES), F32), pltpu.VMEM((2, t, LANES), F32),
                        pltpu.VMEM((2, t, w2), F32),
                        pltpu.VMEM((LANES, w2), BF16), pltpu.VMEM((LANES, w2), BF16)],
        compiler_params=_cparams(4), name="flash_prompt",
    )(q, k, v, k, v, lam_p, sw)


def _sattn_body(pt_ref, q_ref, ck_ref, cv_ref, kn_ref, vn_ref, lam_ref, sw_ref, o_ref,
                qf_ref, qb_ref, m_ref, l_ref, acc_ref, kpad_ref, vpad_ref, *, lq, n_pages, lam_init):
    del pt_ref
    dh = A_HEAD_DIM
    n_qk = 2 * A_HEADS
    rows = n_qk * lq
    j = pl.program_id(1)

    @pl.when(j == 0)
    def _init():
        qf_ref[...] = jnp.zeros(qf_ref.shape, F32)
        for hd in range(n_qk):
            qf_ref[hd * lq:(hd + 1) * lq, hd * dh:(hd + 1) * dh] = q_ref[:, hd * dh:(hd + 1) * dh]
        qb_ref[...] = qf_ref[...].astype(BF16)
        m_ref[...] = jnp.full(m_ref.shape, -jnp.inf, F32)
        l_ref[...] = jnp.zeros(l_ref.shape, F32)
        acc_ref[...] = jnp.zeros(acc_ref.shape, F32)

    def process(k_blk, v_blk, mask):
        s = _dot_nt(qb_ref[...], k_blk.astype(BF16))
        if mask is not None:
            s = jnp.where(mask, s, -jnp.inf)
        m_prev = m_ref[...]
        m_new = jnp.maximum(m_prev, jnp.max(s, axis=1, keepdims=True))
        alpha = jnp.exp(m_prev - m_new)
        pmat = jnp.exp(s - m_new)
        l_ref[...] = alpha * l_ref[...] + jnp.sum(pmat, axis=1, keepdims=True)
        m_ref[...] = m_new
        pb = pmat.astype(BF16)
        alpha2 = _rep(alpha, 2)
        for pp in range(A_HEADS):
            rs = slice(2 * pp * lq, (2 * pp + 2) * lq)
            vp = v_blk[:, pp * 2 * dh:(pp + 1) * 2 * dh].astype(BF16)
            acc_ref[rs, :] = alpha2[rs] * acc_ref[rs, :] + _dot(pb[rs], vp)

    @pl.when(j < n_pages)
    def _past_page():
        process(ck_ref[...], cv_ref[...], None)

    @pl.when(j == n_pages)
    def _new_keys():
        kpad_ref[...] = jnp.zeros(kpad_ref.shape, F32)
        vpad_ref[...] = jnp.zeros(vpad_ref.shape, F32)
        kpad_ref[0:lq, :] = kn_ref[...]
        vpad_ref[0:lq, :] = vn_ref[...]
        r = lax.broadcasted_iota(jnp.int32, (rows, LANES), 0)
        cc = lax.broadcasted_iota(jnp.int32, (rows, LANES), 1)
        process(kpad_ref[...], vpad_ref[...], cc <= jnp.bitwise_and(r, lq - 1))
        lam = _lambda_value(lam_ref, lam_init)
        for pp in range(A_HEADS):
            r1 = slice(2 * pp * lq, (2 * pp + 1) * lq)
            r2 = slice((2 * pp + 1) * lq, (2 * pp + 2) * lq)
            o1 = acc_ref[r1, :] / _rep(l_ref[r1, :], 2)
            o2 = acc_ref[r2, :] / _rep(l_ref[r2, :], 2)
            o_ref[:, pp * 2 * dh:(pp + 1) * 2 * dh] = _subln(o1 - lam * o2, sw_ref[...], lam_init).astype(o_ref.dtype)


def _small_attn(q, k, v, cache_k2, cache_v2, page_table, lam_p, sw, j_layer, lam_init,
                *, row0, lq, nb, n_pages, out_dtype):
    rb = row0 // lq
    last = max(n_pages - 1, 0)
    rows = 2 * A_HEADS * lq
    qspec = pl.BlockSpec((lq, D_MODEL), lambda b, j, pt: (rb + b, 0))
    cspec = pl.BlockSpec((None, PAGE_SIZE, D_MODEL), lambda b, j, pt: (pt[b, jnp.minimum(j, last)], 0, 0))
    grid_spec = pltpu.PrefetchScalarGridSpec(
        num_scalar_prefetch=1,
        grid=(nb, n_pages + 1),
        in_specs=[qspec, cspec, cspec, qspec, qspec,
                  pl.BlockSpec((None, 4, A_HEAD_DIM), lambda b, j, pt: (j_layer, 0, 0)),
                  pl.BlockSpec((None, 1, 2 * A_HEAD_DIM), lambda b, j, pt: (j_layer, 0, 0))],
        out_specs=pl.BlockSpec((lq, D_MODEL), lambda b, j, pt: (b, 0)),
        scratch_shapes=[pltpu.VMEM((rows, D_MODEL), F32), pltpu.VMEM((rows, D_MODEL), BF16),
                        pltpu.VMEM((rows, LANES), F32), pltpu.VMEM((rows, LANES), F32),
                        pltpu.VMEM((rows, 2 * A_HEAD_DIM), F32),
                        pltpu.VMEM((PAGE_SIZE, D_MODEL), F32), pltpu.VMEM((PAGE_SIZE, D_MODEL), F32)],
    )
    return pl.pallas_call(
        functools.partial(_sattn_body, lq=lq, n_pages=n_pages, lam_init=lam_init),
        grid_spec=grid_spec,
        out_shape=jax.ShapeDtypeStruct((nb * lq, D_MODEL), out_dtype),
        compiler_params=_cparams(2), name="small_attn",
    )(page_table, q, cache_k2, cache_v2, k, v, lam_p, sw)


def _attn_layer(h, hb, k, v, rope_tabs, cache_k2, cache_v2, page_table, p, layer):
    j_layer = layer - N_A_LAYERS
    lam_init = 0.8 - 0.6 * math.exp(-0.3 * layer)
    q = _matmul(hb, p['a_w_q'], layer=j_layer, n_out=D_MODEL, tn=1024, rope=rope_tabs,
                scale=A_HEAD_DIM ** -0.5)
    lam_p = p['a_lambda'].astype(F32)
    sw = p['a_subln_w'].astype(F32).reshape(-1, 1, 2 * A_HEAD_DIM)
    o_x = _flash_prompt(q, k, v, lam_p, sw, j_layer, lam_init)
    dummy_pt = jnp.zeros((BATCH, 1), jnp.int32)
    o_m = _small_attn(q, k, v, cache_k2, cache_v2, dummy_pt, lam_p, sw, j_layer, lam_init,
                      row0=ROW_META, lq=N_META, nb=BATCH, n_pages=0, out_dtype=BF16)
    o_s = _small_attn(q, k, v, cache_k2, cache_v2, page_table, lam_p, sw, j_layer, lam_init,
                      row0=ROW_SAMPLE, lq=DEC_SEQ, nb=DEC_BATCH, n_pages=page_table.shape[1],
                      out_dtype=F32)
    o = jnp.concatenate([o_x, o_m, o_s.astype(BF16)], axis=0)
    mix = _matmul(o, p['a_w_o'], layer=j_layer, n_out=D_MODEL, tn=1024)
    return _ln_residual(h, mix, p['ln_g'], p['ln_b'], layer, 0)


def _rope_tables():
    pos = jnp.concatenate([
        jnp.tile(N_META + jnp.arange(SEQ, dtype=jnp.int32), BATCH),
        jnp.tile(jnp.arange(N_META, dtype=jnp.int32), BATCH),
        jnp.tile(PAGE_SIZE * 128 + jnp.arange(DEC_SEQ, dtype=jnp.int32), DEC_BATCH)])
    half = A_HEAD_DIM // 2
    inv = ROPE_THETA ** (-jnp.arange(half, dtype=F32) / half)
    ang = pos.astype(F32)[:, None] * inv[None, :]
    cos = jnp.cos(ang)
    sin = jnp.sin(ang)
    return jnp.concatenate([cos, cos], axis=1), jnp.concatenate([-sin, sin], axis=1)


def kernel(x_prompt, x_sample, cache_k, cache_v, page_table, state_ssm, state_conv, meta_tokens, m_w_in, m_conv_w, m_conv_b, m_dt_bias, m_a_log, m_d, m_norm_w, m_w_out, a_w_q, a_w_k, a_w_v, a_lambda, a_subln_w, a_w_o, moe_w_router, moe_b_router, moe_w_gu, moe_b_gu, moe_w_down, moe_b_down, ln_g, ln_b):
    p = dict(m_w_in=m_w_in, m_conv_w=m_conv_w, m_conv_b=m_conv_b, m_dt_bias=m_dt_bias, m_a_log=m_a_log,
             m_d=m_d, m_norm_w=m_norm_w, m_w_out=m_w_out, a_w_q=a_w_q, a_w_k=a_w_k, a_w_v=a_w_v,
             a_lambda=a_lambda, a_subln_w=a_subln_w, a_w_o=a_w_o, moe_w_router=moe_w_router,
             moe_b_router=moe_b_router, moe_w_gu=moe_w_gu, moe_b_gu=moe_b_gu, moe_w_down=moe_w_down,
             moe_b_down=moe_b_down, ln_g=ln_g, ln_b=ln_b)
    n_pool = cache_k.shape[0]
    assert page_table.shape[1] * PAGE_SIZE == PAGE_SIZE * 128
    cache_k2 = cache_k.reshape(n_pool, PAGE_SIZE, D_MODEL)
    cache_v2 = cache_v.reshape(n_pool, PAGE_SIZE, D_MODEL)
    meta = jnp.broadcast_to(meta_tokens[None].astype(F32), (BATCH, N_META, D_MODEL))
    h = jnp.concatenate([x_prompt.reshape(ROWS_X, D_MODEL), meta.reshape(ROWS_META, D_MODEL),
                         x_sample.reshape(ROWS_SAMPLE, D_MODEL)], axis=0)
    hb = h.astype(BF16)
    rope_tabs = _rope_tables()

    ssm_p, conv_p, ssm_s, conv_s = [], [], [], []
    k = v = None
    for layer in range(DEPTH):
        if layer < N_A_LAYERS:
            h, hb, (h_x, c_x, h_s, c_s) = _mamba_layer(h, hb, state_ssm, state_conv, p, layer)
            ssm_p.append(h_x)
            conv_p.append(c_x)
            ssm_s.append(h_s)
            conv_s.append(c_s)
        else:
            if layer == N_A_LAYERS:
                k = _matmul(hb, a_w_k, n_out=D_MODEL, tn=1024, rope=rope_tabs)
                v = _matmul(hb, a_w_v, n_out=D_MODEL, tn=1024)
            h, hb = _attn_layer(h, hb, k, v, rope_tabs, cache_k2, cache_v2, page_table, p, layer)
        h, hb = _moe_layer(h, hb, p, layer)

    def prompt_rows(a):
        return jnp.concatenate([a[ROW_META:ROW_SAMPLE].reshape(BATCH, N_META, D_MODEL),
                                a[:ROWS_X].reshape(BATCH, SEQ, D_MODEL)], axis=1)

    y_prompt = h[:ROWS_X].reshape(BATCH, SEQ, D_MODEL)
    y_sample = h[ROW_SAMPLE:].reshape(DEC_BATCH, DEC_SEQ, D_MODEL)
    k_prompt = prompt_rows(k).reshape(BATCH, SEQ + N_META, 2 * A_HEADS, A_HEAD_DIM)
    v_prompt = prompt_rows(v).reshape(BATCH, SEQ + N_META, A_HEADS, 2 * A_HEAD_DIM)
    k_sample = k[ROW_SAMPLE:].reshape(DEC_BATCH, DEC_SEQ, 2 * A_HEADS, A_HEAD_DIM)
    v_sample = v[ROW_SAMPLE:].reshape(DEC_BATCH, DEC_SEQ, A_HEADS, 2 * A_HEAD_DIM)
    return (y_prompt, y_sample, k_prompt, v_prompt, jnp.stack(ssm_p), jnp.stack(conv_p),
            k_sample, v_sample, jnp.stack(ssm_s), jnp.stack(conv_s))
```

```python
import functools
import math

import jax
import jax.numpy as jnp
from jax import lax
from jax.experimental import pallas as pl
from jax.experimental.pallas import tpu as pltpu

F32 = jnp.float32
BF16 = jnp.bfloat16

D_MODEL = 2048
BATCH = 4
SEQ = 2048
DEPTH = 4
DEC_BATCH = 8
DEC_SEQ = 8
PAGE_SIZE = 128
PAST_LEN = 16384
N_A_LAYERS = DEPTH // 2
N_META = 16
M_D_INNER = 2 * D_MODEL
M_HEADDIM = 64
M_NHEADS = M_D_INNER // M_HEADDIM
M_NGROUPS = 8
M_DSTATE = 128
M_CONV = 4
M_CONV_DIM = M_D_INNER + 2 * M_NGROUPS * M_DSTATE
A_HEADS = 8
A_HEAD_DIM = D_MODEL // (2 * A_HEADS)
ROPE_THETA = 10000.0
N_EXPERTS = 32
TOP_K = 4
D_FF = D_MODEL // 4
SWIGLU_LIMIT = 7.0
SWIGLU_ALPHA = 1.702
DN_ALPHA = (2 * DEPTH) ** 0.25
NORM_EPS = 1e-5

ROWS_X = BATCH * SEQ
ROW_META = ROWS_X
ROWS_META = BATCH * N_META
ROW_SAMPLE = ROW_META + ROWS_META
ROWS_SAMPLE = DEC_BATCH * DEC_SEQ
N_ROWS = ROW_SAMPLE + ROWS_SAMPLE

LANES = 128
VMEM_LIMIT = 56 * 1024 * 1024

SSD_CHUNK = 128
MOE_TM = 256
MOE_BLOCKS = N_ROWS * TOP_K // MOE_TM + N_EXPERTS
MOE_ROWS = MOE_BLOCKS * MOE_TM
MM_TM = 640
LN_TR = 320
FLASH_T = 512


def _cparams(n_axes):
    return pltpu.CompilerParams(dimension_semantics=("arbitrary",) * n_axes,
                                vmem_limit_bytes=VMEM_LIMIT)


def _dot(a, b):
    return jnp.dot(a, b, preferred_element_type=F32)


def _dot_nt(a, b):
    return lax.dot_general(a, b, (((1,), (1,)), ((), ())), preferred_element_type=F32)


def _split2(x):
    hi = x.astype(BF16)
    lo = (x - hi.astype(F32)).astype(BF16)
    return hi, lo


def _split3(x):
    hi = x.astype(BF16)
    r = x - hi.astype(F32)
    mid = r.astype(BF16)
    lo = (r - mid.astype(F32)).astype(BF16)
    return hi, mid, lo


def _mm_body(*refs, rope, scale):
    if rope:
        x_ref, w_ref, cos_ref, sin_ref, o_ref, wbf_ref = refs
    else:
        x_ref, w_ref, o_ref, wbf_ref = refs

    @pl.when(pl.program_id(1) == 0)
    def _cast_weight():
        wbf_ref[...] = w_ref[...].astype(BF16)

    acc = _dot(x_ref[...].astype(BF16), wbf_ref[...])
    if rope:
        cos = cos_ref[...]
        sin = sin_ref[...]
        for j in range(acc.shape[1] // LANES):
            blk = acc[:, j * LANES:(j + 1) * LANES]
            out = (blk * cos + pltpu.roll(blk, LANES // 2, 1) * sin) * scale
            o_ref[:, j * LANES:(j + 1) * LANES] = out.astype(o_ref.dtype)
    else:
        o_ref[...] = acc.astype(o_ref.dtype)


def _matmul(x, w, *, layer=None, col0=0, n_out, tn, out_dtype=F32, rope=None, scale=1.0):
    m, k = x.shape
    tm = MM_TM
    assert m % tm == 0 and n_out % tn == 0 and col0 % tn == 0
    jb = col0 // tn
    if layer is None:
        w_spec = pl.BlockSpec((k, tn), lambda j, i: (0, j + jb))
    else:
        w_spec = pl.BlockSpec((None, k, tn), lambda j, i: (layer, 0, j + jb))
    in_specs = [pl.BlockSpec((tm, k), lambda j, i: (i, 0)), w_spec]
    args = [x, w]
    if rope is not None:
        in_specs += [pl.BlockSpec((tm, LANES), lambda j, i: (i, 0))] * 2
        args += list(rope)
    return pl.pallas_call(
        functools.partial(_mm_body, rope=rope is not None, scale=scale),
        grid=(n_out // tn, m // tm),
        in_specs=in_specs,
        out_specs=pl.BlockSpec((tm, tn), lambda j, i: (i, j)),
        out_shape=jax.ShapeDtypeStruct((m, n_out), out_dtype),
        scratch_shapes=[pltpu.VMEM((k, tn), BF16)],
        compiler_params=_cparams(2),
        name="dense_matmul",
    )(*args)


ROUTE_SUB = 128


def _lane_pack(cols, lane):
    out = jnp.zeros(lane.shape, cols[0].dtype)
    for k, c in enumerate(cols):
        out = jnp.where(lane == k, c, out)
    return out


def _route_body(x_ref, w_ref, b_ref, idx_ref, gate_ref, rank_ref, cnt_ref, run_ref):
    tm = x_ref.shape[0]

    @pl.when(pl.program_id(0) == 0)
    def _init():
        run_ref[...] = jnp.zeros(run_ref.shape, F32)

    xh, xl = _split2(x_ref[...])
    wh, wl = _split2(w_ref[...])
    logits = _dot(xh, wh) + _dot(xh, wl) + _dot(xl, wh) + b_ref[...]
    lane = lax.broadcasted_iota(jnp.int32, (ROUTE_SUB, LANES), 1)
    r_i = lax.broadcasted_iota(jnp.int32, (ROUTE_SUB, ROUTE_SUB), 0)
    c_i = lax.broadcasted_iota(jnp.int32, (ROUTE_SUB, ROUTE_SUB), 1)
    below = jnp.where(r_i > c_i, 1.0, 0.0).astype(BF16)
    running = run_ref[0:1, :]
    for sb in range(tm // ROUTE_SUB):
        rs = slice(sb * ROUTE_SUB, (sb + 1) * ROUTE_SUB)
        rest = logits[rs]
        vals, idxs, hots = [], [], []
        for _ in range(TOP_K):
            mx = jnp.max(rest, axis=1, keepdims=True)
            idx = jnp.min(jnp.where(rest == mx, lane, LANES), axis=1, keepdims=True)
            hot = lane == idx
            rest = jnp.where(hot, -jnp.inf, rest)
            vals.append(mx)
            idxs.append(idx)
            hots.append(hot)
        exps = [jnp.exp(v - vals[0]) for v in vals]
        denom = exps[0] + exps[1] + exps[2] + exps[3]
        idx_ref[rs, :] = _lane_pack(idxs, lane)
        gate_ref[rs, :] = _lane_pack([e / denom for e in exps], lane)
        sel = jnp.zeros((ROUTE_SUB, LANES), F32)
        for hot in hots:
            sel = sel + jnp.where(hot, 1.0, 0.0)
        before = _dot(below, sel.astype(BF16)) + running
        ranks = [jnp.sum(jnp.where(hot, before, 0.0), axis=1, keepdims=True) for hot in hots]
        rank_ref[rs, :] = _lane_pack(ranks, lane).astype(jnp.int32)
        running = running + jnp.sum(sel, axis=0, keepdims=True)
    run_ref[...] = jnp.broadcast_to(running, run_ref.shape)
    cnt_ref[...] = jnp.broadcast_to(running, cnt_ref.shape)


def _route(h, w_router, b_router, layer):
    tm = MM_TM
    w_pad = jnp.pad(w_router, ((0, 0), (0, 0), (0, LANES - N_EXPERTS)))
    b_pad = jnp.pad(b_router.astype(F32), ((0, 0), (0, LANES - N_EXPERTS)),
                    constant_values=-1e30).reshape(DEPTH, 1, LANES)
    row = pl.BlockSpec((tm, LANES), lambda i: (i, 0))
    return pl.pallas_call(
        _route_body,
        grid=(N_ROWS // tm,),
        in_specs=[pl.BlockSpec((tm, D_MODEL), lambda i: (i, 0)),
                  pl.BlockSpec((None, D_MODEL, LANES), lambda i: (layer, 0, 0)),
                  pl.BlockSpec((None, 1, LANES), lambda i: (layer, 0, 0))],
        out_specs=(row, row, row, pl.BlockSpec((8, LANES), lambda i: (0, 0))),
        out_shape=(jax.ShapeDtypeStruct((N_ROWS, LANES), jnp.int32),
                   jax.ShapeDtypeStruct((N_ROWS, LANES), F32),
                   jax.ShapeDtypeStruct((N_ROWS, LANES), jnp.int32),
                   jax.ShapeDtypeStruct((8, LANES), F32)),
        scratch_shapes=[pltpu.VMEM((8, LANES), F32)],
        compiler_params=_cparams(1),
        name="moe_route",
    )(h, w_pad, b_pad)


def _layer_norm_rows(x, g, b):
    mu = jnp.mean(x, axis=-1, keepdims=True)
    xc = x - mu
    var = jnp.mean(xc * xc, axis=-1, keepdims=True)
    return xc * lax.rsqrt(var + NORM_EPS) * g + b


def _ln_body(h_ref, m_ref, g_ref, b_ref, o_ref):
    o_ref[...] = _layer_norm_rows(DN_ALPHA * h_ref[...] + m_ref[...], g_ref[...], b_ref[...])


def _ln_residual(h, mix, ln_g, ln_b, layer):
    tr = LN_TR
    row = pl.BlockSpec((tr, D_MODEL), lambda i: (i, 0))
    par = pl.BlockSpec((None, None, 1, D_MODEL), lambda i: (layer, 0, 0, 0))
    return pl.pallas_call(
        _ln_body, grid=(N_ROWS // tr,),
        in_specs=[row, row, par, par], out_specs=row,
        out_shape=jax.ShapeDtypeStruct((N_ROWS, D_MODEL), F32),
        compiler_params=_cparams(1), name="ln_residual",
    )(h, mix, ln_g.reshape(DEPTH, 2, 1, D_MODEL), ln_b.reshape(DEPTH, 2, 1, D_MODEL))


MOE_TB = 128
MOE_NBLK = N_ROWS // MOE_TB
MOE_IDX = MOE_TB * TOP_K


def _idx_copy(dest_hbm, idx_smem, isem, blk, slot):
    return pltpu.make_async_copy(dest_hbm.at[blk], idx_smem.at[slot], isem.at[slot])


def _dispatch_body(pstart_ref, pcnt_ref, nused_ref, dest_hbm, h_ref, xs_hbm, idx_smem, zero_ref,
                   isem, rsem, psem, tsem):
    i = pl.program_id(0)
    slot = i % 2

    @pl.when(i == 0)
    def _first():
        _idx_copy(dest_hbm, idx_smem, isem, 0, 0).start()
        zero_ref[...] = jnp.zeros(zero_ref.shape, F32)

        def tail_copy(blk):
            return pltpu.make_async_copy(
                zero_ref, xs_hbm.at[pl.ds(pl.multiple_of(blk * MOE_TM, MOE_TM), MOE_TM)], tsem)

        def tail_start(blk, carry):
            tail_copy(blk).start()
            return carry

        def tail_wait(blk, carry):
            tail_copy(blk).wait()
            return carry

        lax.fori_loop(nused_ref[0], MOE_BLOCKS, tail_start, 0)
        total = jnp.int32(0)
        for e in range(N_EXPERTS):
            first = pstart_ref[e]
            cnt = pcnt_ref[e]

            def pad_row(j, carry, first=first):
                pltpu.make_async_copy(zero_ref.at[pl.ds(0, 1)], xs_hbm.at[pl.ds(first + j, 1)], psem).start()
                return carry

            lax.fori_loop(0, cnt, pad_row, 0)
            total = total + cnt

        def wait_pad(j, carry):
            pltpu.make_async_copy(zero_ref.at[pl.ds(0, 1)], xs_hbm.at[pl.ds(0, 1)], psem).wait()
            return carry

        lax.fori_loop(0, total, wait_pad, 0)
        lax.fori_loop(nused_ref[0], MOE_BLOCKS, tail_wait, 0)

    _idx_copy(dest_hbm, idx_smem, isem, i, slot).wait()

    @pl.when(i + 1 < MOE_NBLK)
    def _next_idx():
        _idx_copy(dest_hbm, idx_smem, isem, i + 1, 1 - slot).start()

    for r in range(MOE_TB):
        for k in range(TOP_K):
            d = idx_smem[slot, r * TOP_K + k]
            pltpu.make_async_copy(h_ref.at[pl.ds(r, 1)], xs_hbm.at[pl.ds(d, 1)], rsem).start()
    pltpu.make_async_copy(xs_hbm.at[pl.ds(0, MOE_IDX)], xs_hbm.at[pl.ds(0, MOE_IDX)], rsem).wait()


def _dispatch(h, dest2, pad_start, pad_cnt, n_used):
    grid_spec = pltpu.PrefetchScalarGridSpec(
        num_scalar_prefetch=3,
        grid=(MOE_NBLK,),
        in_specs=[pl.BlockSpec(memory_space=pl.ANY),
                  pl.BlockSpec((MOE_TB, D_MODEL), lambda i, ps, pc, nu: (i, 0))],
        out_specs=pl.BlockSpec(memory_space=pl.ANY),
        scratch_shapes=[pltpu.SMEM((2, MOE_IDX), jnp.int32), pltpu.VMEM((MOE_TM, D_MODEL), F32),
                        pltpu.SemaphoreType.DMA((2,)), pltpu.SemaphoreType.DMA(()),
                        pltpu.SemaphoreType.DMA(()), pltpu.SemaphoreType.DMA(())],
    )
    return pl.pallas_call(
        _dispatch_body, grid_spec=grid_spec,
        out_shape=jax.ShapeDtypeStruct((MOE_ROWS, D_MODEL), F32),
        compiler_params=_cparams(1), name="moe_dispatch",
    )(pad_start, pad_cnt, n_used, dest2, h)


def _combine_body(dest_hbm, eo_hbm, h_ref, gate_ref, g_ref, b_ref, *rest, with_bf16):
    if with_bf16:
        o_ref, obf_ref, idx_smem, ebuf, isem, rsem = rest
    else:
        o_ref, idx_smem, ebuf, isem, rsem = rest
    i = pl.program_id(0)
    slot = i % 2

    def issue_rows(s):
        for r in range(MOE_TB):
            for k in range(TOP_K):
                d = idx_smem[s, r * TOP_K + k]
                pltpu.make_async_copy(eo_hbm.at[pl.ds(d, 1)], ebuf.at[s, k, pl.ds(r, 1)], rsem.at[s]).start()

    @pl.when(i == 0)
    def _prime():
        first = _idx_copy(dest_hbm, idx_smem, isem, 0, 0)
        first.start()
        first.wait()
        issue_rows(0)
        _idx_copy(dest_hbm, idx_smem, isem, 1, 1).start()

    @pl.when(i + 1 < MOE_NBLK)
    def _prefetch():
        _idx_copy(dest_hbm, idx_smem, isem, i + 1, 1 - slot).wait()
        issue_rows(1 - slot)

        @pl.when(i + 2 < MOE_NBLK)
        def _next_idx():
            _idx_copy(dest_hbm, idx_smem, isem, i + 2, slot).start()

    pltpu.make_async_copy(ebuf.at[slot], ebuf.at[slot], rsem.at[slot]).wait()
    gates = gate_ref[...]
    f = ebuf[slot, 0] * gates[:, 0:1]
    for k in range(1, TOP_K):
        f = f + ebuf[slot, k] * gates[:, k:k + 1]
    y = _layer_norm_rows(DN_ALPHA * h_ref[...] + f, g_ref[...], b_ref[...])
    o_ref[...] = y
    if with_bf16:
        obf_ref[...] = y.astype(BF16)


def _ln_moe(h, eo, dest2, gates, ln_g, ln_b, layer, with_bf16):
    row = pl.BlockSpec((MOE_TB, D_MODEL), lambda i: (i, 0))
    par = pl.BlockSpec((None, None, 1, D_MODEL), lambda i: (layer, 1, 0, 0))
    out_shape = [jax.ShapeDtypeStruct((N_ROWS, D_MODEL), F32)]
    if with_bf16:
        out_shape.append(jax.ShapeDtypeStruct((N_ROWS, D_MODEL), BF16))
    outs = pl.pallas_call(
        functools.partial(_combine_body, with_bf16=with_bf16),
        grid=(MOE_NBLK,),
        in_specs=[pl.BlockSpec(memory_space=pl.ANY), pl.BlockSpec(memory_space=pl.ANY), row,
                  pl.BlockSpec((MOE_TB, LANES), lambda i: (i, 0)), par, par],
        out_specs=tuple([row] * len(out_shape)),
        out_shape=tuple(out_shape),
        scratch_shapes=[pltpu.SMEM((2, MOE_IDX), jnp.int32),
                        pltpu.VMEM((2, TOP_K, MOE_TB, D_MODEL), F32),
                        pltpu.SemaphoreType.DMA((2,)), pltpu.SemaphoreType.DMA((2,))],
        compiler_params=_cparams(1), name="ln_moe_combine",
    )(dest2, eo, h, gates, ln_g.reshape(DEPTH, 2, 1, D_MODEL), ln_b.reshape(DEPTH, 2, 1, D_MODEL))
    return outs if with_bf16 else (outs[0], None)


def _moe_body(be_ref, nb_ref, x_ref, wgu_ref, bgu_ref, wd_ref, bd_ref, o_ref, wgu_bf, wd_bf):
    i = pl.program_id(0)
    e = be_ref[i]
    prev = be_ref[jnp.maximum(i - 1, 0)]

    @pl.when(jnp.logical_or(i == 0, e != prev))
    def _cast_weights():
        wgu_bf[...] = wgu_ref[...].astype(BF16)
        wd_bf[...] = wd_ref[...].astype(BF16)

    @pl.when(i < nb_ref[0])
    def _compute():
        gu = _dot(x_ref[...].astype(BF16), wgu_bf[...]) + bgu_ref[...]
        gate = jnp.minimum(gu[:, :D_FF], SWIGLU_LIMIT)
        up = jnp.clip(gu[:, D_FF:], -SWIGLU_LIMIT, SWIGLU_LIMIT)
        act = (up + 1.0) * gate * jax.nn.sigmoid(SWIGLU_ALPHA * gate)
        o_ref[...] = _dot(act.astype(BF16), wd_bf[...]) + bd_ref[...]

    @pl.when(i >= nb_ref[0])
    def _unused_block():
        o_ref[...] = jnp.zeros(o_ref.shape, o_ref.dtype)


def _moe_experts(xs, blk_e, n_used, w_gu, b_gu, w_down, b_down, layer):
    grid_spec = pltpu.PrefetchScalarGridSpec(
        num_scalar_prefetch=2,
        grid=(MOE_BLOCKS,),
        in_specs=[
            pl.BlockSpec((MOE_TM, D_MODEL), lambda i, be, nb: (jnp.minimum(i, nb[0] - 1), 0)),
            pl.BlockSpec((None, None, D_MODEL, 2 * D_FF), lambda i, be, nb: (layer, be[i], 0, 0)),
            pl.BlockSpec((None, None, 1, 2 * D_FF), lambda i, be, nb: (layer, be[i], 0, 0)),
            pl.BlockSpec((None, None, D_FF, D_MODEL), lambda i, be, nb: (layer, be[i], 0, 0)),
            pl.BlockSpec((None, None, 1, D_MODEL), lambda i, be, nb: (layer, be[i], 0, 0)),
        ],
        out_specs=pl.BlockSpec((MOE_TM, D_MODEL), lambda i, be, nb: (i, 0)),
        scratch_shapes=[pltpu.VMEM((D_MODEL, 2 * D_FF), BF16), pltpu.VMEM((D_FF, D_MODEL), BF16)],
    )
    return pl.pallas_call(
        _moe_body, grid_spec=grid_spec,
        out_shape=jax.ShapeDtypeStruct((MOE_ROWS, D_MODEL), F32),
        compiler_params=_cparams(1), name="moe_experts",
    )(blk_e, n_used, xs, w_gu, b_gu.reshape(DEPTH, N_EXPERTS, 1, 2 * D_FF),
      w_down, b_down.reshape(DEPTH, N_EXPERTS, 1, D_MODEL))


def _moe_layer(h, p, layer, with_bf16):
    idx, gates, rank, cnt = _route(h, p['moe_w_router'], p['moe_b_router'], layer)
    counts = cnt[0, :N_EXPERTS].astype(jnp.int32)
    psz = (counts + MOE_TM - 1) // MOE_TM * MOE_TM
    pend = jnp.cumsum(psz)
    pstart = pend - psz
    expert_ids = jnp.arange(N_EXPERTS, dtype=jnp.int32)
    top_i = idx[:, :TOP_K]
    base = jnp.sum(jnp.where(top_i[:, :, None] == expert_ids, pstart, 0), axis=-1)
    dest2 = (base + rank[:, :TOP_K]).astype(jnp.int32).reshape(MOE_NBLK, MOE_IDX)
    n_used = (pend[-1] // MOE_TM).astype(jnp.int32)
    blk = jnp.arange(MOE_BLOCKS, dtype=jnp.int32)
    blk_e = jnp.sum((blk[:, None] * MOE_TM >= pend[None, :]).astype(jnp.int32), axis=1)
    blk_e = jnp.minimum(blk_e, N_EXPERTS - 1)
    blk_e = jnp.where(blk < n_used, blk_e, blk_e[jnp.maximum(n_used - 1, 0)]).astype(jnp.int32)
    n_used = n_used.reshape(1)
    xs = _dispatch(h, dest2, (pstart + counts).astype(jnp.int32), (psz - counts).astype(jnp.int32), n_used)
    eo = _moe_experts(xs, blk_e, n_used, p['moe_w_gu'], p['moe_b_gu'],
                      p['moe_w_down'], p['moe_b_down'], layer)
    return _ln_moe(h, eo, dest2, gates, p['ln_g'], p['ln_b'], layer, with_bf16)


def _ssd_body(z_ref, xbc_ref, dt_ref, h0_ref, c0_ref, cw_ref, cb_ref, dtb_ref, alog_ref, de_ref, nw_ref,
              eexp_ref, y_ref, hfin_ref, cfin_ref, ht_ref, ext_ref, act_ref, ybuf_ref, *, lv, nc):
    L = SSD_CHUNK
    N = M_DSTATE
    c = pl.program_id(1)
    n_pairs = M_NHEADS // 2
    x_cols = M_D_INNER
    b_col0 = M_D_INNER
    c_col0 = M_D_INNER + M_NGROUPS * M_DSTATE

    @pl.when(c == 0)
    def _init():
        for i in range(n_pairs):
            ht_ref[i] = h0_ref[2 * i:2 * i + 2].reshape(2 * M_HEADDIM, N).T
        ext_ref[0:8, :] = jnp.zeros((8, M_CONV_DIM), F32)
        ext_ref[5:8, :] = c0_ref[...]

    raw = xbc_ref[...]
    if lv < L:
        raw = jnp.concatenate([raw, jnp.zeros((L - lv, M_CONV_DIM), F32)], axis=0)
    ext_ref[8:8 + L, :] = raw
    cw_step = 256
    for j in range(M_CONV_DIM // cw_step):
        cs_ = slice(j * cw_step, (j + 1) * cw_step)
        acc = cb_ref[:, cs_] + cw_ref[0:1, cs_] * ext_ref[5:5 + L, cs_]
        acc = acc + cw_ref[1:2, cs_] * ext_ref[6:6 + L, cs_]
        acc = acc + cw_ref[2:3, cs_] * ext_ref[7:7 + L, cs_]
        acc = acc + cw_ref[3:4, cs_] * ext_ref[8:8 + L, cs_]
        act_ref[:, cs_] = jax.nn.silu(acc)
    carry = ext_ref[5 + lv:8 + lv, :]
    ext_ref[5:8, :] = carry
    cfin_ref[...] = carry

    dtr = dt_ref[...]
    if lv < L:
        dtr = jnp.concatenate([dtr, jnp.zeros((L - lv, LANES), F32)], axis=0)
    dt = jax.nn.softplus(dtr + dtb_ref[...])
    row = lax.broadcasted_iota(jnp.int32, (L, L), 0)
    col = lax.broadcasted_iota(jnp.int32, (L, L), 1)
    if lv < L:
        dt = jnp.where(row < lv, dt, 0.0)
    da = dt * (-jnp.exp(alog_ref[...]))
    causal = row >= col
    tri = jnp.where(causal, 1.0, 0.0).astype(BF16)
    d_hi, d_mid, d_lo = _split3(da)
    cs = _dot(tri, d_hi) + _dot(tri, d_mid) + _dot(tri, d_lo)
    cs_end = cs[L - 1:L, :]
    cst = cs.T
    dtt = dt.T
    wt = (dt * jnp.exp(cs_end - cs)).T
    e_hi, e_lo = _split2(jnp.broadcast_to(jnp.exp(cs_end), (8, LANES)))
    dend = _dot(e_hi, eexp_ref[...]) + _dot(e_lo, eexp_ref[...])
    lane = lax.broadcasted_iota(jnp.int32, (L, LANES), 1)
    first_head = lane < M_HEADDIM

    for g in range(M_NGROUPS):
        bg = act_ref[:, b_col0 + g * N:b_col0 + (g + 1) * N]
        cg = act_ref[:, c_col0 + g * N:c_col0 + (g + 1) * N]
        btg = bg.T
        cb = _dot(cg.astype(BF16), btg.astype(BF16))
        for pr in range(M_NHEADS // M_NGROUPS // 2):
            i = g * (M_NHEADS // M_NGROUPS // 2) + pr
            ps = slice(i * LANES, (i + 1) * LANES)
            xs_pair = act_ref[:, ps]
            ht_pair = ht_ref[i]
            lhs_rows = []
            bts_rows = []
            for j in range(2):
                hd = 2 * i + j
                colb = jnp.broadcast_to(cs[:, hd:hd + 1], (L, L))
                seg = colb - cst[hd:hd + 1, :]
                decay = jnp.exp(jnp.where(causal, seg, -jnp.inf))
                m_h = cb * decay * dtt[hd:hd + 1, :]
                c_h = cg * jnp.exp(colb)
                lhs_rows.append(jnp.concatenate([m_h, c_h], axis=1))
                bts_rows.append(btg * wt[hd:hd + 1, :])
            lhs = jnp.concatenate(lhs_rows, axis=0).astype(BF16)
            rhs = jnp.concatenate([xs_pair, ht_pair], axis=0).astype(BF16)
            out = _dot(lhs, rhs)
            y_pair = jnp.where(first_head, out[:L], out[L:]) + xs_pair * de_ref[:, ps]
            ybuf_ref[:, ps] = y_pair
            bts = jnp.concatenate(bts_rows, axis=0).astype(BF16)
            upd = _dot(bts, xs_pair.astype(BF16))
            ht_ref[i] = dend[0:1, ps] * ht_pair + jnp.where(first_head, upd[:N], upd[N:])

    gw = M_D_INNER // M_NGROUPS
    for g in range(M_NGROUPS):
        gs = slice(g * gw, (g + 1) * gw)
        yg = ybuf_ref[0:lv, gs] * jax.nn.silu(z_ref[:, gs])
        ms = jnp.mean(yg * yg, axis=-1, keepdims=True)
        y_ref[:, gs] = (yg * lax.rsqrt(ms + NORM_EPS) * nw_ref[:, gs]).astype(y_ref.dtype)

    @pl.when(c == nc - 1)
    def _final():
        for i in range(n_pairs):
            hfin_ref[2 * i:2 * i + 2] = ht_ref[i].T.reshape(2, M_HEADDIM, N)


def _ssd(z, xbc, dtr, h0, h0_layer, c0, c0_layer, prm, *, row0, lv, nb, nc, out_dtype):
    rb = row0 // lv

    def rows(width):
        return pl.BlockSpec((lv, width), lambda b, c: (rb + b * nc + c, 0))

    def par(shape):
        return pl.BlockSpec(shape, lambda b, c: (0, 0))

    if h0_layer is None:
        h0_spec = pl.BlockSpec((None, M_NHEADS, M_HEADDIM, M_DSTATE), lambda b, c: (b, 0, 0, 0))
        c0_spec = pl.BlockSpec((None, M_CONV - 1, M_CONV_DIM), lambda b, c: (b, 0, 0))
    else:
        h0_spec = pl.BlockSpec((None, None, M_NHEADS, M_HEADDIM, M_DSTATE),
                               lambda b, c: (h0_layer, b, 0, 0, 0))
        c0_spec = pl.BlockSpec((None, None, M_CONV - 1, M_CONV_DIM), lambda b, c: (c0_layer, b, 0, 0))
    out_rows = nb * nc * lv
    return pl.pallas_call(
        functools.partial(_ssd_body, lv=lv, nc=nc),
        grid=(nb, nc),
        in_specs=[rows(M_D_INNER), rows(M_CONV_DIM), rows(LANES), h0_spec, c0_spec,
                  par((M_CONV, M_CONV_DIM)), par((1, M_CONV_DIM)), par((1, LANES)), par((1, LANES)),
                  par((1, M_D_INNER)), par((1, M_D_INNER)), par((LANES, M_D_INNER))],
        out_specs=(pl.BlockSpec((lv, M_D_INNER), lambda b, c: (b * nc + c, 0)),
                   pl.BlockSpec((None, M_NHEADS, M_HEADDIM, M_DSTATE), lambda b, c: (b, 0, 0, 0)),
                   pl.BlockSpec((None, M_CONV - 1, M_CONV_DIM), lambda b, c: (b, 0, 0))),
        out_shape=(jax.ShapeDtypeStruct((out_rows, M_D_INNER), out_dtype),
                   jax.ShapeDtypeStruct((nb, M_NHEADS, M_HEADDIM, M_DSTATE), F32),
                   jax.ShapeDtypeStruct((nb, M_CONV - 1, M_CONV_DIM), F32)),
        scratch_shapes=[pltpu.VMEM((M_NHEADS // 2, M_DSTATE, LANES), F32),
                        pltpu.VMEM((8 + SSD_CHUNK, M_CONV_DIM), F32),
                        pltpu.VMEM((SSD_CHUNK, M_CONV_DIM), F32),
                        pltpu.VMEM((SSD_CHUNK, M_D_INNER), F32)],
        compiler_params=_cparams(2), name="ssd_mixer",
    )(z, xbc, dtr, h0, c0, *prm)


def _mamba_layer(h, hb, state_ssm, state_conv, p, layer):
    w_in = p['m_w_in']
    z = _matmul(hb, w_in, layer=layer, col0=0, n_out=M_D_INNER, tn=1024)
    xbc = _matmul(hb, w_in, layer=layer, col0=M_D_INNER, n_out=M_CONV_DIM, tn=1024)
    w_dt = jnp.pad(w_in[layer, :, M_D_INNER + M_CONV_DIM:], ((0, 0), (0, LANES - M_NHEADS)))
    dtr = _matmul(hb, w_dt, n_out=LANES, tn=LANES)

    def pad_heads(v):
        return jnp.pad(v[layer].astype(F32), (0, LANES - M_NHEADS)).reshape(1, LANES)

    head_of_col = jnp.arange(M_D_INNER, dtype=jnp.int32) // M_HEADDIM
    expand = (jnp.arange(LANES, dtype=jnp.int32)[:, None] == head_of_col[None, :]).astype(BF16)
    prm = (p['m_conv_w'][layer], p['m_conv_b'][layer].reshape(1, M_CONV_DIM),
           pad_heads(p['m_dt_bias']), pad_heads(p['m_a_log']),
           jnp.repeat(p['m_d'][layer].astype(F32), M_HEADDIM).reshape(1, M_D_INNER),
           p['m_norm_w'][layer].reshape(1, M_D_INNER), expand)
    zeros_h = jnp.zeros((BATCH, M_NHEADS, M_HEADDIM, M_DSTATE), F32)
    zeros_c = jnp.zeros((BATCH, M_CONV - 1, M_CONV_DIM), F32)
    y_m, h_m, c_m = _ssd(z, xbc, dtr, zeros_h, None, zeros_c, None, prm,
                         row0=ROW_META, lv=N_META, nb=BATCH, nc=1, out_dtype=BF16)
    y_x, h_x, c_x = _ssd(z, xbc, dtr, h_m, None, c_m, None, prm,
                         row0=0, lv=SSD_CHUNK, nb=BATCH, nc=SEQ // SSD_CHUNK, out_dtype=BF16)
    y_s, h_s, c_s = _ssd(z, xbc, dtr, state_ssm, layer, state_conv, layer, prm,
                         row0=ROW_SAMPLE, lv=DEC_SEQ, nb=DEC_BATCH, nc=1, out_dtype=F32)
    yn = jnp.concatenate([y_x, y_m, y_s.astype(BF16)], axis=0)
    mix = _matmul(yn, p['m_w_out'], layer=layer, n_out=D_MODEL, tn=512)
    h = _ln_residual(h, mix, p['ln_g'], p['ln_b'], layer)
    return h, (h_x, c_x, h_s, c_s)


def _lambda_value(lam_ref, lam_init):
    lp = lam_ref[...]
    s1 = jnp.sum(lp[0:1, :] * lp[1:2, :], axis=-1, keepdims=True)
    s2 = jnp.sum(lp[2:3, :] * lp[3:4, :], axis=-1, keepdims=True)
    return jnp.exp(s1) - jnp.exp(s2) + lam_init


def _subln(o, sw, lam_init):
    ms = jnp.mean(o * o, axis=-1, keepdims=True)
    return o * lax.rsqrt(ms + NORM_EPS) * sw * (1.0 - lam_init)


def _rep(x, n):
    return x if n == 1 else jnp.concatenate([x] * n, axis=1)


def _flash_body(q_ref, k_ref, v_ref, km_ref, vm_ref, lam_ref, sw_ref, o_ref,
                m_ref, l_ref, acc_ref, kpad_ref, vpad_ref, *, lam_init):
    t = FLASH_T
    dh = A_HEAD_DIM
    qi = pl.program_id(2)
    kj = pl.program_id(3)

    def process(kb, vb, mask):
        for j in range(2):
            qj = q_ref[:, j * dh:(j + 1) * dh].astype(BF16)
            s = _dot_nt(qj, kb[:, j * dh:(j + 1) * dh])
            if mask is not None:
                s = jnp.where(mask, s, -jnp.inf)
            m_prev = m_ref[j]
            m_new = jnp.maximum(m_prev, jnp.max(s, axis=1, keepdims=True))
            alpha = jnp.exp(m_prev - m_new)
            pmat = jnp.exp(s - _rep(m_new, s.shape[1] // LANES))
            l_ref[j] = alpha * l_ref[j] + jnp.sum(pmat, axis=1, keepdims=True)
            acc_ref[j] = _rep(alpha, 2) * acc_ref[j] + _dot(pmat.astype(BF16), vb)
            m_ref[j] = m_new

    @pl.when(kj == 0)
    def _start():
        m_ref[...] = jnp.full(m_ref.shape, -jnp.inf, F32)
        l_ref[...] = jnp.zeros(l_ref.shape, F32)
        acc_ref[...] = jnp.zeros(acc_ref.shape, F32)
        kpad_ref[...] = jnp.zeros(kpad_ref.shape, BF16)
        vpad_ref[...] = jnp.zeros(vpad_ref.shape, BF16)
        kpad_ref[0:N_META, :] = km_ref[...].astype(BF16)
        vpad_ref[0:N_META, :] = vm_ref[...].astype(BF16)
        colm = lax.broadcasted_iota(jnp.int32, (t, LANES), 1)
        process(kpad_ref[...], vpad_ref[...], colm < N_META)

    @pl.when(kj < qi)
    def _full_block():
        process(k_ref[...].astype(BF16), v_ref[...].astype(BF16), None)

    @pl.when(kj == qi)
    def _diag_block():
        r = lax.broadcasted_iota(jnp.int32, (t, t), 0)
        cc = lax.broadcasted_iota(jnp.int32, (t, t), 1)
        process(k_ref[...].astype(BF16), v_ref[...].astype(BF16), cc <= r)
        lam = _lambda_value(lam_ref, lam_init)
        o1 = acc_ref[0] / _rep(l_ref[0], 2)
        o2 = acc_ref[1] / _rep(l_ref[1], 2)
        o_ref[...] = _subln(o1 - lam * o2, sw_ref[...], lam_init).astype(o_ref.dtype)


def _flash_prompt(q, k, v, lam_p, sw, j_layer, lam_init):
    t = FLASH_T
    nq = SEQ // t
    w2 = 2 * A_HEAD_DIM
    qspec = pl.BlockSpec((t, w2), lambda b, p, qi, kj: (b * nq + qi, p))
    kspec = pl.BlockSpec((t, w2), lambda b, p, qi, kj: (b * nq + jnp.minimum(kj, qi), p))
    mspec = pl.BlockSpec((N_META, w2), lambda b, p, qi, kj: (ROW_META // N_META + b, p))
    return pl.pallas_call(
        functools.partial(_flash_body, lam_init=lam_init),
        grid=(BATCH, A_HEADS, nq, nq),
        in_specs=[qspec, kspec, kspec, mspec, mspec,
                  pl.BlockSpec((None, 4, A_HEAD_DIM), lambda b, p, qi, kj: (j_layer, 0, 0)),
                  pl.BlockSpec((None, 1, w2), lambda b, p, qi, kj: (j_layer, 0, 0))],
        out_specs=pl.BlockSpec((t, w2), lambda b, p, qi, kj: (b * nq + qi, p)),
        out_shape=jax.ShapeDtypeStruct((ROWS_X, D_MODEL), BF16),
        scratch_shapes=[pltpu.VMEM((2, t, LANES), F32), pltpu.VMEM((2, t, LANES), F32),
                        pltpu.VMEM((2, t, w2), F32),
                        pltpu.VMEM((LANES, w2), BF16), pltpu.VMEM((LANES, w2), BF16)],
        compiler_params=_cparams(4), name="flash_prompt",
    )(q, k, v, k, v, lam_p, sw)


SATTN_PG = 4


def _sattn_body(pt_ref, q_ref, *refs, lq, n_steps, pg, lam_init):
    del pt_ref
    ck_refs = refs[:pg]
    cv_refs = refs[pg:3 * pg]
    (kn_ref, vn_ref, lam_ref, sw_ref, o_ref, qf_ref, qb_ref, m_ref, l_ref, acc_ref,
     kb_ref, vb_ref) = refs[3 * pg:]
    dh = A_HEAD_DIM
    n_qk = 2 * A_HEADS
    rows = n_qk * lq
    j = pl.program_id(1)

    @pl.when(j == 0)
    def _init():
        qf_ref[...] = jnp.zeros(qf_ref.shape, F32)
        for hd in range(n_qk):
            qf_ref[hd * lq:(hd + 1) * lq, hd * dh:(hd + 1) * dh] = q_ref[:, hd * dh:(hd + 1) * dh]
        qb_ref[...] = qf_ref[...].astype(BF16)
        m_ref[...] = jnp.full(m_ref.shape, -jnp.inf, F32)
        l_ref[...] = jnp.zeros(l_ref.shape, F32)
        acc_ref[...] = jnp.zeros(acc_ref.shape, F32)

    def process(tk, mask):
        s = _dot_nt(qb_ref[...], kb_ref[0:tk, :])
        if mask is not None:
            s = jnp.where(mask, s, -jnp.inf)
        m_prev = m_ref[...]
        m_new = jnp.maximum(m_prev, jnp.max(s, axis=1, keepdims=True))
        alpha = jnp.exp(m_prev - m_new)
        pmat = jnp.exp(s - _rep(m_new, tk // LANES))
        l_ref[...] = alpha * l_ref[...] + jnp.sum(pmat, axis=1, keepdims=True)
        m_ref[...] = m_new
        pb = pmat.astype(BF16)
        alpha2 = _rep(alpha, 2)
        for pp in range(A_HEADS):
            rs = slice(2 * pp * lq, (2 * pp + 2) * lq)
            acc_ref[rs, :] = alpha2[rs] * acc_ref[rs, :] + _dot(pb[rs], vb_ref[0:tk, pp * 2 * dh:(pp + 1) * 2 * dh])

    @pl.when(j < n_steps)
    def _past_pages():
        for g in range(pg):
            ts = slice(g * PAGE_SIZE, (g + 1) * PAGE_SIZE)
            for hd in range(n_qk):
                kb_ref[ts, hd * dh:(hd + 1) * dh] = ck_refs[g][pl.ds(hd, PAGE_SIZE, stride=n_qk), :].astype(BF16)
            for hv in range(A_HEADS):
                for half in range(2):
                    c0 = hv * 2 * dh + half * LANES
                    vb_ref[ts, c0:c0 + LANES] = (
                        cv_refs[2 * g + half][pl.ds(hv, PAGE_SIZE, stride=A_HEADS), :].astype(BF16))
        process(pg * PAGE_SIZE, None)

    @pl.when(j == n_steps)
    def _new_keys():
        pad = jnp.zeros((PAGE_SIZE - lq, D_MODEL), F32)
        kb_ref[0:PAGE_SIZE, :] = jnp.concatenate([kn_ref[...], pad], axis=0).astype(BF16)
        vb_ref[0:PAGE_SIZE, :] = jnp.concatenate([vn_ref[...], pad], axis=0).astype(BF16)
        r = lax.broadcasted_iota(jnp.int32, (rows, LANES), 0)
        cc = lax.broadcasted_iota(jnp.int32, (rows, LANES), 1)
        process(PAGE_SIZE, cc <= jnp.bitwise_and(r, lq - 1))
        lam = _lambda_value(lam_ref, lam_init)
        for pp in range(A_HEADS):
            r1 = slice(2 * pp * lq, (2 * pp + 1) * lq)
            r2 = slice((2 * pp + 1) * lq, (2 * pp + 2) * lq)
            o1 = acc_ref[r1, :] / _rep(l_ref[r1, :], 2)
            o2 = acc_ref[r2, :] / _rep(l_ref[r2, :], 2)
            o_ref[:, pp * 2 * dh:(pp + 1) * 2 * dh] = _subln(o1 - lam * o2, sw_ref[...], lam_init).astype(o_ref.dtype)


def _small_attn(q, k, v, cache_k, cache_v, page_table, lam_p, sw, j_layer, lam_init,
                *, row0, lq, nb, n_pages, out_dtype):
    rb = row0 // lq
    pg = SATTN_PG if n_pages > 0 else 1
    assert n_pages % pg == 0
    n_steps = n_pages // pg
    last = max(n_steps - 1, 0)
    rows = 2 * A_HEADS * lq
    qspec = pl.BlockSpec((lq, D_MODEL), lambda b, j, pt: (rb + b, 0))

    n_pool = cache_k.shape[0]
    cache_k2 = cache_k.reshape(n_pool, PAGE_SIZE * 2 * A_HEADS, A_HEAD_DIM)
    cache_v2 = cache_v.reshape(n_pool, PAGE_SIZE * A_HEADS, 2 * A_HEAD_DIM)

    def page_spec(g, n_rows, half):
        return pl.BlockSpec((None, n_rows, LANES),
                            lambda b, j, pt: (pt[b, jnp.minimum(j, last) * pg + g], 0, half))

    kspecs = [page_spec(g, PAGE_SIZE * 2 * A_HEADS, 0) for g in range(pg)]
    vspecs = [page_spec(g, PAGE_SIZE * A_HEADS, half) for g in range(pg) for half in range(2)]
    grid_spec = pltpu.PrefetchScalarGridSpec(
        num_scalar_prefetch=1,
        grid=(nb, n_steps + 1),
        in_specs=[qspec] + kspecs + vspecs + [
            qspec, qspec,
            pl.BlockSpec((None, 4, A_HEAD_DIM), lambda b, j, pt: (j_layer, 0, 0)),
            pl.BlockSpec((None, 1, 2 * A_HEAD_DIM), lambda b, j, pt: (j_layer, 0, 0))],
        out_specs=pl.BlockSpec((lq, D_MODEL), lambda b, j, pt: (b, 0)),
        scratch_shapes=[pltpu.VMEM((rows, D_MODEL), F32), pltpu.VMEM((rows, D_MODEL), BF16),
                        pltpu.VMEM((rows, LANES), F32), pltpu.VMEM((rows, LANES), F32),
                        pltpu.VMEM((rows, 2 * A_HEAD_DIM), F32),
                        pltpu.VMEM((pg * PAGE_SIZE, D_MODEL), BF16),
                        pltpu.VMEM((pg * PAGE_SIZE, D_MODEL), BF16)],
    )
    return pl.pallas_call(
        functools.partial(_sattn_body, lq=lq, n_steps=n_steps, pg=pg, lam_init=lam_init),
        grid_spec=grid_spec,
        out_shape=jax.ShapeDtypeStruct((nb * lq, D_MODEL), out_dtype),
        compiler_params=_cparams(2), name="small_attn",
    )(page_table, q, *([cache_k2] * pg), *([cache_v2] * (2 * pg)), k, v, lam_p, sw)


def _attn_layer(h, hb, k, v, rope_tabs, cache_k, cache_v, page_table, p, layer):
    j_layer = layer - N_A_LAYERS
    lam_init = 0.8 - 0.6 * math.exp(-0.3 * layer)
    q = _matmul(hb, p['a_w_q'], layer=j_layer, n_out=D_MODEL, tn=1024, rope=rope_tabs,
                scale=A_HEAD_DIM ** -0.5)
    lam_p = p['a_lambda'].astype(F32)
    sw = p['a_subln_w'].astype(F32).reshape(-1, 1, 2 * A_HEAD_DIM)
    o_x = _flash_prompt(q, k, v, lam_p, sw, j_layer, lam_init)
    dummy_pt = jnp.zeros((BATCH, 1), jnp.int32)
    o_m = _small_attn(q, k, v, cache_k, cache_v, dummy_pt, lam_p, sw, j_layer, lam_init,
                      row0=ROW_META, lq=N_META, nb=BATCH, n_pages=0, out_dtype=BF16)
    o_s = _small_attn(q, k, v, cache_k, cache_v, page_table, lam_p, sw, j_layer, lam_init,
                      row0=ROW_SAMPLE, lq=DEC_SEQ, nb=DEC_BATCH, n_pages=page_table.shape[1],
                      out_dtype=F32)
    o = jnp.concatenate([o_x, o_m, o_s.astype(BF16)], axis=0)
    mix = _matmul(o, p['a_w_o'], layer=j_layer, n_out=D_MODEL, tn=1024)
    return _ln_residual(h, mix, p['ln_g'], p['ln_b'], layer)


def _rope_tables():
    pos = jnp.concatenate([
        jnp.tile(N_META + jnp.arange(SEQ, dtype=jnp.int32), BATCH),
        jnp.tile(jnp.arange(N_META, dtype=jnp.int32), BATCH),
        jnp.tile(PAST_LEN + jnp.arange(DEC_SEQ, dtype=jnp.int32), DEC_BATCH)])
    half = A_HEAD_DIM // 2
    inv = ROPE_THETA ** (-jnp.arange(half, dtype=F32) / half)
    ang = pos.astype(F32)[:, None] * inv[None, :]
    cos = jnp.cos(ang)
    sin = jnp.sin(ang)
    return jnp.concatenate([cos, cos], axis=1), jnp.concatenate([-sin, sin], axis=1)


def kernel(x_prompt, x_sample, cache_k, cache_v, page_table, state_ssm, state_conv, meta_tokens, m_w_in, m_conv_w, m_conv_b, m_dt_bias, m_a_log, m_d, m_norm_w, m_w_out, a_w_q, a_w_k, a_w_v, a_lambda, a_subln_w, a_w_o, moe_w_router, moe_b_router, moe_w_gu, moe_b_gu, moe_w_down, moe_b_down, ln_g, ln_b):
    p = dict(m_w_in=m_w_in, m_conv_w=m_conv_w, m_conv_b=m_conv_b, m_dt_bias=m_dt_bias, m_a_log=m_a_log,
             m_d=m_d, m_norm_w=m_norm_w, m_w_out=m_w_out, a_w_q=a_w_q, a_w_k=a_w_k, a_w_v=a_w_v,
             a_lambda=a_lambda, a_subln_w=a_subln_w, a_w_o=a_w_o, moe_w_router=moe_w_router,
             moe_b_router=moe_b_router, moe_w_gu=moe_w_gu, moe_b_gu=moe_b_gu, moe_w_down=moe_w_down,
             moe_b_down=moe_b_down, ln_g=ln_g, ln_b=ln_b)
    assert page_table.shape[1] * PAGE_SIZE == PAST_LEN
    meta = jnp.broadcast_to(meta_tokens[None].astype(F32), (BATCH, N_META, D_MODEL))
    h = jnp.concatenate([x_prompt.reshape(ROWS_X, D_MODEL), meta.reshape(ROWS_META, D_MODEL),
                         x_sample.reshape(ROWS_SAMPLE, D_MODEL)], axis=0)
    hb = h.astype(BF16)
    rope_tabs = _rope_tables()

    ssm_p, conv_p, ssm_s, conv_s = [], [], [], []
    k = v = None
    for layer in range(DEPTH):
        if layer < N_A_LAYERS:
            h, (h_x, c_x, h_s, c_s) = _mamba_layer(h, hb, state_ssm, state_conv, p, layer)
            ssm_p.append(h_x)
            conv_p.append(c_x)
            ssm_s.append(h_s)
            conv_s.append(c_s)
        else:
            if layer == N_A_LAYERS:
                k = _matmul(hb, a_w_k, n_out=D_MODEL, tn=1024, rope=rope_tabs)
                v = _matmul(hb, a_w_v, n_out=D_MODEL, tn=1024)
            h = _attn_layer(h, hb, k, v, rope_tabs, cache_k, cache_v, page_table, p, layer)
        h, hb = _moe_layer(h, p, layer, with_bf16=layer + 1 < DEPTH)

    def prompt_rows(a):
        return jnp.concatenate([a[ROW_META:ROW_SAMPLE].reshape(BATCH, N_META, D_MODEL),
                                a[:ROWS_X].reshape(BATCH, SEQ, D_MODEL)], axis=1)

    y_prompt = h[:ROWS_X].reshape(BATCH, SEQ, D_MODEL)
    y_sample = h[ROW_SAMPLE:].reshape(DEC_BATCH, DEC_SEQ, D_MODEL)
    k_prompt = prompt_rows(k).reshape(BATCH, SEQ + N_META, 2 * A_HEADS, A_HEAD_DIM)
    v_prompt = prompt_rows(v).reshape(BATCH, SEQ + N_META, A_HEADS, 2 * A_HEAD_DIM)
    k_sample = k[ROW_SAMPLE:].reshape(DEC_BATCH, DEC_SEQ, 2 * A_HEADS, A_HEAD_DIM)
    v_sample = v[ROW_SAMPLE:].reshape(DEC_BATCH, DEC_SEQ, A_HEADS, 2 * A_HEAD_DIM)
    return (y_prompt, y_sample, k_prompt, v_prompt, jnp.stack(ssm_p), jnp.stack(conv_p),
            k_sample, v_sample, jnp.stack(ssm_s), jnp.stack(conv_s))
```

```python
import functools
import math

import jax
import jax.numpy as jnp
from jax import lax
from jax.experimental import pallas as pl
from jax.experimental.pallas import tpu as pltpu

F32 = jnp.float32
BF16 = jnp.bfloat16

D_MODEL = 2048
BATCH = 4
SEQ = 2048
DEPTH = 4
DEC_BATCH = 8
DEC_SEQ = 8
PAGE_SIZE = 128
PAST_LEN = 16384
N_A_LAYERS = DEPTH // 2
N_META = 16
M_D_INNER = 2 * D_MODEL
M_HEADDIM = 64
M_NHEADS = M_D_INNER // M_HEADDIM
M_NGROUPS = 8
M_DSTATE = 128
M_CONV = 4
M_CONV_DIM = M_D_INNER + 2 * M_NGROUPS * M_DSTATE
A_HEADS = 8
A_HEAD_DIM = D_MODEL // (2 * A_HEADS)
ROPE_THETA = 10000.0
N_EXPERTS = 32
TOP_K = 4
D_FF = D_MODEL // 4
SWIGLU_LIMIT = 7.0
SWIGLU_ALPHA = 1.702
DN_ALPHA = (2 * DEPTH) ** 0.25
NORM_EPS = 1e-5

ROWS_X = BATCH * SEQ
ROW_META = ROWS_X
ROWS_META = BATCH * N_META
ROW_SAMPLE = ROW_META + ROWS_META
ROWS_SAMPLE = DEC_BATCH * DEC_SEQ
N_ROWS = ROW_SAMPLE + ROWS_SAMPLE

LANES = 128
VMEM_LIMIT = 56 * 1024 * 1024

SSD_CHUNK = 128
MOE_TM = 256
MOE_BLOCKS = N_ROWS * TOP_K // MOE_TM + N_EXPERTS
MOE_ROWS = MOE_BLOCKS * MOE_TM
MM_TM = 640
LN_TR = 320
FLASH_T = 512


def _cparams(n_axes):
    return pltpu.CompilerParams(dimension_semantics=("arbitrary",) * n_axes,
                                vmem_limit_bytes=VMEM_LIMIT)


def _dot(a, b):
    return jnp.dot(a, b, preferred_element_type=F32)


def _dot_nt(a, b):
    return lax.dot_general(a, b, (((1,), (1,)), ((), ())), preferred_element_type=F32)


def _split2(x):
    hi = x.astype(BF16)
    lo = (x - hi.astype(F32)).astype(BF16)
    return hi, lo


def _split3(x):
    hi = x.astype(BF16)
    r = x - hi.astype(F32)
    mid = r.astype(BF16)
    lo = (r - mid.astype(F32)).astype(BF16)
    return hi, mid, lo


def _mm_body(*refs, rope, scale):
    if rope:
        x_ref, w_ref, cos_ref, sin_ref, o_ref, wbf_ref = refs
    else:
        x_ref, w_ref, o_ref, wbf_ref = refs

    @pl.when(pl.program_id(1) == 0)
    def _cast_weight():
        wbf_ref[...] = w_ref[...].astype(BF16)

    acc = _dot(x_ref[...].astype(BF16), wbf_ref[...])
    if rope:
        cos = cos_ref[...]
        sin = sin_ref[...]
        for j in range(acc.shape[1] // LANES):
            blk = acc[:, j * LANES:(j + 1) * LANES]
            out = (blk * cos + pltpu.roll(blk, LANES // 2, 1) * sin) * scale
            o_ref[:, j * LANES:(j + 1) * LANES] = out.astype(o_ref.dtype)
    else:
        o_ref[...] = acc.astype(o_ref.dtype)


def _matmul(x, w, *, layer=None, col0=0, n_out, tn, out_dtype=F32, rope=None, scale=1.0):
    m, k = x.shape
    tm = MM_TM
    assert m % tm == 0 and n_out % tn == 0 and col0 % tn == 0
    jb = col0 // tn
    if layer is None:
        w_spec = pl.BlockSpec((k, tn), lambda j, i: (0, j + jb))
    else:
        w_spec = pl.BlockSpec((None, k, tn), lambda j, i: (layer, 0, j + jb))
    in_specs = [pl.BlockSpec((tm, k), lambda j, i: (i, 0)), w_spec]
    args = [x, w]
    if rope is not None:
        in_specs += [pl.BlockSpec((tm, LANES), lambda j, i: (i, 0))] * 2
        args += list(rope)
    return pl.pallas_call(
        functools.partial(_mm_body, rope=rope is not None, scale=scale),
        grid=(n_out // tn, m // tm),
        in_specs=in_specs,
        out_specs=pl.BlockSpec((tm, tn), lambda j, i: (i, j)),
        out_shape=jax.ShapeDtypeStruct((m, n_out), out_dtype),
        scratch_shapes=[pltpu.VMEM((k, tn), BF16)],
        compiler_params=_cparams(2),
        name="dense_matmul",
    )(*args)


ROUTE_SUB = 128


def _lane_pack(cols, lane):
    out = jnp.zeros(lane.shape, cols[0].dtype)
    for k, c in enumerate(cols):
        out = jnp.where(lane == k, c, out)
    return out


def _route_body(x_ref, w_ref, b_ref, idx_ref, gate_ref, rank_ref, cnt_ref, run_ref):
    tm = x_ref.shape[0]

    @pl.when(pl.program_id(0) == 0)
    def _init():
        run_ref[...] = jnp.zeros(run_ref.shape, F32)

    xh, xl = _split2(x_ref[...])
    wh, wl = _split2(w_ref[...])
    logits = _dot(xh, wh) + _dot(xh, wl) + _dot(xl, wh) + b_ref[...]
    lane = lax.broadcasted_iota(jnp.int32, (ROUTE_SUB, LANES), 1)
    r_i = lax.broadcasted_iota(jnp.int32, (ROUTE_SUB, ROUTE_SUB), 0)
    c_i = lax.broadcasted_iota(jnp.int32, (ROUTE_SUB, ROUTE_SUB), 1)
    below = jnp.where(r_i > c_i, 1.0, 0.0).astype(BF16)
    running = run_ref[0:1, :]
    for sb in range(tm // ROUTE_SUB):
        rs = slice(sb * ROUTE_SUB, (sb + 1) * ROUTE_SUB)
        rest = logits[rs]
        vals, idxs, hots = [], [], []
        for _ in range(TOP_K):
            mx = jnp.max(rest, axis=1, keepdims=True)
            idx = jnp.min(jnp.where(rest == mx, lane, LANES), axis=1, keepdims=True)
            hot = lane == idx
            rest = jnp.where(hot, -jnp.inf, rest)
            vals.append(mx)
            idxs.append(idx)
            hots.append(hot)
        exps = [jnp.exp(v - vals[0]) for v in vals]
        denom = exps[0] + exps[1] + exps[2] + exps[3]
        idx_ref[rs, :] = _lane_pack(idxs, lane)
        gate_ref[rs, :] = _lane_pack([e / denom for e in exps], lane)
        sel = jnp.zeros((ROUTE_SUB, LANES), F32)
        for hot in hots:
            sel = sel + jnp.where(hot, 1.0, 0.0)
        before = _dot(below, sel.astype(BF16)) + running
        ranks = [jnp.sum(jnp.where(hot, before, 0.0), axis=1, keepdims=True) for hot in hots]
        rank_ref[rs, :] = _lane_pack(ranks, lane).astype(jnp.int32)
        running = running + jnp.sum(sel, axis=0, keepdims=True)
    run_ref[...] = jnp.broadcast_to(running, run_ref.shape)
    cnt_ref[...] = jnp.broadcast_to(running, cnt_ref.shape)


def _route(h, w_router, b_router, layer):
    tm = MM_TM
    w_pad = jnp.pad(w_router, ((0, 0), (0, 0), (0, LANES - N_EXPERTS)))
    b_pad = jnp.pad(b_router.astype(F32), ((0, 0), (0, LANES - N_EXPERTS)),
                    constant_values=-1e30).reshape(DEPTH, 1, LANES)
    row = pl.BlockSpec((tm, LANES), lambda i: (i, 0))
    return pl.pallas_call(
        _route_body,
        grid=(N_ROWS // tm,),
        in_specs=[pl.BlockSpec((tm, D_MODEL), lambda i: (i, 0)),
                  pl.BlockSpec((None, D_MODEL, LANES), lambda i: (layer, 0, 0)),
                  pl.BlockSpec((None, 1, LANES), lambda i: (layer, 0, 0))],
        out_specs=(row, row, row, pl.BlockSpec((8, LANES), lambda i: (0, 0))),
        out_shape=(jax.ShapeDtypeStruct((N_ROWS, LANES), jnp.int32),
                   jax.ShapeDtypeStruct((N_ROWS, LANES), F32),
                   jax.ShapeDtypeStruct((N_ROWS, LANES), jnp.int32),
                   jax.ShapeDtypeStruct((8, LANES), F32)),
        scratch_shapes=[pltpu.VMEM((8, LANES), F32)],
        compiler_params=_cparams(1),
        name="moe_route",
    )(h, w_pad, b_pad)


def _layer_norm_rows(x, g, b):
    mu = jnp.mean(x, axis=-1, keepdims=True)
    xc = x - mu
    var = jnp.mean(xc * xc, axis=-1, keepdims=True)
    return xc * lax.rsqrt(var + NORM_EPS) * g + b


def _ln_body(h_ref, m_ref, g_ref, b_ref, o_ref):
    o_ref[...] = _layer_norm_rows(DN_ALPHA * h_ref[...] + m_ref[...], g_ref[...], b_ref[...])


def _ln_residual(h, mix, ln_g, ln_b, layer):
    tr = LN_TR
    row = pl.BlockSpec((tr, D_MODEL), lambda i: (i, 0))
    par = pl.BlockSpec((None, None, 1, D_MODEL), lambda i: (layer, 0, 0, 0))
    return pl.pallas_call(
        _ln_body, grid=(N_ROWS // tr,),
        in_specs=[row, row, par, par], out_specs=row,
        out_shape=jax.ShapeDtypeStruct((N_ROWS, D_MODEL), F32),
        compiler_params=_cparams(1), name="ln_residual",
    )(h, mix, ln_g.reshape(DEPTH, 2, 1, D_MODEL), ln_b.reshape(DEPTH, 2, 1, D_MODEL))


MOE_TB = 128
MOE_NBLK = N_ROWS // MOE_TB
MOE_IDX = MOE_TB * TOP_K


def _idx_copy(dest_hbm, idx_smem, isem, blk, slot):
    return pltpu.make_async_copy(dest_hbm.at[blk], idx_smem.at[slot], isem.at[slot])


def _dispatch_body(pstart_ref, pcnt_ref, nused_ref, dest_hbm, h_ref, xs_hbm, idx_smem, zero_ref,
                   isem, rsem, psem, tsem):
    i = pl.program_id(0)
    slot = i % 2

    @pl.when(i == 0)
    def _first():
        _idx_copy(dest_hbm, idx_smem, isem, 0, 0).start()
        zero_ref[...] = jnp.zeros(zero_ref.shape, F32)

        def tail_copy(blk):
            return pltpu.make_async_copy(
                zero_ref, xs_hbm.at[pl.ds(pl.multiple_of(blk * MOE_TM, MOE_TM), MOE_TM)], tsem)

        def tail_start(blk, carry):
            tail_copy(blk).start()
            return carry

        def tail_wait(blk, carry):
            tail_copy(blk).wait()
            return carry

        lax.fori_loop(nused_ref[0], MOE_BLOCKS, tail_start, 0)
        total = jnp.int32(0)
        for e in range(N_EXPERTS):
            first = pstart_ref[e]
            cnt = pcnt_ref[e]

            def pad_row(j, carry, first=first):
                pltpu.make_async_copy(zero_ref.at[pl.ds(0, 1)], xs_hbm.at[pl.ds(first + j, 1)], psem).start()
                return carry

            lax.fori_loop(0, cnt, pad_row, 0)
            total = total + cnt

        def wait_pad(j, carry):
            pltpu.make_async_copy(zero_ref.at[pl.ds(0, 1)], xs_hbm.at[pl.ds(0, 1)], psem).wait()
            return carry

        lax.fori_loop(0, total, wait_pad, 0)
        lax.fori_loop(nused_ref[0], MOE_BLOCKS, tail_wait, 0)

    _idx_copy(dest_hbm, idx_smem, isem, i, slot).wait()

    @pl.when(i + 1 < MOE_NBLK)
    def _next_idx():
        _idx_copy(dest_hbm, idx_smem, isem, i + 1, 1 - slot).start()

    for r in range(MOE_TB):
        for k in range(TOP_K):
            d = idx_smem[slot, r * TOP_K + k]
            pltpu.make_async_copy(h_ref.at[pl.ds(r, 1)], xs_hbm.at[pl.ds(d, 1)], rsem).start(priority=k % 2)
    pltpu.make_async_copy(xs_hbm.at[pl.ds(0, MOE_IDX)], xs_hbm.at[pl.ds(0, MOE_IDX)], rsem).wait()


def _dispatch(h, dest2, pad_start, pad_cnt, n_used):
    grid_spec = pltpu.PrefetchScalarGridSpec(
        num_scalar_prefetch=3,
        grid=(MOE_NBLK,),
        in_specs=[pl.BlockSpec(memory_space=pl.ANY),
                  pl.BlockSpec((MOE_TB, D_MODEL), lambda i, ps, pc, nu: (i, 0))],
        out_specs=pl.BlockSpec(memory_space=pl.ANY),
        scratch_shapes=[pltpu.SMEM((2, MOE_IDX), jnp.int32), pltpu.VMEM((MOE_TM, D_MODEL), F32),
                        pltpu.SemaphoreType.DMA((2,)), pltpu.SemaphoreType.DMA(()),
                        pltpu.SemaphoreType.DMA(()), pltpu.SemaphoreType.DMA(())],
    )
    return pl.pallas_call(
        _dispatch_body, grid_spec=grid_spec,
        out_shape=jax.ShapeDtypeStruct((MOE_ROWS, D_MODEL), F32),
        compiler_params=_cparams(1), name="moe_dispatch",
    )(pad_start, pad_cnt, n_used, dest2, h)


def _combine_body(dest_hbm, eo_hbm, h_ref, gate_ref, g_ref, b_ref, *rest, with_bf16):
    if with_bf16:
        o_ref, obf_ref, idx_smem, ebuf, isem, rsem = rest
    else:
        o_ref, idx_smem, ebuf, isem, rsem = rest
    i = pl.program_id(0)
    slot = i % 2

    def issue_rows(s):
        for r in range(MOE_TB):
            for k in range(TOP_K):
                d = idx_smem[s, r * TOP_K + k]
                pltpu.make_async_copy(eo_hbm.at[pl.ds(d, 1)], ebuf.at[s, k, pl.ds(r, 1)],
                                      rsem.at[s]).start(priority=k % 2)

    @pl.when(i == 0)
    def _prime():
        first = _idx_copy(dest_hbm, idx_smem, isem, 0, 0)
        first.start()
        first.wait()
        issue_rows(0)
        _idx_copy(dest_hbm, idx_smem, isem, 1, 1).start()

    @pl.when(i + 1 < MOE_NBLK)
    def _prefetch():
        _idx_copy(dest_hbm, idx_smem, isem, i + 1, 1 - slot).wait()
        issue_rows(1 - slot)

        @pl.when(i + 2 < MOE_NBLK)
        def _next_idx():
            _idx_copy(dest_hbm, idx_smem, isem, i + 2, slot).start()

    pltpu.make_async_copy(ebuf.at[slot], ebuf.at[slot], rsem.at[slot]).wait()
    gates = gate_ref[...]
    f = ebuf[slot, 0] * gates[:, 0:1]
    for k in range(1, TOP_K):
        f = f + ebuf[slot, k] * gates[:, k:k + 1]
    y = _layer_norm_rows(DN_ALPHA * h_ref[...] + f, g_ref[...], b_ref[...])
    o_ref[...] = y
    if with_bf16:
        obf_ref[...] = y.astype(BF16)


def _ln_moe(h, eo, dest2, gates, ln_g, ln_b, layer, with_bf16):
    row = pl.BlockSpec((MOE_TB, D_MODEL), lambda i: (i, 0))
    par = pl.BlockSpec((None, None, 1, D_MODEL), lambda i: (layer, 1, 0, 0))
    out_shape = [jax.ShapeDtypeStruct((N_ROWS, D_MODEL), F32)]
    if with_bf16:
        out_shape.append(jax.ShapeDtypeStruct((N_ROWS, D_MODEL), BF16))
    outs = pl.pallas_call(
        functools.partial(_combine_body, with_bf16=with_bf16),
        grid=(MOE_NBLK,),
        in_specs=[pl.BlockSpec(memory_space=pl.ANY), pl.BlockSpec(memory_space=pl.ANY), row,
                  pl.BlockSpec((MOE_TB, LANES), lambda i: (i, 0)), par, par],
        out_specs=tuple([row] * len(out_shape)),
        out_shape=tuple(out_shape),
        scratch_shapes=[pltpu.SMEM((2, MOE_IDX), jnp.int32),
                        pltpu.VMEM((2, TOP_K, MOE_TB, D_MODEL), F32),
                        pltpu.SemaphoreType.DMA((2,)), pltpu.SemaphoreType.DMA((2,))],
        compiler_params=_cparams(1), name="ln_moe_combine",
    )(dest2, eo, h, gates, ln_g.reshape(DEPTH, 2, 1, D_MODEL), ln_b.reshape(DEPTH, 2, 1, D_MODEL))
    return outs if with_bf16 else (outs[0], None)


def _moe_body(be_ref, par_ref, nxt_ref, nb_ref, x_ref, wgu_hbm, bgu_ref, wd_hbm, bd_ref, o_ref,
              wgu_f32, wd_f32, wgu_bf, wd_bf, wsem, *, layer):
    i = pl.program_id(0)
    e = be_ref[i]
    prev = be_ref[jnp.maximum(i - 1, 0)]
    slot = par_ref[i]
    used = i < nb_ref[0]

    def fetch(expert, s):
        return (pltpu.make_async_copy(wgu_hbm.at[layer, expert], wgu_f32.at[s], wsem.at[0, s]),
                pltpu.make_async_copy(wd_hbm.at[layer, expert], wd_f32.at[s], wsem.at[1, s]))

    @pl.when(i == 0)
    def _cold_start():
        for cp in fetch(e, 0):
            cp.start()

    @pl.when(jnp.logical_and(used, jnp.logical_or(i == 0, e != prev)))
    def _switch_expert():
        for cp in fetch(e, slot):
            cp.wait()
        nxt = nxt_ref[i]

        @pl.when(nxt >= 0)
        def _prefetch_next():
            for cp in fetch(nxt, 1 - slot):
                cp.start()

        wgu_bf[...] = wgu_f32[slot].astype(BF16)
        wd_bf[...] = wd_f32[slot].astype(BF16)

    @pl.when(used)
    def _compute():
        gu = _dot(x_ref[...].astype(BF16), wgu_bf[...]) + bgu_ref[...]
        gate = jnp.minimum(gu[:, :D_FF], SWIGLU_LIMIT)
        up = jnp.clip(gu[:, D_FF:], -SWIGLU_LIMIT, SWIGLU_LIMIT)
        act = (up + 1.0) * gate * jax.nn.sigmoid(SWIGLU_ALPHA * gate)
        o_ref[...] = _dot(act.astype(BF16), wd_bf[...]) + bd_ref[...]

    @pl.when(jnp.logical_not(used))
    def _unused_block():
        o_ref[...] = jnp.zeros(o_ref.shape, o_ref.dtype)


def _moe_experts(xs, blk_e, blk_par, blk_nxt, n_used, w_gu, b_gu, w_down, b_down, layer):
    grid_spec = pltpu.PrefetchScalarGridSpec(
        num_scalar_prefetch=4,
        grid=(MOE_BLOCKS,),
        in_specs=[
            pl.BlockSpec((MOE_TM, D_MODEL), lambda i, be, pa, nx, nb: (jnp.minimum(i, nb[0] - 1), 0)),
            pl.BlockSpec(memory_space=pl.ANY),
            pl.BlockSpec((None, None, 1, 2 * D_FF), lambda i, be, pa, nx, nb: (layer, be[i], 0, 0)),
            pl.BlockSpec(memory_space=pl.ANY),
            pl.BlockSpec((None, None, 1, D_MODEL), lambda i, be, pa, nx, nb: (layer, be[i], 0, 0)),
        ],
        out_specs=pl.BlockSpec((MOE_TM, D_MODEL), lambda i, be, pa, nx, nb: (i, 0)),
        scratch_shapes=[pltpu.VMEM((2, D_MODEL, 2 * D_FF), F32), pltpu.VMEM((2, D_FF, D_MODEL), F32),
                        pltpu.VMEM((D_MODEL, 2 * D_FF), BF16), pltpu.VMEM((D_FF, D_MODEL), BF16),
                        pltpu.SemaphoreType.DMA((2, 2))],
    )
    return pl.pallas_call(
        functools.partial(_moe_body, layer=layer), grid_spec=grid_spec,
        out_shape=jax.ShapeDtypeStruct((MOE_ROWS, D_MODEL), F32),
        compiler_params=_cparams(1), name="moe_experts",
    )(blk_e, blk_par, blk_nxt, n_used, xs, w_gu, b_gu.reshape(DEPTH, N_EXPERTS, 1, 2 * D_FF),
      w_down, b_down.reshape(DEPTH, N_EXPERTS, 1, D_MODEL))


def _moe_layer(h, p, layer, with_bf16):
    idx, gates, rank, cnt = _route(h, p['moe_w_router'], p['moe_b_router'], layer)
    counts = cnt[0, :N_EXPERTS].astype(jnp.int32)
    psz = (counts + MOE_TM - 1) // MOE_TM * MOE_TM
    pend = jnp.cumsum(psz)
    pstart = pend - psz
    expert_ids = jnp.arange(N_EXPERTS, dtype=jnp.int32)
    top_i = idx[:, :TOP_K]
    base = jnp.sum(jnp.where(top_i[:, :, None] == expert_ids, pstart, 0), axis=-1)
    dest2 = (base + rank[:, :TOP_K]).astype(jnp.int32).reshape(MOE_NBLK, MOE_IDX)
    n_used = (pend[-1] // MOE_TM).astype(jnp.int32)
    blk = jnp.arange(MOE_BLOCKS, dtype=jnp.int32)
    blk_e = jnp.sum((blk[:, None] * MOE_TM >= pend[None, :]).astype(jnp.int32), axis=1)
    blk_e = jnp.minimum(blk_e, N_EXPERTS - 1)
    blk_e = jnp.where(blk < n_used, blk_e, blk_e[jnp.maximum(n_used - 1, 0)]).astype(jnp.int32)
    used = counts > 0
    ordinal = jnp.cumsum(used.astype(jnp.int32)) - 1
    cand = jnp.where(used, expert_ids, N_EXPERTS)
    later = jnp.concatenate([lax.cummin(cand[::-1])[::-1][1:], jnp.full((1,), N_EXPERTS, jnp.int32)])
    nxt_e = jnp.where(later >= N_EXPERTS, -1, later)
    blk_hot = blk_e[:, None] == expert_ids
    blk_par = jnp.sum(jnp.where(blk_hot, ordinal % 2, 0), axis=1).astype(jnp.int32)
    blk_nxt = jnp.sum(jnp.where(blk_hot, nxt_e, 0), axis=1).astype(jnp.int32)
    n_used = n_used.reshape(1)
    xs = _dispatch(h, dest2, (pstart + counts).astype(jnp.int32), (psz - counts).astype(jnp.int32), n_used)
    eo = _moe_experts(xs, blk_e, blk_par, blk_nxt, n_used, p['moe_w_gu'], p['moe_b_gu'],
                      p['moe_w_down'], p['moe_b_down'], layer)
    return _ln_moe(h, eo, dest2, gates, p['ln_g'], p['ln_b'], layer, with_bf16)


def _ssd_body(z_ref, xbc_ref, dt_ref, h0_ref, c0_ref, cw_ref, cb_ref, dtb_ref, alog_ref, de_ref, nw_ref,
              eexp_ref, y_ref, hfin_ref, cfin_ref, ht_ref, ext_ref, act_ref, ybuf_ref, *, lv, nc):
    L = SSD_CHUNK
    N = M_DSTATE
    c = pl.program_id(1)
    n_pairs = M_NHEADS // 2
    x_cols = M_D_INNER
    b_col0 = M_D_INNER
    c_col0 = M_D_INNER + M_NGROUPS * M_DSTATE

    @pl.when(c == 0)
    def _init():
        for i in range(n_pairs):
            ht_ref[i] = h0_ref[2 * i:2 * i + 2].reshape(2 * M_HEADDIM, N).T
        ext_ref[0:8, :] = jnp.zeros((8, M_CONV_DIM), F32)
        ext_ref[5:8, :] = c0_ref[...]

    raw = xbc_ref[...]
    if lv < L:
        raw = jnp.concatenate([raw, jnp.zeros((L - lv, M_CONV_DIM), F32)], axis=0)
    ext_ref[8:8 + L, :] = raw
    cw_step = 256
    for j in range(M_CONV_DIM // cw_step):
        cs_ = slice(j * cw_step, (j + 1) * cw_step)
        acc = cb_ref[:, cs_] + cw_ref[0:1, cs_] * ext_ref[5:5 + L, cs_]
        acc = acc + cw_ref[1:2, cs_] * ext_ref[6:6 + L, cs_]
        acc = acc + cw_ref[2:3, cs_] * ext_ref[7:7 + L, cs_]
        acc = acc + cw_ref[3:4, cs_] * ext_ref[8:8 + L, cs_]
        act_ref[:, cs_] = jax.nn.silu(acc)
    carry = ext_ref[5 + lv:8 + lv, :]
    ext_ref[5:8, :] = carry
    cfin_ref[...] = carry

    dtr = dt_ref[...]
    if lv < L:
        dtr = jnp.concatenate([dtr, jnp.zeros((L - lv, LANES), F32)], axis=0)
    dt = jax.nn.softplus(dtr + dtb_ref[...])
    row = lax.broadcasted_iota(jnp.int32, (L, L), 0)
    col = lax.broadcasted_iota(jnp.int32, (L, L), 1)
    if lv < L:
        dt = jnp.where(row < lv, dt, 0.0)
    da = dt * (-jnp.exp(alog_ref[...]))
    causal = row >= col
    tri = jnp.where(causal, 1.0, 0.0).astype(BF16)
    d_hi, d_mid, d_lo = _split3(da)
    cs = _dot(tri, d_hi) + _dot(tri, d_mid) + _dot(tri, d_lo)
    cs_end = cs[L - 1:L, :]
    cst = cs.T
    dtt = dt.T
    wt = (dt * jnp.exp(cs_end - cs)).T
    e_hi, e_lo = _split2(jnp.broadcast_to(jnp.exp(cs_end), (8, LANES)))
    dend = _dot(e_hi, eexp_ref[...]) + _dot(e_lo, eexp_ref[...])
    lane = lax.broadcasted_iota(jnp.int32, (L, LANES), 1)
    first_head = lane < M_HEADDIM

    for g in range(M_NGROUPS):
        bg = act_ref[:, b_col0 + g * N:b_col0 + (g + 1) * N]
        cg = act_ref[:, c_col0 + g * N:c_col0 + (g + 1) * N]
        btg = bg.T
        cb = _dot(cg.astype(BF16), btg.astype(BF16))
        for pr in range(M_NHEADS // M_NGROUPS // 2):
            i = g * (M_NHEADS // M_NGROUPS // 2) + pr
            ps = slice(i * LANES, (i + 1) * LANES)
            xs_pair = act_ref[:, ps]
            ht_pair = ht_ref[i]
            lhs_rows = []
            bts_rows = []
            for j in range(2):
                hd = 2 * i + j
                colb = jnp.broadcast_to(cs[:, hd:hd + 1], (L, L))
                seg = colb - cst[hd:hd + 1, :]
                decay = jnp.exp(jnp.where(causal, seg, -jnp.inf))
                m_h = cb * decay * dtt[hd:hd + 1, :]
                c_h = cg * jnp.exp(colb)
                lhs_rows.append(jnp.concatenate([m_h, c_h], axis=1))
                bts_rows.append(btg * wt[hd:hd + 1, :])
            lhs = jnp.concatenate(lhs_rows, axis=0).astype(BF16)
            rhs = jnp.concatenate([xs_pair, ht_pair], axis=0).astype(BF16)
            out = _dot(lhs, rhs)
            y_pair = jnp.where(first_head, out[:L], out[L:]) + xs_pair * de_ref[:, ps]
            ybuf_ref[:, ps] = y_pair
            bts = jnp.concatenate(bts_rows, axis=0).astype(BF16)
            upd = _dot(bts, xs_pair.astype(BF16))
            ht_ref[i] = dend[0:1, ps] * ht_pair + jnp.where(first_head, upd[:N], upd[N:])

    gw = M_D_INNER // M_NGROUPS
    for g in range(M_NGROUPS):
        gs = slice(g * gw, (g + 1) * gw)
        yg = ybuf_ref[0:lv, gs] * jax.nn.silu(z_ref[:, gs])
        ms = jnp.mean(yg * yg, axis=-1, keepdims=True)
        y_ref[:, gs] = (yg * lax.rsqrt(ms + NORM_EPS) * nw_ref[:, gs]).astype(y_ref.dtype)

    @pl.when(c == nc - 1)
    def _final():
        for i in range(n_pairs):
            hfin_ref[2 * i:2 * i + 2] = ht_ref[i].T.reshape(2, M_HEADDIM, N)


def _ssd(z, xbc, dtr, h0, h0_layer, c0, c0_layer, prm, *, row0, lv, nb, nc, out_dtype):
    rb = row0 // lv

    def rows(width):
        return pl.BlockSpec((lv, width), lambda b, c: (rb + b * nc + c, 0))

    def par(shape):
        return pl.BlockSpec(shape, lambda b, c: (0, 0))

    if h0_layer is None:
        h0_spec = pl.BlockSpec((None, M_NHEADS, M_HEADDIM, M_DSTATE), lambda b, c: (b, 0, 0, 0))
        c0_spec = pl.BlockSpec((None, M_CONV - 1, M_CONV_DIM), lambda b, c: (b, 0, 0))
    else:
        h0_spec = pl.BlockSpec((None, None, M_NHEADS, M_HEADDIM, M_DSTATE),
                               lambda b, c: (h0_layer, b, 0, 0, 0))
        c0_spec = pl.BlockSpec((None, None, M_CONV - 1, M_CONV_DIM), lambda b, c: (c0_layer, b, 0, 0))
    out_rows = nb * nc * lv
    return pl.pallas_call(
        functools.partial(_ssd_body, lv=lv, nc=nc),
        grid=(nb, nc),
        in_specs=[rows(M_D_INNER), rows(M_CONV_DIM), rows(LANES), h0_spec, c0_spec,
                  par((M_CONV, M_CONV_DIM)), par((1, M_CONV_DIM)), par((1, LANES)), par((1, LANES)),
                  par((1, M_D_INNER)), par((1, M_D_INNER)), par((LANES, M_D_INNER))],
        out_specs=(pl.BlockSpec((lv, M_D_INNER), lambda b, c: (b * nc + c, 0)),
                   pl.BlockSpec((None, M_NHEADS, M_HEADDIM, M_DSTATE), lambda b, c: (b, 0, 0, 0)),
                   pl.BlockSpec((None, M_CONV - 1, M_CONV_DIM), lambda b, c: (b, 0, 0))),
        out_shape=(jax.ShapeDtypeStruct((out_rows, M_D_INNER), out_dtype),
                   jax.ShapeDtypeStruct((nb, M_NHEADS, M_HEADDIM, M_DSTATE), F32),
                   jax.ShapeDtypeStruct((nb, M_CONV - 1, M_CONV_DIM), F32)),
        scratch_shapes=[pltpu.VMEM((M_NHEADS // 2, M_DSTATE, LANES), F32),
                        pltpu.VMEM((8 + SSD_CHUNK, M_CONV_DIM), F32),
                        pltpu.VMEM((SSD_CHUNK, M_CONV_DIM), F32),
                        pltpu.VMEM((SSD_CHUNK, M_D_INNER), F32)],
        compiler_params=_cparams(2), name="ssd_mixer",
    )(z, xbc, dtr, h0, c0, *prm)


def _mamba_layer(h, hb, state_ssm, state_conv, p, layer):
    w_in = p['m_w_in']
    z = _matmul(hb, w_in, layer=layer, col0=0, n_out=M_D_INNER, tn=1024)
    xbc = _matmul(hb, w_in, layer=layer, col0=M_D_INNER, n_out=M_CONV_DIM, tn=1024)
    w_dt = jnp.pad(w_in[layer, :, M_D_INNER + M_CONV_DIM:], ((0, 0), (0, LANES - M_NHEADS)))
    dtr = _matmul(hb, w_dt, n_out=LANES, tn=LANES)

    def pad_heads(v):
        return jnp.pad(v[layer].astype(F32), (0, LANES - M_NHEADS)).reshape(1, LANES)

    head_of_col = jnp.arange(M_D_INNER, dtype=jnp.int32) // M_HEADDIM
    expand = (jnp.arange(LANES, dtype=jnp.int32)[:, None] == head_of_col[None, :]).astype(BF16)
    prm = (p['m_conv_w'][layer], p['m_conv_b'][layer].reshape(1, M_CONV_DIM),
           pad_heads(p['m_dt_bias']), pad_heads(p['m_a_log']),
           jnp.repeat(p['m_d'][layer].astype(F32), M_HEADDIM).reshape(1, M_D_INNER),
           p['m_norm_w'][layer].reshape(1, M_D_INNER), expand)
    zeros_h = jnp.zeros((BATCH, M_NHEADS, M_HEADDIM, M_DSTATE), F32)
    zeros_c = jnp.zeros((BATCH, M_CONV - 1, M_CONV_DIM), F32)
    y_m, h_m, c_m = _ssd(z, xbc, dtr, zeros_h, None, zeros_c, None, prm,
                         row0=ROW_META, lv=N_META, nb=BATCH, nc=1, out_dtype=BF16)
    y_x, h_x, c_x = _ssd(z, xbc, dtr, h_m, None, c_m, None, prm,
                         row0=0, lv=SSD_CHUNK, nb=BATCH, nc=SEQ // SSD_CHUNK, out_dtype=BF16)
    y_s, h_s, c_s = _ssd(z, xbc, dtr, state_ssm, layer, state_conv, layer, prm,
                         row0=ROW_SAMPLE, lv=DEC_SEQ, nb=DEC_BATCH, nc=1, out_dtype=F32)
    yn = jnp.concatenate([y_x, y_m, y_s.astype(BF16)], axis=0)
    mix = _matmul(yn, p['m_w_out'], layer=layer, n_out=D_MODEL, tn=512)
    h = _ln_residual(h, mix, p['ln_g'], p['ln_b'], layer)
    return h, (h_x, c_x, h_s, c_s)


def _lambda_value(lam_ref, lam_init):
    lp = lam_ref[...]
    s1 = jnp.sum(lp[0:1, :] * lp[1:2, :], axis=-1, keepdims=True)
    s2 = jnp.sum(lp[2:3, :] * lp[3:4, :], axis=-1, keepdims=True)
    return jnp.exp(s1) - jnp.exp(s2) + lam_init


def _subln(o, sw, lam_init):
    ms = jnp.mean(o * o, axis=-1, keepdims=True)
    return o * lax.rsqrt(ms + NORM_EPS) * sw * (1.0 - lam_init)


def _rep(x, n):
    return x if n == 1 else jnp.concatenate([x] * n, axis=1)


def _flash_body(q_ref, k_ref, v_ref, km_ref, vm_ref, lam_ref, sw_ref, o_ref,
                m_ref, l_ref, acc_ref, kb_ref, vb_ref, kpad_ref, vpad_ref, *, lam_init):
    t = FLASH_T
    dh = A_HEAD_DIM
    qi = pl.program_id(2)

    @pl.when(qi == 0)
    def _load_kv():
        kb_ref[...] = k_ref[...].astype(BF16)
        vb_ref[...] = v_ref[...].astype(BF16)
        kpad_ref[...] = jnp.zeros(kpad_ref.shape, BF16)
        vpad_ref[...] = jnp.zeros(vpad_ref.shape, BF16)
        kpad_ref[0:N_META, :] = km_ref[...].astype(BF16)
        vpad_ref[0:N_META, :] = vm_ref[...].astype(BF16)

    m_ref[...] = jnp.full(m_ref.shape, -jnp.inf, F32)
    l_ref[...] = jnp.zeros(l_ref.shape, F32)
    acc_ref[...] = jnp.zeros(acc_ref.shape, F32)
    qs = [q_ref[:, j * dh:(j + 1) * dh].astype(BF16) for j in range(2)]

    def process(kb, vb, mask):
        for j in range(2):
            s = _dot_nt(qs[j], kb[:, j * dh:(j + 1) * dh])
            if mask is not None:
                s = jnp.where(mask, s, -jnp.inf)
            m_prev = m_ref[j]
            m_new = jnp.maximum(m_prev, jnp.max(s, axis=1, keepdims=True))
            alpha = jnp.exp(m_prev - m_new)
            pmat = jnp.exp(s - _rep(m_new, s.shape[1] // LANES))
            l_ref[j] = alpha * l_ref[j] + jnp.sum(pmat, axis=1, keepdims=True)
            acc_ref[j] = _rep(alpha, 2) * acc_ref[j] + _dot(pmat.astype(BF16), vb)
            m_ref[j] = m_new

    colm = lax.broadcasted_iota(jnp.int32, (t, LANES), 1)
    process(kpad_ref[...], vpad_ref[...], colm < N_META)

    def full_block(kj, carry):
        off = pl.multiple_of(kj * t, t)
        process(kb_ref[pl.ds(off, t), :], vb_ref[pl.ds(off, t), :], None)
        return carry

    lax.fori_loop(0, qi, full_block, 0)

    off = pl.multiple_of(qi * t, t)
    r = lax.broadcasted_iota(jnp.int32, (t, t), 0)
    cc = lax.broadcasted_iota(jnp.int32, (t, t), 1)
    process(kb_ref[pl.ds(off, t), :], vb_ref[pl.ds(off, t), :], cc <= r)
    lam = _lambda_value(lam_ref, lam_init)
    o1 = acc_ref[0] / _rep(l_ref[0], 2)
    o2 = acc_ref[1] / _rep(l_ref[1], 2)
    o_ref[...] = _subln(o1 - lam * o2, sw_ref[...], lam_init).astype(o_ref.dtype)


def _flash_prompt(q, k, v, lam_p, sw, j_layer, lam_init):
    t = FLASH_T
    nq = SEQ // t
    w2 = 2 * A_HEAD_DIM
    qspec = pl.BlockSpec((t, w2), lambda b, p, qi: (b * nq + qi, p))
    kspec = pl.BlockSpec((SEQ, w2), lambda b, p, qi: (b, p))
    mspec = pl.BlockSpec((N_META, w2), lambda b, p, qi: (ROW_META // N_META + b, p))
    return pl.pallas_call(
        functools.partial(_flash_body, lam_init=lam_init),
        grid=(BATCH, A_HEADS, nq),
        in_specs=[qspec, kspec, kspec, mspec, mspec,
                  pl.BlockSpec((None, 4, A_HEAD_DIM), lambda b, p, qi: (j_layer, 0, 0)),
                  pl.BlockSpec((None, 1, w2), lambda b, p, qi: (j_layer, 0, 0))],
        out_specs=pl.BlockSpec((t, w2), lambda b, p, qi: (b * nq + qi, p)),
        out_shape=jax.ShapeDtypeStruct((ROWS_X, D_MODEL), BF16),
        scratch_shapes=[pltpu.VMEM((2, t, LANES), F32), pltpu.VMEM((2, t, LANES), F32),
                        pltpu.VMEM((2, t, w2), F32),
                        pltpu.VMEM((SEQ, w2), BF16), pltpu.VMEM((SEQ, w2), BF16),
                        pltpu.VMEM((LANES, w2), BF16), pltpu.VMEM((LANES, w2), BF16)],
        compiler_params=_cparams(3), name="flash_prompt",
    )(q, k, v, k, v, lam_p, sw)


SATTN_PG = 8


def _sattn_body(pt_ref, q_ref, *refs, lq, n_steps, pg, lam_init):
    del pt_ref
    ck_refs = refs[:pg]
    cv_refs = refs[pg:3 * pg]
    (kn_ref, vn_ref, lam_ref, sw_ref, o_ref, qf_ref, qb_ref, m_ref, l_ref, acc_ref,
     kb_ref, vb_ref) = refs[3 * pg:]
    dh = A_HEAD_DIM
    n_qk = 2 * A_HEADS
    rows = n_qk * lq
    j = pl.program_id(1)

    @pl.when(j == 0)
    def _init():
        qf_ref[...] = jnp.zeros(qf_ref.shape, F32)
        for hd in range(n_qk):
            qf_ref[hd * lq:(hd + 1) * lq, hd * dh:(hd + 1) * dh] = q_ref[:, hd * dh:(hd + 1) * dh]
        qb_ref[...] = qf_ref[...].astype(BF16)
        m_ref[...] = jnp.full(m_ref.shape, -jnp.inf, F32)
        l_ref[...] = jnp.zeros(l_ref.shape, F32)
        acc_ref[...] = jnp.zeros(acc_ref.shape, F32)

    def process(tk, mask):
        s = _dot_nt(qb_ref[...], kb_ref[0:tk, :])
        if mask is not None:
            s = jnp.where(mask, s, -jnp.inf)
        m_prev = m_ref[...]
        m_new = jnp.maximum(m_prev, jnp.max(s, axis=1, keepdims=True))
        alpha = jnp.exp(m_prev - m_new)
        pmat = jnp.exp(s - _rep(m_new, tk // LANES))
        l_ref[...] = alpha * l_ref[...] + jnp.sum(pmat, axis=1, keepdims=True)
        m_ref[...] = m_new
        pb = pmat.astype(BF16)
        alpha2 = _rep(alpha, 2)
        for pp in range(A_HEADS):
            rs = slice(2 * pp * lq, (2 * pp + 2) * lq)
            acc_ref[rs, :] = alpha2[rs] * acc_ref[rs, :] + _dot(pb[rs], vb_ref[0:tk, pp * 2 * dh:(pp + 1) * 2 * dh])

    @pl.when(j < n_steps)
    def _past_pages():
        for g in range(pg):
            ts = slice(g * PAGE_SIZE, (g + 1) * PAGE_SIZE)
            for hd in range(n_qk):
                kb_ref[ts, hd * dh:(hd + 1) * dh] = ck_refs[g][pl.ds(hd, PAGE_SIZE, stride=n_qk), :].astype(BF16)
            for hv in range(A_HEADS):
                for half in range(2):
                    c0 = hv * 2 * dh + half * LANES
                    vb_ref[ts, c0:c0 + LANES] = (
                        cv_refs[2 * g + half][pl.ds(hv, PAGE_SIZE, stride=A_HEADS), :].astype(BF16))
        process(pg * PAGE_SIZE, None)

    @pl.when(j == n_steps)
    def _new_keys():
        pad = jnp.zeros((PAGE_SIZE - lq, D_MODEL), F32)
        kb_ref[0:PAGE_SIZE, :] = jnp.concatenate([kn_ref[...], pad], axis=0).astype(BF16)
        vb_ref[0:PAGE_SIZE, :] = jnp.concatenate([vn_ref[...], pad], axis=0).astype(BF16)
        r = lax.broadcasted_iota(jnp.int32, (rows, LANES), 0)
        cc = lax.broadcasted_iota(jnp.int32, (rows, LANES), 1)
        process(PAGE_SIZE, cc <= jnp.bitwise_and(r, lq - 1))
        lam = _lambda_value(lam_ref, lam_init)
        for pp in range(A_HEADS):
            r1 = slice(2 * pp * lq, (2 * pp + 1) * lq)
            r2 = slice((2 * pp + 1) * lq, (2 * pp + 2) * lq)
            o1 = acc_ref[r1, :] / _rep(l_ref[r1, :], 2)
            o2 = acc_ref[r2, :] / _rep(l_ref[r2, :], 2)
            o_ref[:, pp * 2 * dh:(pp + 1) * 2 * dh] = _subln(o1 - lam * o2, sw_ref[...], lam_init).astype(o_ref.dtype)


def _small_attn(q, k, v, cache_k, cache_v, page_table, lam_p, sw, j_layer, lam_init,
                *, row0, lq, nb, n_pages, out_dtype):
    rb = row0 // lq
    pg = SATTN_PG if n_pages > 0 else 1
    assert n_pages % pg == 0
    n_steps = n_pages // pg
    last = max(n_steps - 1, 0)
    rows = 2 * A_HEADS * lq
    qspec = pl.BlockSpec((lq, D_MODEL), lambda b, j, pt: (rb + b, 0))

    n_pool = cache_k.shape[0]
    cache_k2 = cache_k.reshape(n_pool, PAGE_SIZE * 2 * A_HEADS, A_HEAD_DIM)
    cache_v2 = cache_v.reshape(n_pool, PAGE_SIZE * A_HEADS, 2 * A_HEAD_DIM)

    def page_spec(g, n_rows, half):
        return pl.BlockSpec((None, n_rows, LANES),
                            lambda b, j, pt: (pt[b, jnp.minimum(j, last) * pg + g], 0, half))

    kspecs = [page_spec(g, PAGE_SIZE * 2 * A_HEADS, 0) for g in range(pg)]
    vspecs = [page_spec(g, PAGE_SIZE * A_HEADS, half) for g in range(pg) for half in range(2)]
    grid_spec = pltpu.PrefetchScalarGridSpec(
        num_scalar_prefetch=1,
        grid=(nb, n_steps + 1),
        in_specs=[qspec] + kspecs + vspecs + [
            qspec, qspec,
            pl.BlockSpec((None, 4, A_HEAD_DIM), lambda b, j, pt: (j_layer, 0, 0)),
            pl.BlockSpec((None, 1, 2 * A_HEAD_DIM), lambda b, j, pt: (j_layer, 0, 0))],
        out_specs=pl.BlockSpec((lq, D_MODEL), lambda b, j, pt: (b, 0)),
        scratch_shapes=[pltpu.VMEM((rows, D_MODEL), F32), pltpu.VMEM((rows, D_MODEL), BF16),
                        pltpu.VMEM((rows, LANES), F32), pltpu.VMEM((rows, LANES), F32),
                        pltpu.VMEM((rows, 2 * A_HEAD_DIM), F32),
                        pltpu.VMEM((pg * PAGE_SIZE, D_MODEL), BF16),
                        pltpu.VMEM((pg * PAGE_SIZE, D_MODEL), BF16)],
    )
    return pl.pallas_call(
        functools.partial(_sattn_body, lq=lq, n_steps=n_steps, pg=pg, lam_init=lam_init),
        grid_spec=grid_spec,
        out_shape=jax.ShapeDtypeStruct((nb * lq, D_MODEL), out_dtype),
        compiler_params=_cparams(2), name="small_attn",
    )(page_table, q, *([cache_k2] * pg), *([cache_v2] * (2 * pg)), k, v, lam_p, sw)


def _attn_layer(h, hb, k, v, rope_tabs, cache_k, cache_v, page_table, p, layer):
    j_layer = layer - N_A_LAYERS
    lam_init = 0.8 - 0.6 * math.exp(-0.3 * layer)
    q = _matmul(hb, p['a_w_q'], layer=j_layer, n_out=D_MODEL, tn=1024, rope=rope_tabs,
                scale=A_HEAD_DIM ** -0.5)
    lam_p = p['a_lambda'].astype(F32)
    sw = p['a_subln_w'].astype(F32).reshape(-1, 1, 2 * A_HEAD_DIM)
    o_x = _flash_prompt(q, k, v, lam_p, sw, j_layer, lam_init)
    dummy_pt = jnp.zeros((BATCH, 1), jnp.int32)
    o_m = _small_attn(q, k, v, cache_k, cache_v, dummy_pt, lam_p, sw, j_layer, lam_init,
                      row0=ROW_META, lq=N_META, nb=BATCH, n_pages=0, out_dtype=BF16)
    o_s = _small_attn(q, k, v, cache_k, cache_v, page_table, lam_p, sw, j_layer, lam_init,
                      row0=ROW_SAMPLE, lq=DEC_SEQ, nb=DEC_BATCH, n_pages=page_table.shape[1],
                      out_dtype=F32)
    o = jnp.concatenate([o_x, o_m, o_s.astype(BF16)], axis=0)
    mix = _matmul(o, p['a_w_o'], layer=j_layer, n_out=D_MODEL, tn=1024)
    return _ln_residual(h, mix, p['ln_g'], p['ln_b'], layer)


def _rope_tables():
    pos = jnp.concatenate([
        jnp.tile(N_META + jnp.arange(SEQ, dtype=jnp.int32), BATCH),
        jnp.tile(jnp.arange(N_META, dtype=jnp.int32), BATCH),
        jnp.tile(PAST_LEN + jnp.arange(DEC_SEQ, dtype=jnp.int32), DEC_BATCH)])
    half = A_HEAD_DIM // 2
    inv = ROPE_THETA ** (-jnp.arange(half, dtype=F32) / half)
    ang = pos.astype(F32)[:, None] * inv[None, :]
    cos = jnp.cos(ang)
    sin = jnp.sin(ang)
    return jnp.concatenate([cos, cos], axis=1), jnp.concatenate([-sin, sin], axis=1)


def kernel(x_prompt, x_sample, cache_k, cache_v, page_table, state_ssm, state_conv, meta_tokens, m_w_in, m_conv_w, m_conv_b, m_dt_bias, m_a_log, m_d, m_norm_w, m_w_out, a_w_q, a_w_k, a_w_v, a_lambda, a_subln_w, a_w_o, moe_w_router, moe_b_router, moe_w_gu, moe_b_gu, moe_w_down, moe_b_down, ln_g, ln_b):
    p = dict(m_w_in=m_w_in, m_conv_w=m_conv_w, m_conv_b=m_conv_b, m_dt_bias=m_dt_bias, m_a_log=m_a_log,
             m_d=m_d, m_norm_w=m_norm_w, m_w_out=m_w_out, a_w_q=a_w_q, a_w_k=a_w_k, a_w_v=a_w_v,
             a_lambda=a_lambda, a_subln_w=a_subln_w, a_w_o=a_w_o, moe_w_router=moe_w_router,
             moe_b_router=moe_b_router, moe_w_gu=moe_w_gu, moe_b_gu=moe_b_gu, moe_w_down=moe_w_down,
             moe_b_down=moe_b_down, ln_g=ln_g, ln_b=ln_b)
    assert page_table.shape[1] * PAGE_SIZE == PAST_LEN
    meta = jnp.broadcast_to(meta_tokens[None].astype(F32), (BATCH, N_META, D_MODEL))
    h = jnp.concatenate([x_prompt.reshape(ROWS_X, D_MODEL), meta.reshape(ROWS_META, D_MODEL),
                         x_sample.reshape(ROWS_SAMPLE, D_MODEL)], axis=0)
    hb = h.astype(BF16)
    rope_tabs = _rope_tables()

    ssm_p, conv_p, ssm_s, conv_s = [], [], [], []
    k = v = None
    for layer in range(DEPTH):
        if layer < N_A_LAYERS:
            h, (h_x, c_x, h_s, c_s) = _mamba_layer(h, hb, state_ssm, state_conv, p, layer)
            ssm_p.append(h_x)
            conv_p.append(c_x)
            ssm_s.append(h_s)
            conv_s.append(c_s)
        else:
            if layer == N_A_LAYERS:
                k = _matmul(hb, a_w_k, n_out=D_MODEL, tn=1024, rope=rope_tabs)
                v = _matmul(hb, a_w_v, n_out=D_MODEL, tn=1024)
            h = _attn_layer(h, hb, k, v, rope_tabs, cache_k, cache_v, page_table, p, layer)
        h, hb = _moe_layer(h, p, layer, with_bf16=layer + 1 < DEPTH)

    def prompt_rows(a):
        return jnp.concatenate([a[ROW_META:ROW_SAMPLE].reshape(BATCH, N_META, D_MODEL),
                                a[:ROWS_X].reshape(BATCH, SEQ, D_MODEL)], axis=1)

    y_prompt = h[:ROWS_X].reshape(BATCH, SEQ, D_MODEL)
    y_sample = h[ROW_SAMPLE:].reshape(DEC_BATCH, DEC_SEQ, D_MODEL)
    k_prompt = prompt_rows(k).reshape(BATCH, SEQ + N_META, 2 * A_HEADS, A_HEAD_DIM)
    v_prompt = prompt_rows(v).reshape(BATCH, SEQ + N_META, A_HEADS, 2 * A_HEAD_DIM)
    k_sample = k[ROW_SAMPLE:].reshape(DEC_BATCH, DEC_SEQ, 2 * A_HEADS, A_HEAD_DIM)
    v_sample = v[ROW_SAMPLE:].reshape(DEC_BATCH, DEC_SEQ, A_HEADS, 2 * A_HEAD_DIM)
    return (y_prompt, y_sample, k_prompt, v_prompt, jnp.stack(ssm_p), jnp.stack(conv_p),
            k_sample, v_sample, jnp.stack(ssm_s), jnp.stack(conv_s))
```

```python
import functools
import math

import jax
import jax.numpy as jnp
from jax import lax
from jax.experimental import pallas as pl
from jax.experimental.pallas import tpu as pltpu

F32 = jnp.float32
BF16 = jnp.bfloat16

D_MODEL = 2048
BATCH = 4
SEQ = 2048
DEPTH = 4
DEC_BATCH = 8
DEC_SEQ = 8
PAGE_SIZE = 128
PAST_LEN = 16384
N_A_LAYERS = DEPTH // 2
N_META = 16
M_D_INNER = 2 * D_MODEL
M_HEADDIM = 64
M_NHEADS = M_D_INNER // M_HEADDIM
M_NGROUPS = 8
M_DSTATE = 128
M_CONV = 4
M_CONV_DIM = M_D_INNER + 2 * M_NGROUPS * M_DSTATE
A_HEADS = 8
A_HEAD_DIM = D_MODEL // (2 * A_HEADS)
ROPE_THETA = 10000.0
N_EXPERTS = 32
TOP_K = 4
D_FF = D_MODEL // 4
SWIGLU_LIMIT = 7.0
SWIGLU_ALPHA = 1.702
DN_ALPHA = (2 * DEPTH) ** 0.25
NORM_EPS = 1e-5

ROWS_X = BATCH * SEQ
ROW_META = ROWS_X
ROWS_META = BATCH * N_META
ROW_SAMPLE = ROW_META + ROWS_META
ROWS_SAMPLE = DEC_BATCH * DEC_SEQ
N_ROWS = ROW_SAMPLE + ROWS_SAMPLE

LANES = 128
VMEM_LIMIT = 56 * 1024 * 1024

SSD_CHUNK = 128
MOE_TM = 256
MOE_BLOCKS = N_ROWS * TOP_K // MOE_TM + N_EXPERTS
MOE_ROWS = MOE_BLOCKS * MOE_TM
MM_TM = 640
LN_TR = 320
FLASH_T = 512


def _cparams(n_axes):
    return pltpu.CompilerParams(dimension_semantics=("arbitrary",) * n_axes,
                                vmem_limit_bytes=VMEM_LIMIT)


def _dot(a, b):
    return jnp.dot(a, b, preferred_element_type=F32)


def _dot_nt(a, b):
    return lax.dot_general(a, b, (((1,), (1,)), ((), ())), preferred_element_type=F32)


def _split2(x):
    hi = x.astype(BF16)
    lo = (x - hi.astype(F32)).astype(BF16)
    return hi, lo


def _split3(x):
    hi = x.astype(BF16)
    r = x - hi.astype(F32)
    mid = r.astype(BF16)
    lo = (r - mid.astype(F32)).astype(BF16)
    return hi, mid, lo


def _mm_body(*refs, rope, scale):
    if rope:
        x_ref, w_ref, cos_ref, sin_ref, o_ref, wbf_ref = refs
    else:
        x_ref, w_ref, o_ref, wbf_ref = refs

    @pl.when(pl.program_id(1) == 0)
    def _cast_weight():
        wbf_ref[...] = w_ref[...].astype(BF16)

    acc = _dot(x_ref[...].astype(BF16), wbf_ref[...])
    if rope:
        cos = cos_ref[...]
        sin = sin_ref[...]
        for j in range(acc.shape[1] // LANES):
            blk = acc[:, j * LANES:(j + 1) * LANES]
            out = (blk * cos + pltpu.roll(blk, LANES // 2, 1) * sin) * scale
            o_ref[:, j * LANES:(j + 1) * LANES] = out.astype(o_ref.dtype)
    else:
        o_ref[...] = acc.astype(o_ref.dtype)


def _matmul(x, w, *, layer=None, col0=0, n_out, tn, out_dtype=F32, rope=None, scale=1.0):
    m, k = x.shape
    tm = MM_TM
    assert m % tm == 0 and n_out % tn == 0 and col0 % tn == 0
    jb = col0 // tn
    if layer is None:
        w_spec = pl.BlockSpec((k, tn), lambda j, i: (0, j + jb))
    else:
        w_spec = pl.BlockSpec((None, k, tn), lambda j, i: (layer, 0, j + jb))
    in_specs = [pl.BlockSpec((tm, k), lambda j, i: (i, 0)), w_spec]
    args = [x, w]
    if rope is not None:
        in_specs += [pl.BlockSpec((tm, LANES), lambda j, i: (i, 0))] * 2
        args += list(rope)
    return pl.pallas_call(
        functools.partial(_mm_body, rope=rope is not None, scale=scale),
        grid=(n_out // tn, m // tm),
        in_specs=in_specs,
        out_specs=pl.BlockSpec((tm, tn), lambda j, i: (i, j)),
        out_shape=jax.ShapeDtypeStruct((m, n_out), out_dtype),
        scratch_shapes=[pltpu.VMEM((k, tn), BF16)],
        compiler_params=_cparams(2),
        name="dense_matmul",
    )(*args)


ROUTE_SUB = 128


def _lane_pack(cols, lane):
    out = jnp.zeros(lane.shape, cols[0].dtype)
    for k, c in enumerate(cols):
        out = jnp.where(lane == k, c, out)
    return out


def _route_body(x_ref, w_ref, b_ref, idx_ref, gate_ref, rank_ref, cnt_ref, run_ref):
    tm = x_ref.shape[0]

    @pl.when(pl.program_id(0) == 0)
    def _init():
        run_ref[...] = jnp.zeros(run_ref.shape, F32)

    xh, xl = _split2(x_ref[...])
    wh, wl = _split2(w_ref[...])
    logits = _dot(xh, wh) + _dot(xh, wl) + _dot(xl, wh) + b_ref[...]
    lane = lax.broadcasted_iota(jnp.int32, (ROUTE_SUB, LANES), 1)
    r_i = lax.broadcasted_iota(jnp.int32, (ROUTE_SUB, ROUTE_SUB), 0)
    c_i = lax.broadcasted_iota(jnp.int32, (ROUTE_SUB, ROUTE_SUB), 1)
    below = jnp.where(r_i > c_i, 1.0, 0.0).astype(BF16)
    running = run_ref[0:1, :]
    for sb in range(tm // ROUTE_SUB):
        rs = slice(sb * ROUTE_SUB, (sb + 1) * ROUTE_SUB)
        rest = logits[rs]
        vals, idxs, hots = [], [], []
        for _ in range(TOP_K):
            mx = jnp.max(rest, axis=1, keepdims=True)
            idx = jnp.min(jnp.where(rest == mx, lane, LANES), axis=1, keepdims=True)
            hot = lane == idx
            rest = jnp.where(hot, -jnp.inf, rest)
            vals.append(mx)
            idxs.append(idx)
            hots.append(hot)
        exps = [jnp.exp(v - vals[0]) for v in vals]
        denom = exps[0] + exps[1] + exps[2] + exps[3]
        idx_ref[rs, :] = _lane_pack(idxs, lane)
        gate_ref[rs, :] = _lane_pack([e / denom for e in exps], lane)
        sel = jnp.zeros((ROUTE_SUB, LANES), F32)
        for hot in hots:
            sel = sel + jnp.where(hot, 1.0, 0.0)
        before = _dot(below, sel.astype(BF16)) + running
        ranks = [jnp.sum(jnp.where(hot, before, 0.0), axis=1, keepdims=True) for hot in hots]
        rank_ref[rs, :] = _lane_pack(ranks, lane).astype(jnp.int32)
        running = running + jnp.sum(sel, axis=0, keepdims=True)
    run_ref[...] = jnp.broadcast_to(running, run_ref.shape)
    cnt_ref[...] = jnp.broadcast_to(running, cnt_ref.shape)


def _route(h, w_router, b_router, layer):
    tm = MM_TM
    w_pad = jnp.pad(w_router, ((0, 0), (0, 0), (0, LANES - N_EXPERTS)))
    b_pad = jnp.pad(b_router.astype(F32), ((0, 0), (0, LANES - N_EXPERTS)),
                    constant_values=-1e30).reshape(DEPTH, 1, LANES)
    row = pl.BlockSpec((tm, LANES), lambda i: (i, 0))
    return pl.pallas_call(
        _route_body,
        grid=(N_ROWS // tm,),
        in_specs=[pl.BlockSpec((tm, D_MODEL), lambda i: (i, 0)),
                  pl.BlockSpec((None, D_MODEL, LANES), lambda i: (layer, 0, 0)),
                  pl.BlockSpec((None, 1, LANES), lambda i: (layer, 0, 0))],
        out_specs=(row, row, row, pl.BlockSpec((8, LANES), lambda i: (0, 0))),
        out_shape=(jax.ShapeDtypeStruct((N_ROWS, LANES), jnp.int32),
                   jax.ShapeDtypeStruct((N_ROWS, LANES), F32),
                   jax.ShapeDtypeStruct((N_ROWS, LANES), jnp.int32),
                   jax.ShapeDtypeStruct((8, LANES), F32)),
        scratch_shapes=[pltpu.VMEM((8, LANES), F32)],
        compiler_params=_cparams(1),
        name="moe_route",
    )(h, w_pad, b_pad)


def _layer_norm_rows(x, g, b):
    mu = jnp.mean(x, axis=-1, keepdims=True)
    xc = x - mu
    var = jnp.mean(xc * xc, axis=-1, keepdims=True)
    return xc * lax.rsqrt(var + NORM_EPS) * g + b


PACK_W = D_MODEL // 2
PACK_T = PACK_W // LANES
ROW_T = D_MODEL // LANES


def _pack_rows(y, out_ref):
    rows = y.shape[0]
    bits = pltpu.bitcast(y.astype(BF16).astype(F32), jnp.uint32)
    packed = bits[:, :PACK_W] | lax.shift_right_logical(bits[:, PACK_W:], jnp.uint32(16))
    for c in range(PACK_T):
        out_ref[pl.ds(c, rows, stride=PACK_T), :] = packed[:, c * LANES:(c + 1) * LANES]


def _unpack_rows(x_ref, out_ref, rows):
    for c in range(PACK_T):
        w = x_ref[pl.ds(c, rows, stride=PACK_T), :]
        hi = pltpu.bitcast(w & jnp.uint32(0xFFFF0000), F32)
        lo = pltpu.bitcast(lax.shift_left(w, jnp.uint32(16)), F32)
        out_ref[:, c * LANES:(c + 1) * LANES] = hi.astype(BF16)
        out_ref[:, PACK_W + c * LANES:PACK_W + (c + 1) * LANES] = lo.astype(BF16)


def _ln_body(h_ref, m_ref, g_ref, b_ref, o_ref, op_ref):
    y = _layer_norm_rows(DN_ALPHA * h_ref[...] + m_ref[...], g_ref[...], b_ref[...])
    o_ref[...] = y
    _pack_rows(y, op_ref)


def _ln_residual(h, mix, ln_g, ln_b, layer):
    tr = LN_TR
    row = pl.BlockSpec((tr, D_MODEL), lambda i: (i, 0))
    par = pl.BlockSpec((None, None, 1, D_MODEL), lambda i: (layer, 0, 0, 0))
    return pl.pallas_call(
        _ln_body, grid=(N_ROWS // tr,),
        in_specs=[row, row, par, par],
        out_specs=(row, pl.BlockSpec((tr * PACK_T, LANES), lambda i: (i, 0))),
        out_shape=(jax.ShapeDtypeStruct((N_ROWS, D_MODEL), F32),
                   jax.ShapeDtypeStruct((N_ROWS * PACK_T, LANES), jnp.uint32)),
        compiler_params=_cparams(1), name="ln_residual",
    )(h, mix, ln_g.reshape(DEPTH, 2, 1, D_MODEL), ln_b.reshape(DEPTH, 2, 1, D_MODEL))


MOE_TB = 128
MOE_NBLK = N_ROWS // MOE_TB
MOE_IDX = MOE_TB * TOP_K


def _idx_copy(dest_hbm, idx_smem, isem, blk, slot):
    return pltpu.make_async_copy(dest_hbm.at[blk], idx_smem.at[slot], isem.at[slot])


def _tile_rows(row, n):
    return pl.ds(pl.multiple_of(row * n, n), n)


def _dispatch_body(pstart_ref, pcnt_ref, nused_ref, dest_hbm, h_ref, xs_hbm, idx_smem, zero_ref,
                   isem, rsem, psem, tsem):
    i = pl.program_id(0)
    slot = i % 2
    blk_rows = MOE_TM * PACK_T

    @pl.when(i == 0)
    def _first():
        _idx_copy(dest_hbm, idx_smem, isem, 0, 0).start()
        zero_ref[...] = jnp.zeros(zero_ref.shape, jnp.uint32)

        def tail_copy(blk):
            return pltpu.make_async_copy(zero_ref, xs_hbm.at[_tile_rows(blk, blk_rows)], tsem)

        def tail_start(blk, carry):
            tail_copy(blk).start()
            return carry

        def tail_wait(blk, carry):
            tail_copy(blk).wait()
            return carry

        lax.fori_loop(nused_ref[0], MOE_BLOCKS, tail_start, 0)
        total = jnp.int32(0)
        for e in range(N_EXPERTS):
            first = pstart_ref[e]
            cnt = pcnt_ref[e]

            def pad_row(j, carry, first=first):
                pltpu.make_async_copy(zero_ref.at[pl.ds(0, PACK_T)], xs_hbm.at[_tile_rows(first + j, PACK_T)],
                                      psem).start()
                return carry

            lax.fori_loop(0, cnt, pad_row, 0)
            total = total + cnt

        def wait_pad(j, carry):
            pltpu.make_async_copy(zero_ref.at[pl.ds(0, PACK_T)], xs_hbm.at[pl.ds(0, PACK_T)], psem).wait()
            return carry

        lax.fori_loop(0, total, wait_pad, 0)
        lax.fori_loop(nused_ref[0], MOE_BLOCKS, tail_wait, 0)

    _idx_copy(dest_hbm, idx_smem, isem, i, slot).wait()

    @pl.when(i + 1 < MOE_NBLK)
    def _next_idx():
        _idx_copy(dest_hbm, idx_smem, isem, i + 1, 1 - slot).start()

    for r in range(MOE_TB):
        for k in range(TOP_K):
            d = idx_smem[slot, r * TOP_K + k]
            pltpu.make_async_copy(h_ref.at[pl.ds(r * PACK_T, PACK_T)], xs_hbm.at[_tile_rows(d, PACK_T)],
                                  rsem).start(priority=k % 2)
    n_rows = MOE_IDX * PACK_T
    pltpu.make_async_copy(xs_hbm.at[pl.ds(0, n_rows)], xs_hbm.at[pl.ds(0, n_rows)], rsem).wait()


def _dispatch(hp, dest2, pad_start, pad_cnt, n_used):
    grid_spec = pltpu.PrefetchScalarGridSpec(
        num_scalar_prefetch=3,
        grid=(MOE_NBLK,),
        in_specs=[pl.BlockSpec(memory_space=pl.ANY),
                  pl.BlockSpec((MOE_TB * PACK_T, LANES), lambda i, ps, pc, nu: (i, 0))],
        out_specs=pl.BlockSpec(memory_space=pl.ANY),
        scratch_shapes=[pltpu.SMEM((2, MOE_IDX), jnp.int32),
                        pltpu.VMEM((MOE_TM * PACK_T, LANES), jnp.uint32),
                        pltpu.SemaphoreType.DMA((2,)), pltpu.SemaphoreType.DMA(()),
                        pltpu.SemaphoreType.DMA(()), pltpu.SemaphoreType.DMA(())],
    )
    return pl.pallas_call(
        _dispatch_body, grid_spec=grid_spec,
        out_shape=jax.ShapeDtypeStruct((MOE_ROWS * PACK_T, LANES), jnp.uint32),
        compiler_params=_cparams(1), name="moe_dispatch",
    )(pad_start, pad_cnt, n_used, dest2, hp)


def _combine_body(dest_hbm, eo_hbm, h_ref, gate_ref, g_ref, b_ref, *rest, with_bf16):
    if with_bf16:
        o_ref, obf_ref, idx_smem, ebuf, xbuf, isem, rsem = rest
    else:
        o_ref, idx_smem, ebuf, xbuf, isem, rsem = rest
    i = pl.program_id(0)
    slot = i % 2

    def issue_rows(s):
        for r in range(MOE_TB):
            for k in range(TOP_K):
                d = idx_smem[s, r * TOP_K + k]
                pltpu.make_async_copy(eo_hbm.at[_tile_rows(d, ROW_T)], ebuf.at[s, k, pl.ds(r * ROW_T, ROW_T)],
                                      rsem.at[s]).start(priority=k % 2)

    @pl.when(i == 0)
    def _prime():
        first = _idx_copy(dest_hbm, idx_smem, isem, 0, 0)
        first.start()
        first.wait()
        issue_rows(0)
        _idx_copy(dest_hbm, idx_smem, isem, 1, 1).start()

    @pl.when(i + 1 < MOE_NBLK)
    def _prefetch():
        _idx_copy(dest_hbm, idx_smem, isem, i + 1, 1 - slot).wait()
        issue_rows(1 - slot)

        @pl.when(i + 2 < MOE_NBLK)
        def _next_idx():
            _idx_copy(dest_hbm, idx_smem, isem, i + 2, slot).start()

    pltpu.make_async_copy(ebuf.at[slot], ebuf.at[slot], rsem.at[slot]).wait()
    gates = gate_ref[...]
    for c in range(ROW_T):
        cs_ = slice(c * LANES, (c + 1) * LANES)
        f = ebuf[slot, 0, pl.ds(c, MOE_TB, stride=ROW_T), :] * gates[:, 0:1]
        for k in range(1, TOP_K):
            f = f + ebuf[slot, k, pl.ds(c, MOE_TB, stride=ROW_T), :] * gates[:, k:k + 1]
        xbuf[:, cs_] = DN_ALPHA * h_ref[:, cs_] + f
    y = _layer_norm_rows(xbuf[...], g_ref[...], b_ref[...])
    o_ref[...] = y
    if with_bf16:
        obf_ref[...] = y.astype(BF16)


def _ln_moe(h, eo, dest2, gates, ln_g, ln_b, layer, with_bf16):
    row = pl.BlockSpec((MOE_TB, D_MODEL), lambda i: (i, 0))
    par = pl.BlockSpec((None, None, 1, D_MODEL), lambda i: (layer, 1, 0, 0))
    out_shape = [jax.ShapeDtypeStruct((N_ROWS, D_MODEL), F32)]
    if with_bf16:
        out_shape.append(jax.ShapeDtypeStruct((N_ROWS, D_MODEL), BF16))
    outs = pl.pallas_call(
        functools.partial(_combine_body, with_bf16=with_bf16),
        grid=(MOE_NBLK,),
        in_specs=[pl.BlockSpec(memory_space=pl.ANY), pl.BlockSpec(memory_space=pl.ANY), row,
                  pl.BlockSpec((MOE_TB, LANES), lambda i: (i, 0)), par, par],
        out_specs=tuple([row] * len(out_shape)),
        out_shape=tuple(out_shape),
        scratch_shapes=[pltpu.SMEM((2, MOE_IDX), jnp.int32),
                        pltpu.VMEM((2, TOP_K, MOE_TB * ROW_T, LANES), F32),
                        pltpu.VMEM((MOE_TB, D_MODEL), F32),
                        pltpu.SemaphoreType.DMA((2,)), pltpu.SemaphoreType.DMA((2,))],
        compiler_params=_cparams(1), name="ln_moe_combine",
    )(dest2, eo, h, gates, ln_g.reshape(DEPTH, 2, 1, D_MODEL), ln_b.reshape(DEPTH, 2, 1, D_MODEL))
    return outs if with_bf16 else (outs[0], None)


def _moe_body(be_ref, par_ref, nxt_ref, nb_ref, x_ref, wgu_hbm, bgu_ref, wd_hbm, bd_ref, o_ref,
              wgu_f32, wd_f32, wgu_bf, wd_bf, xb_ref, wsem, *, layer):
    i = pl.program_id(0)
    e = be_ref[i]
    prev = be_ref[jnp.maximum(i - 1, 0)]
    slot = par_ref[i]
    used = i < nb_ref[0]

    def fetch(expert, s):
        return (pltpu.make_async_copy(wgu_hbm.at[layer, expert], wgu_f32.at[s], wsem.at[0, s]),
                pltpu.make_async_copy(wd_hbm.at[layer, expert], wd_f32.at[s], wsem.at[1, s]))

    @pl.when(i == 0)
    def _cold_start():
        for cp in fetch(e, 0):
            cp.start()

    @pl.when(jnp.logical_and(used, jnp.logical_or(i == 0, e != prev)))
    def _switch_expert():
        for cp in fetch(e, slot):
            cp.wait()
        nxt = nxt_ref[i]

        @pl.when(nxt >= 0)
        def _prefetch_next():
            for cp in fetch(nxt, 1 - slot):
                cp.start()

        wgu_bf[...] = wgu_f32[slot].astype(BF16)
        wd_bf[...] = wd_f32[slot].astype(BF16)

    @pl.when(used)
    def _compute():
        _unpack_rows(x_ref, xb_ref, MOE_TM)
        gu = _dot(xb_ref[...], wgu_bf[...]) + bgu_ref[...]
        gate = jnp.minimum(gu[:, :D_FF], SWIGLU_LIMIT)
        up = jnp.clip(gu[:, D_FF:], -SWIGLU_LIMIT, SWIGLU_LIMIT)
        act = (up + 1.0) * gate * jax.nn.sigmoid(SWIGLU_ALPHA * gate)
        out = _dot(act.astype(BF16), wd_bf[...]) + bd_ref[...]
        for c in range(ROW_T):
            o_ref[pl.ds(c, MOE_TM, stride=ROW_T), :] = out[:, c * LANES:(c + 1) * LANES]

    @pl.when(jnp.logical_not(used))
    def _unused_block():
        o_ref[...] = jnp.zeros(o_ref.shape, o_ref.dtype)


def _moe_experts(xs, blk_e, blk_par, blk_nxt, n_used, w_gu, b_gu, w_down, b_down, layer):
    grid_spec = pltpu.PrefetchScalarGridSpec(
        num_scalar_prefetch=4,
        grid=(MOE_BLOCKS,),
        in_specs=[
            pl.BlockSpec((MOE_TM * PACK_T, LANES), lambda i, be, pa, nx, nb: (jnp.minimum(i, nb[0] - 1), 0)),
            pl.BlockSpec(memory_space=pl.ANY),
            pl.BlockSpec((None, None, 1, 2 * D_FF), lambda i, be, pa, nx, nb: (layer, be[i], 0, 0)),
            pl.BlockSpec(memory_space=pl.ANY),
            pl.BlockSpec((None, None, 1, D_MODEL), lambda i, be, pa, nx, nb: (layer, be[i], 0, 0)),
        ],
        out_specs=pl.BlockSpec((MOE_TM * ROW_T, LANES), lambda i, be, pa, nx, nb: (i, 0)),
        scratch_shapes=[pltpu.VMEM((2, D_MODEL, 2 * D_FF), F32), pltpu.VMEM((2, D_FF, D_MODEL), F32),
                        pltpu.VMEM((D_MODEL, 2 * D_FF), BF16), pltpu.VMEM((D_FF, D_MODEL), BF16),
                        pltpu.VMEM((MOE_TM, D_MODEL), BF16),
                        pltpu.SemaphoreType.DMA((2, 2))],
    )
    return pl.pallas_call(
        functools.partial(_moe_body, layer=layer), grid_spec=grid_spec,
        out_shape=jax.ShapeDtypeStruct((MOE_ROWS * ROW_T, LANES), F32),
        compiler_params=_cparams(1), name="moe_experts",
    )(blk_e, blk_par, blk_nxt, n_used, xs, w_gu, b_gu.reshape(DEPTH, N_EXPERTS, 1, 2 * D_FF),
      w_down, b_down.reshape(DEPTH, N_EXPERTS, 1, D_MODEL))


def _moe_layer(h, hp, p, layer, with_bf16):
    idx, gates, rank, cnt = _route(h, p['moe_w_router'], p['moe_b_router'], layer)
    counts = cnt[0, :N_EXPERTS].astype(jnp.int32)
    psz = (counts + MOE_TM - 1) // MOE_TM * MOE_TM
    pend = jnp.cumsum(psz)
    pstart = pend - psz
    expert_ids = jnp.arange(N_EXPERTS, dtype=jnp.int32)
    top_i = idx[:, :TOP_K]
    base = jnp.sum(jnp.where(top_i[:, :, None] == expert_ids, pstart, 0), axis=-1)
    dest2 = (base + rank[:, :TOP_K]).astype(jnp.int32).reshape(MOE_NBLK, MOE_IDX)
    n_used = (pend[-1] // MOE_TM).astype(jnp.int32)
    blk = jnp.arange(MOE_BLOCKS, dtype=jnp.int32)
    blk_e = jnp.sum((blk[:, None] * MOE_TM >= pend[None, :]).astype(jnp.int32), axis=1)
    blk_e = jnp.minimum(blk_e, N_EXPERTS - 1)
    blk_e = jnp.where(blk < n_used, blk_e, blk_e[jnp.maximum(n_used - 1, 0)]).astype(jnp.int32)
    used = counts > 0
    ordinal = jnp.cumsum(used.astype(jnp.int32)) - 1
    cand = jnp.where(used, expert_ids, N_EXPERTS)
    later = jnp.concatenate([lax.cummin(cand[::-1])[::-1][1:], jnp.full((1,), N_EXPERTS, jnp.int32)])
    nxt_e = jnp.where(later >= N_EXPERTS, -1, later)
    blk_hot = blk_e[:, None] == expert_ids
    blk_par = jnp.sum(jnp.where(blk_hot, ordinal % 2, 0), axis=1).astype(jnp.int32)
    blk_nxt = jnp.sum(jnp.where(blk_hot, nxt_e, 0), axis=1).astype(jnp.int32)
    n_used = n_used.reshape(1)
    xs = _dispatch(hp, dest2, (pstart + counts).astype(jnp.int32), (psz - counts).astype(jnp.int32), n_used)
    eo = _moe_experts(xs, blk_e, blk_par, blk_nxt, n_used, p['moe_w_gu'], p['moe_b_gu'],
                      p['moe_w_down'], p['moe_b_down'], layer)
    return _ln_moe(h, eo, dest2, gates, p['ln_g'], p['ln_b'], layer, with_bf16)


def _ssd_body(z_ref, xbc_ref, dt_ref, h0_ref, c0_ref, cw_ref, cb_ref, dtb_ref, alog_ref, de_ref, nw_ref,
              eexp_ref, y_ref, hfin_ref, cfin_ref, ht_ref, ext_ref, act_ref, ybuf_ref, *, lv, nc):
    L = SSD_CHUNK
    N = M_DSTATE
    c = pl.program_id(1)
    n_pairs = M_NHEADS // 2
    x_cols = M_D_INNER
    b_col0 = M_D_INNER
    c_col0 = M_D_INNER + M_NGROUPS * M_DSTATE

    @pl.when(c == 0)
    def _init():
        for i in range(n_pairs):
            ht_ref[i] = h0_ref[2 * i:2 * i + 2].reshape(2 * M_HEADDIM, N).T
        ext_ref[0:8, :] = jnp.zeros((8, M_CONV_DIM), F32)
        ext_ref[5:8, :] = c0_ref[...]

    raw = xbc_ref[...]
    if lv < L:
        raw = jnp.concatenate([raw, jnp.zeros((L - lv, M_CONV_DIM), F32)], axis=0)
    ext_ref[8:8 + L, :] = raw
    cw_step = 256
    for j in range(M_CONV_DIM // cw_step):
        cs_ = slice(j * cw_step, (j + 1) * cw_step)
        acc = cb_ref[:, cs_] + cw_ref[0:1, cs_] * ext_ref[5:5 + L, cs_]
        acc = acc + cw_ref[1:2, cs_] * ext_ref[6:6 + L, cs_]
        acc = acc + cw_ref[2:3, cs_] * ext_ref[7:7 + L, cs_]
        acc = acc + cw_ref[3:4, cs_] * ext_ref[8:8 + L, cs_]
        act_ref[:, cs_] = jax.nn.silu(acc)
    carry = ext_ref[5 + lv:8 + lv, :]
    ext_ref[5:8, :] = carry
    cfin_ref[...] = carry

    dtr = dt_ref[...]
    if lv < L:
        dtr = jnp.concatenate([dtr, jnp.zeros((L - lv, LANES), F32)], axis=0)
    dt = jax.nn.softplus(dtr + dtb_ref[...])
    row = lax.broadcasted_iota(jnp.int32, (L, L), 0)
    col = lax.broadcasted_iota(jnp.int32, (L, L), 1)
    if lv < L:
        dt = jnp.where(row < lv, dt, 0.0)
    da = dt * (-jnp.exp(alog_ref[...]))
    causal = row >= col
    tri = jnp.where(causal, 1.0, 0.0).astype(BF16)
    d_hi, d_mid, d_lo = _split3(da)
    cs = _dot(tri, d_hi) + _dot(tri, d_mid) + _dot(tri, d_lo)
    cs_end = cs[L - 1:L, :]
    cst = cs.T
    dtt = dt.T
    wt = (dt * jnp.exp(cs_end - cs)).T
    e_hi, e_lo = _split2(jnp.broadcast_to(jnp.exp(cs_end), (8, LANES)))
    dend = _dot(e_hi, eexp_ref[...]) + _dot(e_lo, eexp_ref[...])
    lane = lax.broadcasted_iota(jnp.int32, (L, LANES), 1)
    first_head = lane < M_HEADDIM

    for g in range(M_NGROUPS):
        bg = act_ref[:, b_col0 + g * N:b_col0 + (g + 1) * N]
        cg = act_ref[:, c_col0 + g * N:c_col0 + (g + 1) * N]
        btg = bg.T
        cb = _dot(cg.astype(BF16), btg.astype(BF16))
        for pr in range(M_NHEADS // M_NGROUPS // 2):
            i = g * (M_NHEADS // M_NGROUPS // 2) + pr
            ps = slice(i * LANES, (i + 1) * LANES)
            xs_pair = act_ref[:, ps]
            ht_pair = ht_ref[i]
            lhs_rows = []
            bts_rows = []
            for j in range(2):
                hd = 2 * i + j
                colb = jnp.broadcast_to(cs[:, hd:hd + 1], (L, L))
                seg = colb - cst[hd:hd + 1, :]
                decay = jnp.exp(jnp.where(causal, seg, -jnp.inf))
                m_h = cb * decay * dtt[hd:hd + 1, :]
                c_h = cg * jnp.exp(colb)
                lhs_rows.append(jnp.concatenate([m_h, c_h], axis=1))
                bts_rows.append(btg * wt[hd:hd + 1, :])
            lhs = jnp.concatenate(lhs_rows, axis=0).astype(BF16)
            rhs = jnp.concatenate([xs_pair, ht_pair], axis=0).astype(BF16)
            out = _dot(lhs, rhs)
            y_pair = jnp.where(first_head, out[:L], out[L:]) + xs_pair * de_ref[:, ps]
            ybuf_ref[:, ps] = y_pair
            bts = jnp.concatenate(bts_rows, axis=0).astype(BF16)
            upd = _dot(bts, xs_pair.astype(BF16))
            ht_ref[i] = dend[0:1, ps] * ht_pair + jnp.where(first_head, upd[:N], upd[N:])

    gw = M_D_INNER // M_NGROUPS
    for g in range(M_NGROUPS):
        gs = slice(g * gw, (g + 1) * gw)
        yg = ybuf_ref[0:lv, gs] * jax.nn.silu(z_ref[:, gs])
        ms = jnp.mean(yg * yg, axis=-1, keepdims=True)
        y_ref[:, gs] = (yg * lax.rsqrt(ms + NORM_EPS) * nw_ref[:, gs]).astype(y_ref.dtype)

    @pl.when(c == nc - 1)
    def _final():
        for i in range(n_pairs):
            hfin_ref[2 * i:2 * i + 2] = ht_ref[i].T.reshape(2, M_HEADDIM, N)


def _ssd(z, xbc, dtr, h0, h0_layer, c0, c0_layer, prm, *, row0, lv, nb, nc, out_dtype):
    rb = row0 // lv

    def rows(width):
        return pl.BlockSpec((lv, width), lambda b, c: (rb + b * nc + c, 0))

    def par(shape):
        return pl.BlockSpec(shape, lambda b, c: (0, 0))

    if h0_layer is None:
        h0_spec = pl.BlockSpec((None, M_NHEADS, M_HEADDIM, M_DSTATE), lambda b, c: (b, 0, 0, 0))
        c0_spec = pl.BlockSpec((None, M_CONV - 1, M_CONV_DIM), lambda b, c: (b, 0, 0))
    else:
        h0_spec = pl.BlockSpec((None, None, M_NHEADS, M_HEADDIM, M_DSTATE),
                               lambda b, c: (h0_layer, b, 0, 0, 0))
        c0_spec = pl.BlockSpec((None, None, M_CONV - 1, M_CONV_DIM), lambda b, c: (c0_layer, b, 0, 0))
    out_rows = nb * nc * lv
    return pl.pallas_call(
        functools.partial(_ssd_body, lv=lv, nc=nc),
        grid=(nb, nc),
        in_specs=[rows(M_D_INNER), rows(M_CONV_DIM), rows(LANES), h0_spec, c0_spec,
                  par((M_CONV, M_CONV_DIM)), par((1, M_CONV_DIM)), par((1, LANES)), par((1, LANES)),
                  par((1, M_D_INNER)), par((1, M_D_INNER)), par((LANES, M_D_INNER))],
        out_specs=(pl.BlockSpec((lv, M_D_INNER), lambda b, c: (b * nc + c, 0)),
                   pl.BlockSpec((None, M_NHEADS, M_HEADDIM, M_DSTATE), lambda b, c: (b, 0, 0, 0)),
                   pl.BlockSpec((None, M_CONV - 1, M_CONV_DIM), lambda b, c: (b, 0, 0))),
        out_shape=(jax.ShapeDtypeStruct((out_rows, M_D_INNER), out_dtype),
                   jax.ShapeDtypeStruct((nb, M_NHEADS, M_HEADDIM, M_DSTATE), F32),
                   jax.ShapeDtypeStruct((nb, M_CONV - 1, M_CONV_DIM), F32)),
        scratch_shapes=[pltpu.VMEM((M_NHEADS // 2, M_DSTATE, LANES), F32),
                        pltpu.VMEM((8 + SSD_CHUNK, M_CONV_DIM), F32),
                        pltpu.VMEM((SSD_CHUNK, M_CONV_DIM), F32),
                        pltpu.VMEM((SSD_CHUNK, M_D_INNER), F32)],
        compiler_params=_cparams(2), name="ssd_mixer",
    )(z, xbc, dtr, h0, c0, *prm)


def _mamba_layer(h, hb, state_ssm, state_conv, p, layer):
    w_in = p['m_w_in']
    z = _matmul(hb, w_in, layer=layer, col0=0, n_out=M_D_INNER, tn=1024)
    xbc = _matmul(hb, w_in, layer=layer, col0=M_D_INNER, n_out=M_CONV_DIM, tn=1024)
    w_dt = jnp.pad(w_in[layer, :, M_D_INNER + M_CONV_DIM:], ((0, 0), (0, LANES - M_NHEADS)))
    dtr = _matmul(hb, w_dt, n_out=LANES, tn=LANES)

    def pad_heads(v):
        return jnp.pad(v[layer].astype(F32), (0, LANES - M_NHEADS)).reshape(1, LANES)

    head_of_col = jnp.arange(M_D_INNER, dtype=jnp.int32) // M_HEADDIM
    expand = (jnp.arange(LANES, dtype=jnp.int32)[:, None] == head_of_col[None, :]).astype(BF16)
    prm = (p['m_conv_w'][layer], p['m_conv_b'][layer].reshape(1, M_CONV_DIM),
           pad_heads(p['m_dt_bias']), pad_heads(p['m_a_log']),
           jnp.repeat(p['m_d'][layer].astype(F32), M_HEADDIM).reshape(1, M_D_INNER),
           p['m_norm_w'][layer].reshape(1, M_D_INNER), expand)
    zeros_h = jnp.zeros((BATCH, M_NHEADS, M_HEADDIM, M_DSTATE), F32)
    zeros_c = jnp.zeros((BATCH, M_CONV - 1, M_CONV_DIM), F32)
    y_m, h_m, c_m = _ssd(z, xbc, dtr, zeros_h, None, zeros_c, None, prm,
                         row0=ROW_META, lv=N_META, nb=BATCH, nc=1, out_dtype=BF16)
    y_x, h_x, c_x = _ssd(z, xbc, dtr, h_m, None, c_m, None, prm,
                         row0=0, lv=SSD_CHUNK, nb=BATCH, nc=SEQ // SSD_CHUNK, out_dtype=BF16)
    y_s, h_s, c_s = _ssd(z, xbc, dtr, state_ssm, layer, state_conv, layer, prm,
                         row0=ROW_SAMPLE, lv=DEC_SEQ, nb=DEC_BATCH, nc=1, out_dtype=F32)
    yn = jnp.concatenate([y_x, y_m, y_s.astype(BF16)], axis=0)
    mix = _matmul(yn, p['m_w_out'], layer=layer, n_out=D_MODEL, tn=512)
    h, hp = _ln_residual(h, mix, p['ln_g'], p['ln_b'], layer)
    return h, hp, (h_x, c_x, h_s, c_s)


def _lambda_value(lam_ref, lam_init):
    lp = lam_ref[...]
    s1 = jnp.sum(lp[0:1, :] * lp[1:2, :], axis=-1, keepdims=True)
    s2 = jnp.sum(lp[2:3, :] * lp[3:4, :], axis=-1, keepdims=True)
    return jnp.exp(s1) - jnp.exp(s2) + lam_init


def _subln(o, sw, lam_init):
    ms = jnp.mean(o * o, axis=-1, keepdims=True)
    return o * lax.rsqrt(ms + NORM_EPS) * sw * (1.0 - lam_init)


def _rep(x, n):
    return x if n == 1 else jnp.concatenate([x] * n, axis=1)


def _flash_body(q_ref, k_ref, v_ref, km_ref, vm_ref, lam_ref, sw_ref, o_ref,
                m_ref, l_ref, acc_ref, kb_ref, vb_ref, kpad_ref, vpad_ref, *, lam_init):
    t = FLASH_T
    dh = A_HEAD_DIM
    qi = pl.program_id(2)

    @pl.when(qi == 0)
    def _load_kv():
        kb_ref[...] = k_ref[...].astype(BF16)
        vb_ref[...] = v_ref[...].astype(BF16)
        kpad_ref[...] = jnp.zeros(kpad_ref.shape, BF16)
        vpad_ref[...] = jnp.zeros(vpad_ref.shape, BF16)
        kpad_ref[0:N_META, :] = km_ref[...].astype(BF16)
        vpad_ref[0:N_META, :] = vm_ref[...].astype(BF16)

    m_ref[...] = jnp.full(m_ref.shape, -jnp.inf, F32)
    l_ref[...] = jnp.zeros(l_ref.shape, F32)
    acc_ref[...] = jnp.zeros(acc_ref.shape, F32)
    qs = [q_ref[:, j * dh:(j + 1) * dh].astype(BF16) for j in range(2)]

    def process(kb, vb, mask):
        for j in range(2):
            s = _dot_nt(qs[j], kb[:, j * dh:(j + 1) * dh])
            if mask is not None:
                s = jnp.where(mask, s, -jnp.inf)
            m_prev = m_ref[j]
            m_new = jnp.maximum(m_prev, jnp.max(s, axis=1, keepdims=True))
            alpha = jnp.exp(m_prev - m_new)
            pmat = jnp.exp(s - _rep(m_new, s.shape[1] // LANES))
            l_ref[j] = alpha * l_ref[j] + jnp.sum(pmat, axis=1, keepdims=True)
            acc_ref[j] = _rep(alpha, 2) * acc_ref[j] + _dot(pmat.astype(BF16), vb)
            m_ref[j] = m_new

    colm = lax.broadcasted_iota(jnp.int32, (t, LANES), 1)
    process(kpad_ref[...], vpad_ref[...], colm < N_META)

    def full_block(kj, carry):
        off = pl.multiple_of(kj * t, t)
        process(kb_ref[pl.ds(off, t), :], vb_ref[pl.ds(off, t), :], None)
        return carry

    lax.fori_loop(0, qi, full_block, 0)

    off = pl.multiple_of(qi * t, t)
    r = lax.broadcasted_iota(jnp.int32, (t, t), 0)
    cc = lax.broadcasted_iota(jnp.int32, (t, t), 1)
    process(kb_ref[pl.ds(off, t), :], vb_ref[pl.ds(off, t), :], cc <= r)
    lam = _lambda_value(lam_ref, lam_init)
    o1 = acc_ref[0] / _rep(l_ref[0], 2)
    o2 = acc_ref[1] / _rep(l_ref[1], 2)
    o_ref[...] = _subln(o1 - lam * o2, sw_ref[...], lam_init).astype(o_ref.dtype)


def _flash_prompt(q, k, v, lam_p, sw, j_layer, lam_init):
    t = FLASH_T
    nq = SEQ // t
    w2 = 2 * A_HEAD_DIM
    qspec = pl.BlockSpec((t, w2), lambda b, p, qi: (b * nq + qi, p))
    kspec = pl.BlockSpec((SEQ, w2), lambda b, p, qi: (b, p))
    mspec = pl.BlockSpec((N_META, w2), lambda b, p, qi: (ROW_META // N_META + b, p))
    return pl.pallas_call(
        functools.partial(_flash_body, lam_init=lam_init),
        grid=(BATCH, A_HEADS, nq),
        in_specs=[qspec, kspec, kspec, mspec, mspec,
                  pl.BlockSpec((None, 4, A_HEAD_DIM), lambda b, p, qi: (j_layer, 0, 0)),
                  pl.BlockSpec((None, 1, w2), lambda b, p, qi: (j_layer, 0, 0))],
        out_specs=pl.BlockSpec((t, w2), lambda b, p, qi: (b * nq + qi, p)),
        out_shape=jax.ShapeDtypeStruct((ROWS_X, D_MODEL), BF16),
        scratch_shapes=[pltpu.VMEM((2, t, LANES), F32), pltpu.VMEM((2, t, LANES), F32),
                        pltpu.VMEM((2, t, w2), F32),
                        pltpu.VMEM((SEQ, w2), BF16), pltpu.VMEM((SEQ, w2), BF16),
                        pltpu.VMEM((LANES, w2), BF16), pltpu.VMEM((LANES, w2), BF16)],
        compiler_params=_cparams(3), name="flash_prompt",
    )(q, k, v, k, v, lam_p, sw)


SATTN_PG = 8


def _sattn_body(pt_ref, q_ref, *refs, lq, n_steps, pg, lam_init):
    del pt_ref
    ck_refs = refs[:pg]
    cv_refs = refs[pg:3 * pg]
    (kn_ref, vn_ref, lam_ref, sw_ref, o_ref, qf_ref, qb_ref, m_ref, l_ref, acc_ref,
     kb_ref, vb_ref) = refs[3 * pg:]
    dh = A_HEAD_DIM
    n_qk = 2 * A_HEADS
    rows = n_qk * lq
    j = pl.program_id(1)

    @pl.when(j == 0)
    def _init():
        qf_ref[...] = jnp.zeros(qf_ref.shape, F32)
        for hd in range(n_qk):
            qf_ref[hd * lq:(hd + 1) * lq, hd * dh:(hd + 1) * dh] = q_ref[:, hd * dh:(hd + 1) * dh]
        qb_ref[...] = qf_ref[...].astype(BF16)
        m_ref[...] = jnp.full(m_ref.shape, -jnp.inf, F32)
        l_ref[...] = jnp.zeros(l_ref.shape, F32)
        acc_ref[...] = jnp.zeros(acc_ref.shape, F32)

    def process(tk, mask):
        s = _dot_nt(qb_ref[...], kb_ref[0:tk, :])
        if mask is not None:
            s = jnp.where(mask, s, -jnp.inf)
        m_prev = m_ref[...]
        m_new = jnp.maximum(m_prev, jnp.max(s, axis=1, keepdims=True))
        alpha = jnp.exp(m_prev - m_new)
        pmat = jnp.exp(s - _rep(m_new, tk // LANES))
        l_ref[...] = alpha * l_ref[...] + jnp.sum(pmat, axis=1, keepdims=True)
        m_ref[...] = m_new
        pb = pmat.astype(BF16)
        alpha2 = _rep(alpha, 2)
        for pp in range(A_HEADS):
            rs = slice(2 * pp * lq, (2 * pp + 2) * lq)
            acc_ref[rs, :] = alpha2[rs] * acc_ref[rs, :] + _dot(pb[rs], vb_ref[0:tk, pp * 2 * dh:(pp + 1) * 2 * dh])

    @pl.when(j < n_steps)
    def _past_pages():
        for g in range(pg):
            ts = slice(g * PAGE_SIZE, (g + 1) * PAGE_SIZE)
            for hd in range(n_qk):
                kb_ref[ts, hd * dh:(hd + 1) * dh] = ck_refs[g][pl.ds(hd, PAGE_SIZE, stride=n_qk), :].astype(BF16)
            for hv in range(A_HEADS):
                for half in range(2):
                    c0 = hv * 2 * dh + half * LANES
                    vb_ref[ts, c0:c0 + LANES] = (
                        cv_refs[2 * g + half][pl.ds(hv, PAGE_SIZE, stride=A_HEADS), :].astype(BF16))
        process(pg * PAGE_SIZE, None)

    @pl.when(j == n_steps)
    def _new_keys():
        pad = jnp.zeros((PAGE_SIZE - lq, D_MODEL), F32)
        kb_ref[0:PAGE_SIZE, :] = jnp.concatenate([kn_ref[...], pad], axis=0).astype(BF16)
        vb_ref[0:PAGE_SIZE, :] = jnp.concatenate([vn_ref[...], pad], axis=0).astype(BF16)
        r = lax.broadcasted_iota(jnp.int32, (rows, LANES), 0)
        cc = lax.broadcasted_iota(jnp.int32, (rows, LANES), 1)
        process(PAGE_SIZE, cc <= jnp.bitwise_and(r, lq - 1))
        lam = _lambda_value(lam_ref, lam_init)
        for pp in range(A_HEADS):
            r1 = slice(2 * pp * lq, (2 * pp + 1) * lq)
            r2 = slice((2 * pp + 1) * lq, (2 * pp + 2) * lq)
            o1 = acc_ref[r1, :] / _rep(l_ref[r1, :], 2)
            o2 = acc_ref[r2, :] / _rep(l_ref[r2, :], 2)
            o_ref[:, pp * 2 * dh:(pp + 1) * 2 * dh] = _subln(o1 - lam * o2, sw_ref[...], lam_init).astype(o_ref.dtype)


def _small_attn(q, k, v, cache_k, cache_v, page_table, lam_p, sw, j_layer, lam_init,
                *, row0, lq, nb, n_pages, out_dtype):
    rb = row0 // lq
    pg = SATTN_PG if n_pages > 0 else 1
    assert n_pages % pg == 0
    n_steps = n_pages // pg
    last = max(n_steps - 1, 0)
    rows = 2 * A_HEADS * lq
    qspec = pl.BlockSpec((lq, D_MODEL), lambda b, j, pt: (rb + b, 0))

    n_pool = cache_k.shape[0]
    cache_k2 = cache_k.reshape(n_pool, PAGE_SIZE * 2 * A_HEADS, A_HEAD_DIM)
    cache_v2 = cache_v.reshape(n_pool, PAGE_SIZE * A_HEADS, 2 * A_HEAD_DIM)

    def page_spec(g, n_rows, half):
        return pl.BlockSpec((None, n_rows, LANES),
                            lambda b, j, pt: (pt[b, jnp.minimum(j, last) * pg + g], 0, half))

    kspecs = [page_spec(g, PAGE_SIZE * 2 * A_HEADS, 0) for g in range(pg)]
    vspecs = [page_spec(g, PAGE_SIZE * A_HEADS, half) for g in range(pg) for half in range(2)]
    grid_spec = pltpu.PrefetchScalarGridSpec(
        num_scalar_prefetch=1,
        grid=(nb, n_steps + 1),
        in_specs=[qspec] + kspecs + vspecs + [
            qspec, qspec,
            pl.BlockSpec((None, 4, A_HEAD_DIM), lambda b, j, pt: (j_layer, 0, 0)),
            pl.BlockSpec((None, 1, 2 * A_HEAD_DIM), lambda b, j, pt: (j_layer, 0, 0))],
        out_specs=pl.BlockSpec((lq, D_MODEL), lambda b, j, pt: (b, 0)),
        scratch_shapes=[pltpu.VMEM((rows, D_MODEL), F32), pltpu.VMEM((rows, D_MODEL), BF16),
                        pltpu.VMEM((rows, LANES), F32), pltpu.VMEM((rows, LANES), F32),
                        pltpu.VMEM((rows, 2 * A_HEAD_DIM), F32),
                        pltpu.VMEM((pg * PAGE_SIZE, D_MODEL), BF16),
                        pltpu.VMEM((pg * PAGE_SIZE, D_MODEL), BF16)],
    )
    return pl.pallas_call(
        functools.partial(_sattn_body, lq=lq, n_steps=n_steps, pg=pg, lam_init=lam_init),
        grid_spec=grid_spec,
        out_shape=jax.ShapeDtypeStruct((nb * lq, D_MODEL), out_dtype),
        compiler_params=_cparams(2), name="small_attn",
    )(page_table, q, *([cache_k2] * pg), *([cache_v2] * (2 * pg)), k, v, lam_p, sw)


def _attn_layer(h, hb, k, v, rope_tabs, cache_k, cache_v, page_table, p, layer):
    j_layer = layer - N_A_LAYERS
    lam_init = 0.8 - 0.6 * math.exp(-0.3 * layer)
    q = _matmul(hb, p['a_w_q'], layer=j_layer, n_out=D_MODEL, tn=1024, rope=rope_tabs,
                scale=A_HEAD_DIM ** -0.5)
    lam_p = p['a_lambda'].astype(F32)
    sw = p['a_subln_w'].astype(F32).reshape(-1, 1, 2 * A_HEAD_DIM)
    o_x = _flash_prompt(q, k, v, lam_p, sw, j_layer, lam_init)
    dummy_pt = jnp.zeros((BATCH, 1), jnp.int32)
    o_m = _small_attn(q, k, v, cache_k, cache_v, dummy_pt, lam_p, sw, j_layer, lam_init,
                      row0=ROW_META, lq=N_META, nb=BATCH, n_pages=0, out_dtype=BF16)
    o_s = _small_attn(q, k, v, cache_k, cache_v, page_table, lam_p, sw, j_layer, lam_init,
                      row0=ROW_SAMPLE, lq=DEC_SEQ, nb=DEC_BATCH, n_pages=page_table.shape[1],
                      out_dtype=F32)
    o = jnp.concatenate([o_x, o_m, o_s.astype(BF16)], axis=0)
    mix = _matmul(o, p['a_w_o'], layer=j_layer, n_out=D_MODEL, tn=1024)
    return _ln_residual(h, mix, p['ln_g'], p['ln_b'], layer)


def _rope_tables():
    pos = jnp.concatenate([
        jnp.tile(N_META + jnp.arange(SEQ, dtype=jnp.int32), BATCH),
        jnp.tile(jnp.arange(N_META, dtype=jnp.int32), BATCH),
        jnp.tile(PAST_LEN + jnp.arange(DEC_SEQ, dtype=jnp.int32), DEC_BATCH)])
    half = A_HEAD_DIM // 2
    inv = ROPE_THETA ** (-jnp.arange(half, dtype=F32) / half)
    ang = pos.astype(F32)[:, None] * inv[None, :]
    cos = jnp.cos(ang)
    sin = jnp.sin(ang)
    return jnp.concatenate([cos, cos], axis=1), jnp.concatenate([-sin, sin], axis=1)


def kernel(x_prompt, x_sample, cache_k, cache_v, page_table, state_ssm, state_conv, meta_tokens, m_w_in, m_conv_w, m_conv_b, m_dt_bias, m_a_log, m_d, m_norm_w, m_w_out, a_w_q, a_w_k, a_w_v, a_lambda, a_subln_w, a_w_o, moe_w_router, moe_b_router, moe_w_gu, moe_b_gu, moe_w_down, moe_b_down, ln_g, ln_b):
    p = dict(m_w_in=m_w_in, m_conv_w=m_conv_w, m_conv_b=m_conv_b, m_dt_bias=m_dt_bias, m_a_log=m_a_log,
             m_d=m_d, m_norm_w=m_norm_w, m_w_out=m_w_out, a_w_q=a_w_q, a_w_k=a_w_k, a_w_v=a_w_v,
             a_lambda=a_lambda, a_subln_w=a_subln_w, a_w_o=a_w_o, moe_w_router=moe_w_router,
             moe_b_router=moe_b_router, moe_w_gu=moe_w_gu, moe_b_gu=moe_b_gu, moe_w_down=moe_w_down,
             moe_b_down=moe_b_down, ln_g=ln_g, ln_b=ln_b)
    assert page_table.shape[1] * PAGE_SIZE == PAST_LEN
    meta = jnp.broadcast_to(meta_tokens[None].astype(F32), (BATCH, N_META, D_MODEL))
    h = jnp.concatenate([x_prompt.reshape(ROWS_X, D_MODEL), meta.reshape(ROWS_META, D_MODEL),
                         x_sample.reshape(ROWS_SAMPLE, D_MODEL)], axis=0)
    hb = h.astype(BF16)
    rope_tabs = _rope_tables()

    ssm_p, conv_p, ssm_s, conv_s = [], [], [], []
    k = v = None
    for layer in range(DEPTH):
        if layer < N_A_LAYERS:
            h, hp, (h_x, c_x, h_s, c_s) = _mamba_layer(h, hb, state_ssm, state_conv, p, layer)
            ssm_p.append(h_x)
            conv_p.append(c_x)
            ssm_s.append(h_s)
            conv_s.append(c_s)
        else:
            if layer == N_A_LAYERS:
                k = _matmul(hb, a_w_k, n_out=D_MODEL, tn=1024, rope=rope_tabs)
                v = _matmul(hb, a_w_v, n_out=D_MODEL, tn=1024)
            h, hp = _attn_layer(h, hb, k, v, rope_tabs, cache_k, cache_v, page_table, p, layer)
        h, hb = _moe_layer(h, hp, p, layer, with_bf16=layer + 1 < DEPTH)

    def prompt_rows(a):
        return jnp.concatenate([a[ROW_META:ROW_SAMPLE].reshape(BATCH, N_META, D_MODEL),
                                a[:ROWS_X].reshape(BATCH, SEQ, D_MODEL)], axis=1)

    y_prompt = h[:ROWS_X].reshape(BATCH, SEQ, D_MODEL)
    y_sample = h[ROW_SAMPLE:].reshape(DEC_BATCH, DEC_SEQ, D_MODEL)
    k_prompt = prompt_rows(k).reshape(BATCH, SEQ + N_META, 2 * A_HEADS, A_HEAD_DIM)
    v_prompt = prompt_rows(v).reshape(BATCH, SEQ + N_META, A_HEADS, 2 * A_HEAD_DIM)
    k_sample = k[ROW_SAMPLE:].reshape(DEC_BATCH, DEC_SEQ, 2 * A_HEADS, A_HEAD_DIM)
    v_sample = v[ROW_SAMPLE:].reshape(DEC_BATCH, DEC_SEQ, A_HEADS, 2 * A_HEAD_DIM)
    return (y_prompt, y_sample, k_prompt, v_prompt, jnp.stack(ssm_p), jnp.stack(conv_p),
            k_sample, v_sample, jnp.stack(ssm_s), jnp.stack(conv_s))
```

```python
import functools
import math

import jax
import jax.numpy as jnp
from jax import lax
from jax.experimental import pallas as pl
from jax.experimental.pallas import tpu as pltpu

F32 = jnp.float32
BF16 = jnp.bfloat16

D_MODEL = 2048
BATCH = 4
SEQ = 2048
DEPTH = 4
DEC_BATCH = 8
DEC_SEQ = 8
PAGE_SIZE = 128
PAST_LEN = 16384
N_A_LAYERS = DEPTH // 2
N_META = 16
M_D_INNER = 2 * D_MODEL
M_HEADDIM = 64
M_NHEADS = M_D_INNER // M_HEADDIM
M_NGROUPS = 8
M_DSTATE = 128
M_CONV = 4
M_CONV_DIM = M_D_INNER + 2 * M_NGROUPS * M_DSTATE
A_HEADS = 8
A_HEAD_DIM = D_MODEL // (2 * A_HEADS)
ROPE_THETA = 10000.0
N_EXPERTS = 32
TOP_K = 4
D_FF = D_MODEL // 4
SWIGLU_LIMIT = 7.0
SWIGLU_ALPHA = 1.702
DN_ALPHA = (2 * DEPTH) ** 0.25
NORM_EPS = 1e-5

ROWS_X = BATCH * SEQ
ROW_META = ROWS_X
ROWS_META = BATCH * N_META
ROW_SAMPLE = ROW_META + ROWS_META
ROWS_SAMPLE = DEC_BATCH * DEC_SEQ
N_ROWS = ROW_SAMPLE + ROWS_SAMPLE

LANES = 128
VMEM_LIMIT = 56 * 1024 * 1024

SSD_CHUNK = 128
MOE_TM = 256
MOE_BLOCKS = N_ROWS * TOP_K // MOE_TM + N_EXPERTS
MOE_ROWS = MOE_BLOCKS * MOE_TM
MM_TM = 640
LN_TR = 320
FLASH_T = 512


def _cparams(n_axes):
    return pltpu.CompilerParams(dimension_semantics=("arbitrary",) * n_axes,
                                vmem_limit_bytes=VMEM_LIMIT)


def _dot(a, b):
    return jnp.dot(a, b, preferred_element_type=F32)


def _dot_nt(a, b):
    return lax.dot_general(a, b, (((1,), (1,)), ((), ())), preferred_element_type=F32)


def _split2(x):
    hi = x.astype(BF16)
    lo = (x - hi.astype(F32)).astype(BF16)
    return hi, lo


def _split3(x):
    hi = x.astype(BF16)
    r = x - hi.astype(F32)
    mid = r.astype(BF16)
    lo = (r - mid.astype(F32)).astype(BF16)
    return hi, mid, lo


def _mm_body(*refs, rope, scale):
    if rope:
        x_ref, w_ref, cos_ref, sin_ref, o_ref, wbf_ref = refs
    else:
        x_ref, w_ref, o_ref, wbf_ref = refs

    @pl.when(pl.program_id(1) == 0)
    def _cast_weight():
        wbf_ref[...] = w_ref[...].astype(BF16)

    acc = _dot(x_ref[...].astype(BF16), wbf_ref[...])
    if rope:
        cos = cos_ref[...]
        sin = sin_ref[...]
        for j in range(acc.shape[1] // LANES):
            blk = acc[:, j * LANES:(j + 1) * LANES]
            out = (blk * cos + pltpu.roll(blk, LANES // 2, 1) * sin) * scale
            o_ref[:, j * LANES:(j + 1) * LANES] = out.astype(o_ref.dtype)
    else:
        o_ref[...] = acc.astype(o_ref.dtype)


def _matmul(x, w, *, layer=None, col0=0, n_out, tn, out_dtype=F32, rope=None, scale=1.0):
    m, k = x.shape
    tm = MM_TM
    assert m % tm == 0 and n_out % tn == 0 and col0 % tn == 0
    jb = col0 // tn
    if layer is None:
        w_spec = pl.BlockSpec((k, tn), lambda j, i: (0, j + jb))
    else:
        w_spec = pl.BlockSpec((None, k, tn), lambda j, i: (layer, 0, j + jb))
    in_specs = [pl.BlockSpec((tm, k), lambda j, i: (i, 0)), w_spec]
    args = [x, w]
    if rope is not None:
        in_specs += [pl.BlockSpec((tm, LANES), lambda j, i: (i, 0))] * 2
        args += list(rope)
    return pl.pallas_call(
        functools.partial(_mm_body, rope=rope is not None, scale=scale),
        grid=(n_out // tn, m // tm),
        in_specs=in_specs,
        out_specs=pl.BlockSpec((tm, tn), lambda j, i: (i, j)),
        out_shape=jax.ShapeDtypeStruct((m, n_out), out_dtype),
        scratch_shapes=[pltpu.VMEM((k, tn), BF16)],
        compiler_params=_cparams(2),
        name="dense_matmul",
    )(*args)


ROUTE_SUB = 128


def _lane_pack(cols, lane):
    out = jnp.zeros(lane.shape, cols[0].dtype)
    for k, c in enumerate(cols):
        out = jnp.where(lane == k, c, out)
    return out


def _route_body(x_ref, w_ref, b_ref, idx_ref, gate_ref, rank_ref, cnt_ref, run_ref):
    tm = x_ref.shape[0]

    @pl.when(pl.program_id(0) == 0)
    def _init():
        run_ref[...] = jnp.zeros(run_ref.shape, F32)

    xh, xl = _split2(x_ref[...])
    wh, wl = _split2(w_ref[...])
    logits = _dot(xh, wh) + _dot(xh, wl) + _dot(xl, wh) + b_ref[...]
    lane = lax.broadcasted_iota(jnp.int32, (ROUTE_SUB, LANES), 1)
    r_i = lax.broadcasted_iota(jnp.int32, (ROUTE_SUB, ROUTE_SUB), 0)
    c_i = lax.broadcasted_iota(jnp.int32, (ROUTE_SUB, ROUTE_SUB), 1)
    below = jnp.where(r_i > c_i, 1.0, 0.0).astype(BF16)
    running = run_ref[0:1, :]
    for sb in range(tm // ROUTE_SUB):
        rs = slice(sb * ROUTE_SUB, (sb + 1) * ROUTE_SUB)
        rest = logits[rs]
        vals, idxs, hots = [], [], []
        for _ in range(TOP_K):
            mx = jnp.max(rest, axis=1, keepdims=True)
            idx = jnp.min(jnp.where(rest == mx, lane, LANES), axis=1, keepdims=True)
            hot = lane == idx
            rest = jnp.where(hot, -jnp.inf, rest)
            vals.append(mx)
            idxs.append(idx)
            hots.append(hot)
        exps = [jnp.exp(v - vals[0]) for v in vals]
        denom = exps[0] + exps[1] + exps[2] + exps[3]
        idx_ref[rs, :] = _lane_pack(idxs, lane)
        gate_ref[rs, :] = _lane_pack([e / denom for e in exps], lane)
        sel = jnp.zeros((ROUTE_SUB, LANES), F32)
        for hot in hots:
            sel = sel + jnp.where(hot, 1.0, 0.0)
        before = _dot(below, sel.astype(BF16)) + running
        ranks = [jnp.sum(jnp.where(hot, before, 0.0), axis=1, keepdims=True) for hot in hots]
        rank_ref[rs, :] = _lane_pack(ranks, lane).astype(jnp.int32)
        running = running + jnp.sum(sel, axis=0, keepdims=True)
    run_ref[...] = jnp.broadcast_to(running, run_ref.shape)
    cnt_ref[...] = jnp.broadcast_to(running, cnt_ref.shape)


def _route(h, w_router, b_router, layer):
    tm = MM_TM
    w_pad = jnp.pad(w_router, ((0, 0), (0, 0), (0, LANES - N_EXPERTS)))
    b_pad = jnp.pad(b_router.astype(F32), ((0, 0), (0, LANES - N_EXPERTS)),
                    constant_values=-1e30).reshape(DEPTH, 1, LANES)
    row = pl.BlockSpec((tm, LANES), lambda i: (i, 0))
    return pl.pallas_call(
        _route_body,
        grid=(N_ROWS // tm,),
        in_specs=[pl.BlockSpec((tm, D_MODEL), lambda i: (i, 0)),
                  pl.BlockSpec((None, D_MODEL, LANES), lambda i: (layer, 0, 0)),
                  pl.BlockSpec((None, 1, LANES), lambda i: (layer, 0, 0))],
        out_specs=(row, row, row, pl.BlockSpec((8, LANES), lambda i: (0, 0))),
        out_shape=(jax.ShapeDtypeStruct((N_ROWS, LANES), jnp.int32),
                   jax.ShapeDtypeStruct((N_ROWS, LANES), F32),
                   jax.ShapeDtypeStruct((N_ROWS, LANES), jnp.int32),
                   jax.ShapeDtypeStruct((8, LANES), F32)),
        scratch_shapes=[pltpu.VMEM((8, LANES), F32)],
        compiler_params=_cparams(1),
        name="moe_route",
    )(h, w_pad, b_pad)


def _layer_norm_rows(x, g, b):
    mu = jnp.mean(x, axis=-1, keepdims=True)
    xc = x - mu
    var = jnp.mean(xc * xc, axis=-1, keepdims=True)
    return xc * lax.rsqrt(var + NORM_EPS) * g + b


PACK_W = D_MODEL // 2
PACK_T = PACK_W // LANES
ROW_T = D_MODEL // LANES


def _pack_rows(y, out_ref):
    rows = y.shape[0]
    bits = pltpu.bitcast(y.astype(BF16).astype(F32), jnp.uint32)
    packed = bits[:, :PACK_W] | lax.shift_right_logical(bits[:, PACK_W:], jnp.uint32(16))
    for c in range(PACK_T):
        out_ref[pl.ds(c, rows, stride=PACK_T), :] = packed[:, c * LANES:(c + 1) * LANES]


def _unpack_rows(x_ref, out_ref, rows):
    for c in range(PACK_T):
        w = x_ref[pl.ds(c, rows, stride=PACK_T), :]
        hi = pltpu.bitcast(w & jnp.uint32(0xFFFF0000), F32)
        lo = pltpu.bitcast(lax.shift_left(w, jnp.uint32(16)), F32)
        out_ref[:, c * LANES:(c + 1) * LANES] = hi.astype(BF16)
        out_ref[:, PACK_W + c * LANES:PACK_W + (c + 1) * LANES] = lo.astype(BF16)


def _ln_body(h_ref, m_ref, g_ref, b_ref, o_ref, op_ref):
    y = _layer_norm_rows(DN_ALPHA * h_ref[...] + m_ref[...], g_ref[...], b_ref[...])
    o_ref[...] = y
    _pack_rows(y, op_ref)


def _ln_residual(h, mix, ln_g, ln_b, layer):
    tr = LN_TR
    row = pl.BlockSpec((tr, D_MODEL), lambda i: (i, 0))
    par = pl.BlockSpec((None, None, 1, D_MODEL), lambda i: (layer, 0, 0, 0))
    return pl.pallas_call(
        _ln_body, grid=(N_ROWS // tr,),
        in_specs=[row, row, par, par],
        out_specs=(row, pl.BlockSpec((tr * PACK_T, LANES), lambda i: (i, 0))),
        out_shape=(jax.ShapeDtypeStruct((N_ROWS, D_MODEL), F32),
                   jax.ShapeDtypeStruct((N_ROWS * PACK_T, LANES), jnp.uint32)),
        compiler_params=_cparams(1), name="ln_residual",
    )(h, mix, ln_g.reshape(DEPTH, 2, 1, D_MODEL), ln_b.reshape(DEPTH, 2, 1, D_MODEL))


MOE_TB = 128
MOE_NBLK = N_ROWS // MOE_TB
MOE_IDX = MOE_TB * TOP_K


def _idx_copy(dest_hbm, idx_smem, isem, blk, slot):
    return pltpu.make_async_copy(dest_hbm.at[blk], idx_smem.at[slot], isem.at[slot])


def _tile_rows(row, n):
    return pl.ds(pl.multiple_of(row * n, n), n)


def _dispatch_body(pstart_ref, pcnt_ref, nused_ref, dest_hbm, h_ref, xs_hbm, idx_smem, zero_ref,
                   isem, rsem, psem, tsem):
    i = pl.program_id(0)
    slot = i % 2
    blk_rows = MOE_TM * PACK_T

    @pl.when(i == 0)
    def _first():
        _idx_copy(dest_hbm, idx_smem, isem, 0, 0).start()
        zero_ref[...] = jnp.zeros(zero_ref.shape, jnp.uint32)

        def tail_copy(blk):
            return pltpu.make_async_copy(zero_ref, xs_hbm.at[_tile_rows(blk, blk_rows)], tsem)

        def tail_start(blk, carry):
            tail_copy(blk).start()
            return carry

        def tail_wait(blk, carry):
            tail_copy(blk).wait()
            return carry

        lax.fori_loop(nused_ref[0], MOE_BLOCKS, tail_start, 0)
        total = jnp.int32(0)
        for e in range(N_EXPERTS):
            first = pstart_ref[e]
            cnt = pcnt_ref[e]

            def pad_row(j, carry, first=first):
                pltpu.make_async_copy(zero_ref.at[pl.ds(0, PACK_T)], xs_hbm.at[_tile_rows(first + j, PACK_T)],
                                      psem).start()
                return carry

            lax.fori_loop(0, cnt, pad_row, 0)
            total = total + cnt

        def wait_pad(j, carry):
            pltpu.make_async_copy(zero_ref.at[pl.ds(0, PACK_T)], xs_hbm.at[pl.ds(0, PACK_T)], psem).wait()
            return carry

        lax.fori_loop(0, total, wait_pad, 0)
        lax.fori_loop(nused_ref[0], MOE_BLOCKS, tail_wait, 0)

    _idx_copy(dest_hbm, idx_smem, isem, i, slot).wait()

    @pl.when(i + 1 < MOE_NBLK)
    def _next_idx():
        _idx_copy(dest_hbm, idx_smem, isem, i + 1, 1 - slot).start()

    for r in range(MOE_TB):
        for k in range(TOP_K):
            d = idx_smem[slot, r * TOP_K + k]
            pltpu.make_async_copy(h_ref.at[pl.ds(r * PACK_T, PACK_T)], xs_hbm.at[_tile_rows(d, PACK_T)],
                                  rsem).start(priority=k % 2)
    n_rows = MOE_IDX * PACK_T
    pltpu.make_async_copy(xs_hbm.at[pl.ds(0, n_rows)], xs_hbm.at[pl.ds(0, n_rows)], rsem).wait()


def _dispatch(hp, dest2, pad_start, pad_cnt, n_used):
    grid_spec = pltpu.PrefetchScalarGridSpec(
        num_scalar_prefetch=3,
        grid=(MOE_NBLK,),
        in_specs=[pl.BlockSpec(memory_space=pl.ANY),
                  pl.BlockSpec((MOE_TB * PACK_T, LANES), lambda i, ps, pc, nu: (i, 0))],
        out_specs=pl.BlockSpec(memory_space=pl.ANY),
        scratch_shapes=[pltpu.SMEM((2, MOE_IDX), jnp.int32),
                        pltpu.VMEM((MOE_TM * PACK_T, LANES), jnp.uint32),
                        pltpu.SemaphoreType.DMA((2,)), pltpu.SemaphoreType.DMA(()),
                        pltpu.SemaphoreType.DMA(()), pltpu.SemaphoreType.DMA(())],
    )
    return pl.pallas_call(
        _dispatch_body, grid_spec=grid_spec,
        out_shape=jax.ShapeDtypeStruct((MOE_ROWS * PACK_T, LANES), jnp.uint32),
        compiler_params=_cparams(1), name="moe_dispatch",
    )(pad_start, pad_cnt, n_used, dest2, hp)


def _combine_body(dest_hbm, eo_hbm, h_ref, gate_ref, g_ref, b_ref, *rest, with_bf16):
    if with_bf16:
        o_ref, obf_ref, idx_smem, ebuf, xbuf, isem, rsem = rest
    else:
        o_ref, idx_smem, ebuf, xbuf, isem, rsem = rest
    i = pl.program_id(0)
    slot = i % 2

    def issue_rows(s):
        for r in range(MOE_TB):
            for k in range(TOP_K):
                d = idx_smem[s, r * TOP_K + k]
                pltpu.make_async_copy(eo_hbm.at[_tile_rows(d, ROW_T)], ebuf.at[s, k, pl.ds(r * ROW_T, ROW_T)],
                                      rsem.at[s]).start(priority=k % 2)

    @pl.when(i == 0)
    def _prime():
        first = _idx_copy(dest_hbm, idx_smem, isem, 0, 0)
        first.start()
        first.wait()
        issue_rows(0)
        _idx_copy(dest_hbm, idx_smem, isem, 1, 1).start()

    @pl.when(i + 1 < MOE_NBLK)
    def _prefetch():
        _idx_copy(dest_hbm, idx_smem, isem, i + 1, 1 - slot).wait()
        issue_rows(1 - slot)

        @pl.when(i + 2 < MOE_NBLK)
        def _next_idx():
            _idx_copy(dest_hbm, idx_smem, isem, i + 2, slot).start()

    pltpu.make_async_copy(ebuf.at[slot], ebuf.at[slot], rsem.at[slot]).wait()
    gates = gate_ref[...]
    for c in range(ROW_T):
        cs_ = slice(c * LANES, (c + 1) * LANES)
        f = ebuf[slot, 0, pl.ds(c, MOE_TB, stride=ROW_T), :] * gates[:, 0:1]
        for k in range(1, TOP_K):
            f = f + ebuf[slot, k, pl.ds(c, MOE_TB, stride=ROW_T), :] * gates[:, k:k + 1]
        xbuf[:, cs_] = DN_ALPHA * h_ref[:, cs_] + f
    y = _layer_norm_rows(xbuf[...], g_ref[...], b_ref[...])
    o_ref[...] = y
    if with_bf16:
        obf_ref[...] = y.astype(BF16)


def _ln_moe(h, eo, dest2, gates, ln_g, ln_b, layer, with_bf16):
    row = pl.BlockSpec((MOE_TB, D_MODEL), lambda i: (i, 0))
    par = pl.BlockSpec((None, None, 1, D_MODEL), lambda i: (layer, 1, 0, 0))
    out_shape = [jax.ShapeDtypeStruct((N_ROWS, D_MODEL), F32)]
    if with_bf16:
        out_shape.append(jax.ShapeDtypeStruct((N_ROWS, D_MODEL), BF16))
    outs = pl.pallas_call(
        functools.partial(_combine_body, with_bf16=with_bf16),
        grid=(MOE_NBLK,),
        in_specs=[pl.BlockSpec(memory_space=pl.ANY), pl.BlockSpec(memory_space=pl.ANY), row,
                  pl.BlockSpec((MOE_TB, LANES), lambda i: (i, 0)), par, par],
        out_specs=tuple([row] * len(out_shape)),
        out_shape=tuple(out_shape),
        scratch_shapes=[pltpu.SMEM((2, MOE_IDX), jnp.int32),
                        pltpu.VMEM((2, TOP_K, MOE_TB * ROW_T, LANES), F32),
                        pltpu.VMEM((MOE_TB, D_MODEL), F32),
                        pltpu.SemaphoreType.DMA((2,)), pltpu.SemaphoreType.DMA((2,))],
        compiler_params=_cparams(1), name="ln_moe_combine",
    )(dest2, eo, h, gates, ln_g.reshape(DEPTH, 2, 1, D_MODEL), ln_b.reshape(DEPTH, 2, 1, D_MODEL))
    return outs if with_bf16 else (outs[0], None)


def _moe_body(be_ref, par_ref, nxt_ref, nb_ref, x_ref, wgu_hbm, bgu_ref, wd_hbm, bd_ref, o_ref,
              wgu_f32, wd_f32, wgu_bf, wd_bf, xb_ref, wsem, *, layer):
    i = pl.program_id(0)
    e = be_ref[i]
    prev = be_ref[jnp.maximum(i - 1, 0)]
    slot = par_ref[i]
    used = i < nb_ref[0]

    def fetch(expert, s):
        return (pltpu.make_async_copy(wgu_hbm.at[layer, expert], wgu_f32.at[s], wsem.at[0, s]),
                pltpu.make_async_copy(wd_hbm.at[layer, expert], wd_f32.at[s], wsem.at[1, s]))

    @pl.when(i == 0)
    def _cold_start():
        for cp in fetch(e, 0):
            cp.start()

    @pl.when(jnp.logical_and(used, jnp.logical_or(i == 0, e != prev)))
    def _switch_expert():
        for cp in fetch(e, slot):
            cp.wait()
        nxt = nxt_ref[i]

        @pl.when(nxt >= 0)
        def _prefetch_next():
            for cp in fetch(nxt, 1 - slot):
                cp.start()

        wgu_bf[...] = wgu_f32[slot].astype(BF16)
        wd_bf[...] = wd_f32[slot].astype(BF16)

    @pl.when(used)
    def _compute():
        _unpack_rows(x_ref, xb_ref, MOE_TM)
        gu = _dot(xb_ref[...], wgu_bf[...]) + bgu_ref[...]
        gate = jnp.minimum(gu[:, :D_FF], SWIGLU_LIMIT)
        up = jnp.clip(gu[:, D_FF:], -SWIGLU_LIMIT, SWIGLU_LIMIT)
        act = (up + 1.0) * gate * jax.nn.sigmoid(SWIGLU_ALPHA * gate)
        out = _dot(act.astype(BF16), wd_bf[...]) + bd_ref[...]
        for c in range(ROW_T):
            o_ref[pl.ds(c, MOE_TM, stride=ROW_T), :] = out[:, c * LANES:(c + 1) * LANES]

    @pl.when(jnp.logical_not(used))
    def _unused_block():
        o_ref[...] = jnp.zeros(o_ref.shape, o_ref.dtype)


def _moe_experts(xs, blk_e, blk_par, blk_nxt, n_used, w_gu, b_gu, w_down, b_down, layer):
    grid_spec = pltpu.PrefetchScalarGridSpec(
        num_scalar_prefetch=4,
        grid=(MOE_BLOCKS,),
        in_specs=[
            pl.BlockSpec((MOE_TM * PACK_T, LANES), lambda i, be, pa, nx, nb: (jnp.minimum(i, nb[0] - 1), 0)),
            pl.BlockSpec(memory_space=pl.ANY),
            pl.BlockSpec((None, None, 1, 2 * D_FF), lambda i, be, pa, nx, nb: (layer, be[i], 0, 0)),
            pl.BlockSpec(memory_space=pl.ANY),
            pl.BlockSpec((None, None, 1, D_MODEL), lambda i, be, pa, nx, nb: (layer, be[i], 0, 0)),
        ],
        out_specs=pl.BlockSpec((MOE_TM * ROW_T, LANES), lambda i, be, pa, nx, nb: (i, 0)),
        scratch_shapes=[pltpu.VMEM((2, D_MODEL, 2 * D_FF), F32), pltpu.VMEM((2, D_FF, D_MODEL), F32),
                        pltpu.VMEM((D_MODEL, 2 * D_FF), BF16), pltpu.VMEM((D_FF, D_MODEL), BF16),
                        pltpu.VMEM((MOE_TM, D_MODEL), BF16),
                        pltpu.SemaphoreType.DMA((2, 2))],
    )
    return pl.pallas_call(
        functools.partial(_moe_body, layer=layer), grid_spec=grid_spec,
        out_shape=jax.ShapeDtypeStruct((MOE_ROWS * ROW_T, LANES), F32),
        compiler_params=_cparams(1), name="moe_experts",
    )(blk_e, blk_par, blk_nxt, n_used, xs, w_gu, b_gu.reshape(DEPTH, N_EXPERTS, 1, 2 * D_FF),
      w_down, b_down.reshape(DEPTH, N_EXPERTS, 1, D_MODEL))


def _moe_layer(h, hp, p, layer, with_bf16):
    idx, gates, rank, cnt = _route(h, p['moe_w_router'], p['moe_b_router'], layer)
    counts = cnt[0, :N_EXPERTS].astype(jnp.int32)
    psz = (counts + MOE_TM - 1) // MOE_TM * MOE_TM
    pend = jnp.cumsum(psz)
    pstart = pend - psz
    expert_ids = jnp.arange(N_EXPERTS, dtype=jnp.int32)
    top_i = idx[:, :TOP_K]
    base = jnp.sum(jnp.where(top_i[:, :, None] == expert_ids, pstart, 0), axis=-1)
    dest2 = (base + rank[:, :TOP_K]).astype(jnp.int32).reshape(MOE_NBLK, MOE_IDX)
    n_used = (pend[-1] // MOE_TM).astype(jnp.int32)
    blk = jnp.arange(MOE_BLOCKS, dtype=jnp.int32)
    blk_e = jnp.sum((blk[:, None] * MOE_TM >= pend[None, :]).astype(jnp.int32), axis=1)
    blk_e = jnp.minimum(blk_e, N_EXPERTS - 1)
    blk_e = jnp.where(blk < n_used, blk_e, blk_e[jnp.maximum(n_used - 1, 0)]).astype(jnp.int32)
    used = counts > 0
    ordinal = jnp.cumsum(used.astype(jnp.int32)) - 1
    cand = jnp.where(used, expert_ids, N_EXPERTS)
    later = jnp.concatenate([lax.cummin(cand[::-1])[::-1][1:], jnp.full((1,), N_EXPERTS, jnp.int32)])
    nxt_e = jnp.where(later >= N_EXPERTS, -1, later)
    blk_hot = blk_e[:, None] == expert_ids
    blk_par = jnp.sum(jnp.where(blk_hot, ordinal % 2, 0), axis=1).astype(jnp.int32)
    blk_nxt = jnp.sum(jnp.where(blk_hot, nxt_e, 0), axis=1).astype(jnp.int32)
    n_used = n_used.reshape(1)
    xs = _dispatch(hp, dest2, (pstart + counts).astype(jnp.int32), (psz - counts).astype(jnp.int32), n_used)
    eo = _moe_experts(xs, blk_e, blk_par, blk_nxt, n_used, p['moe_w_gu'], p['moe_b_gu'],
                      p['moe_w_down'], p['moe_b_down'], layer)
    return _ln_moe(h, eo, dest2, gates, p['ln_g'], p['ln_b'], layer, with_bf16)


def _ssd_body(z_ref, xbc_ref, dt_ref, h0_ref, c0_ref, cw_ref, cb_ref, dtb_ref, alog_ref, de_ref, nw_ref,
              eexp_ref, y_ref, hfin_ref, cfin_ref, ht_ref, ext_ref, act_ref, ybuf_ref, *, lv, nc):
    L = SSD_CHUNK
    N = M_DSTATE
    c = pl.program_id(1)
    n_pairs = M_NHEADS // 2
    x_cols = M_D_INNER
    b_col0 = M_D_INNER
    c_col0 = M_D_INNER + M_NGROUPS * M_DSTATE

    @pl.when(c == 0)
    def _init():
        for i in range(n_pairs):
            ht_ref[i] = h0_ref[2 * i:2 * i + 2].reshape(2 * M_HEADDIM, N).T
        ext_ref[0:8, :] = jnp.zeros((8, M_CONV_DIM), F32)
        ext_ref[5:8, :] = c0_ref[...]

    raw = xbc_ref[...]
    if lv < L:
        raw = jnp.concatenate([raw, jnp.zeros((L - lv, M_CONV_DIM), F32)], axis=0)
    ext_ref[8:8 + L, :] = raw
    cw_step = 256
    for j in range(M_CONV_DIM // cw_step):
        cs_ = slice(j * cw_step, (j + 1) * cw_step)
        acc = cb_ref[:, cs_] + cw_ref[0:1, cs_] * ext_ref[5:5 + L, cs_]
        acc = acc + cw_ref[1:2, cs_] * ext_ref[6:6 + L, cs_]
        acc = acc + cw_ref[2:3, cs_] * ext_ref[7:7 + L, cs_]
        acc = acc + cw_ref[3:4, cs_] * ext_ref[8:8 + L, cs_]
        act_ref[:, cs_] = jax.nn.silu(acc)
    carry = ext_ref[5 + lv:8 + lv, :]
    ext_ref[5:8, :] = carry
    cfin_ref[...] = carry

    dtr = dt_ref[...]
    if lv < L:
        dtr = jnp.concatenate([dtr, jnp.zeros((L - lv, LANES), F32)], axis=0)
    dt = jax.nn.softplus(dtr + dtb_ref[...])
    row = lax.broadcasted_iota(jnp.int32, (L, L), 0)
    col = lax.broadcasted_iota(jnp.int32, (L, L), 1)
    if lv < L:
        dt = jnp.where(row < lv, dt, 0.0)
    da = dt * (-jnp.exp(alog_ref[...]))
    causal = row >= col
    tri = jnp.where(causal, 1.0, 0.0).astype(BF16)
    d_hi, d_mid, d_lo = _split3(da)
    cs = _dot(tri, d_hi) + _dot(tri, d_mid) + _dot(tri, d_lo)
    cs_end = cs[L - 1:L, :]
    cst = cs.T
    dtt = dt.T
    wt = (dt * jnp.exp(cs_end - cs)).T
    e_hi, e_lo = _split2(jnp.broadcast_to(jnp.exp(cs_end), (8, LANES)))
    dend = _dot(e_hi, eexp_ref[...]) + _dot(e_lo, eexp_ref[...])
    lane = lax.broadcasted_iota(jnp.int32, (L, LANES), 1)
    first_head = lane < M_HEADDIM

    for g in range(M_NGROUPS):
        bg = act_ref[:, b_col0 + g * N:b_col0 + (g + 1) * N]
        cg = act_ref[:, c_col0 + g * N:c_col0 + (g + 1) * N]
        btg = bg.T
        cb = _dot(cg.astype(BF16), btg.astype(BF16))
        for pr in range(M_NHEADS // M_NGROUPS // 2):
            i = g * (M_NHEADS // M_NGROUPS // 2) + pr
            ps = slice(i * LANES, (i + 1) * LANES)
            xs_pair = act_ref[:, ps]
            ht_pair = ht_ref[i]
            lhs_rows = []
            bts_rows = []
            for j in range(2):
                hd = 2 * i + j
                colb = jnp.broadcast_to(cs[:, hd:hd + 1], (L, L))
                seg = colb - cst[hd:hd + 1, :]
                decay = jnp.exp(jnp.where(causal, seg, -jnp.inf))
                m_h = cb * decay * dtt[hd:hd + 1, :]
                c_h = cg * jnp.exp(colb)
                lhs_rows.append(jnp.concatenate([m_h, c_h], axis=1))
                bts_rows.append(btg * wt[hd:hd + 1, :])
            lhs = jnp.concatenate(lhs_rows, axis=0).astype(BF16)
            rhs = jnp.concatenate([xs_pair, ht_pair], axis=0).astype(BF16)
            out = _dot(lhs, rhs)
            y_pair = jnp.where(first_head, out[:L], out[L:]) + xs_pair * de_ref[:, ps]
            ybuf_ref[:, ps] = y_pair
            bts = jnp.concatenate(bts_rows, axis=0).astype(BF16)
            upd = _dot(bts, xs_pair.astype(BF16))
            ht_ref[i] = dend[0:1, ps] * ht_pair + jnp.where(first_head, upd[:N], upd[N:])

    gw = M_D_INNER // M_NGROUPS
    for g in range(M_NGROUPS):
        gs = slice(g * gw, (g + 1) * gw)
        yg = ybuf_ref[0:lv, gs] * jax.nn.silu(z_ref[:, gs])
        ms = jnp.mean(yg * yg, axis=-1, keepdims=True)
        y_ref[:, gs] = (yg * lax.rsqrt(ms + NORM_EPS) * nw_ref[:, gs]).astype(y_ref.dtype)

    @pl.when(c == nc - 1)
    def _final():
        for i in range(n_pairs):
            hfin_ref[2 * i:2 * i + 2] = ht_ref[i].T.reshape(2, M_HEADDIM, N)


def _ssd(z, xbc, dtr, h0, h0_layer, c0, c0_layer, prm, *, row0, lv, nb, nc, out_dtype):
    rb = row0 // lv

    def rows(width):
        return pl.BlockSpec((lv, width), lambda b, c: (rb + b * nc + c, 0))

    def par(shape):
        return pl.BlockSpec(shape, lambda b, c: (0, 0))

    if h0_layer is None:
        h0_spec = pl.BlockSpec((None, M_NHEADS, M_HEADDIM, M_DSTATE), lambda b, c: (b, 0, 0, 0))
        c0_spec = pl.BlockSpec((None, M_CONV - 1, M_CONV_DIM), lambda b, c: (b, 0, 0))
    else:
        h0_spec = pl.BlockSpec((None, None, M_NHEADS, M_HEADDIM, M_DSTATE),
                               lambda b, c: (h0_layer, b, 0, 0, 0))
        c0_spec = pl.BlockSpec((None, None, M_CONV - 1, M_CONV_DIM), lambda b, c: (c0_layer, b, 0, 0))
    out_rows = nb * nc * lv
    return pl.pallas_call(
        functools.partial(_ssd_body, lv=lv, nc=nc),
        grid=(nb, nc),
        in_specs=[rows(M_D_INNER), rows(M_CONV_DIM), rows(LANES), h0_spec, c0_spec,
                  par((M_CONV, M_CONV_DIM)), par((1, M_CONV_DIM)), par((1, LANES)), par((1, LANES)),
                  par((1, M_D_INNER)), par((1, M_D_INNER)), par((LANES, M_D_INNER))],
        out_specs=(pl.BlockSpec((lv, M_D_INNER), lambda b, c: (b * nc + c, 0)),
                   pl.BlockSpec((None, M_NHEADS, M_HEADDIM, M_DSTATE), lambda b, c: (b, 0, 0, 0)),
                   pl.BlockSpec((None, M_CONV - 1, M_CONV_DIM), lambda b, c: (b, 0, 0))),
        out_shape=(jax.ShapeDtypeStruct((out_rows, M_D_INNER), out_dtype),
                   jax.ShapeDtypeStruct((nb, M_NHEADS, M_HEADDIM, M_DSTATE), F32),
                   jax.ShapeDtypeStruct((nb, M_CONV - 1, M_CONV_DIM), F32)),
        scratch_shapes=[pltpu.VMEM((M_NHEADS // 2, M_DSTATE, LANES), F32),
                        pltpu.VMEM((8 + SSD_CHUNK, M_CONV_DIM), F32),
                        pltpu.VMEM((SSD_CHUNK, M_CONV_DIM), F32),
                        pltpu.VMEM((SSD_CHUNK, M_D_INNER), F32)],
        compiler_params=_cparams(2), name="ssd_mixer",
    )(z, xbc, dtr, h0, c0, *prm)


def _mamba_layer(h, hb, state_ssm, state_conv, p, layer):
    w_in = p['m_w_in']
    z = _matmul(hb, w_in, layer=layer, col0=0, n_out=M_D_INNER, tn=1024)
    xbc = _matmul(hb, w_in, layer=layer, col0=M_D_INNER, n_out=M_CONV_DIM, tn=1024)
    w_dt = jnp.pad(w_in[layer, :, M_D_INNER + M_CONV_DIM:], ((0, 0), (0, LANES - M_NHEADS)))
    dtr = _matmul(hb, w_dt, n_out=LANES, tn=LANES)

    def pad_heads(v):
        return jnp.pad(v[layer].astype(F32), (0, LANES - M_NHEADS)).reshape(1, LANES)

    head_of_col = jnp.arange(M_D_INNER, dtype=jnp.int32) // M_HEADDIM
    expand = (jnp.arange(LANES, dtype=jnp.int32)[:, None] == head_of_col[None, :]).astype(BF16)
    prm = (p['m_conv_w'][layer], p['m_conv_b'][layer].reshape(1, M_CONV_DIM),
           pad_heads(p['m_dt_bias']), pad_heads(p['m_a_log']),
           jnp.repeat(p['m_d'][layer].astype(F32), M_HEADDIM).reshape(1, M_D_INNER),
           p['m_norm_w'][layer].reshape(1, M_D_INNER), expand)
    zeros_h = jnp.zeros((BATCH, M_NHEADS, M_HEADDIM, M_DSTATE), F32)
    zeros_c = jnp.zeros((BATCH, M_CONV - 1, M_CONV_DIM), F32)
    y_m, h_m, c_m = _ssd(z, xbc, dtr, zeros_h, None, zeros_c, None, prm,
                         row0=ROW_META, lv=N_META, nb=BATCH, nc=1, out_dtype=BF16)
    y_x, h_x, c_x = _ssd(z, xbc, dtr, h_m, None, c_m, None, prm,
                         row0=0, lv=SSD_CHUNK, nb=BATCH, nc=SEQ // SSD_CHUNK, out_dtype=BF16)
    y_s, h_s, c_s = _ssd(z, xbc, dtr, state_ssm, layer, state_conv, layer, prm,
                         row0=ROW_SAMPLE, lv=DEC_SEQ, nb=DEC_BATCH, nc=1, out_dtype=F32)
    yn = jnp.concatenate([y_x, y_m, y_s.astype(BF16)], axis=0)
    mix = _matmul(yn, p['m_w_out'], layer=layer, n_out=D_MODEL, tn=512)
    h, hp = _ln_residual(h, mix, p['ln_g'], p['ln_b'], layer)
    return h, hp, (h_x, c_x, h_s, c_s)


def _lambda_value(lam_ref, lam_init):
    lp = lam_ref[...]
    s1 = jnp.sum(lp[0:1, :] * lp[1:2, :], axis=-1, keepdims=True)
    s2 = jnp.sum(lp[2:3, :] * lp[3:4, :], axis=-1, keepdims=True)
    return jnp.exp(s1) - jnp.exp(s2) + lam_init


def _subln(o, sw, lam_init):
    ms = jnp.mean(o * o, axis=-1, keepdims=True)
    return o * lax.rsqrt(ms + NORM_EPS) * sw * (1.0 - lam_init)


def _rep(x, n):
    return x if n == 1 else jnp.concatenate([x] * n, axis=1)


def _flash_body(q_ref, k_ref, v_ref, km_ref, vm_ref, lam_ref, sw_ref, o_ref,
                m_ref, l_ref, acc_ref, kb_ref, vb_ref, kpad_ref, vpad_ref, *, lam_init):
    t = FLASH_T
    dh = A_HEAD_DIM
    qi = pl.program_id(2)

    @pl.when(qi == 0)
    def _load_kv():
        kb_ref[...] = k_ref[...].astype(BF16)
        vb_ref[...] = v_ref[...].astype(BF16)
        kpad_ref[...] = jnp.zeros(kpad_ref.shape, BF16)
        vpad_ref[...] = jnp.zeros(vpad_ref.shape, BF16)
        kpad_ref[0:N_META, :] = km_ref[...].astype(BF16)
        vpad_ref[0:N_META, :] = vm_ref[...].astype(BF16)

    m_ref[...] = jnp.full(m_ref.shape, -jnp.inf, F32)
    l_ref[...] = jnp.zeros(l_ref.shape, F32)
    acc_ref[...] = jnp.zeros(acc_ref.shape, F32)
    qs = [q_ref[:, j * dh:(j + 1) * dh].astype(BF16) for j in range(2)]

    def process(kb, vb, mask):
        for j in range(2):
            s = _dot_nt(qs[j], kb[:, j * dh:(j + 1) * dh])
            if mask is not None:
                s = jnp.where(mask, s, -jnp.inf)
            m_prev = m_ref[j]
            m_new = jnp.maximum(m_prev, jnp.max(s, axis=1, keepdims=True))
            alpha = jnp.exp(m_prev - m_new)
            pmat = jnp.exp(s - _rep(m_new, s.shape[1] // LANES))
            l_ref[j] = alpha * l_ref[j] + jnp.sum(pmat, axis=1, keepdims=True)
            acc_ref[j] = _rep(alpha, 2) * acc_ref[j] + _dot(pmat.astype(BF16), vb)
            m_ref[j] = m_new

    colm = lax.broadcasted_iota(jnp.int32, (t, LANES), 1)
    process(kpad_ref[...], vpad_ref[...], colm < N_META)

    def full_block(kj, carry):
        off = pl.multiple_of(kj * t, t)
        process(kb_ref[pl.ds(off, t), :], vb_ref[pl.ds(off, t), :], None)
        return carry

    lax.fori_loop(0, qi, full_block, 0)

    off = pl.multiple_of(qi * t, t)
    r = lax.broadcasted_iota(jnp.int32, (t, t), 0)
    cc = lax.broadcasted_iota(jnp.int32, (t, t), 1)
    process(kb_ref[pl.ds(off, t), :], vb_ref[pl.ds(off, t), :], cc <= r)
    lam = _lambda_value(lam_ref, lam_init)
    o1 = acc_ref[0] / _rep(l_ref[0], 2)
    o2 = acc_ref[1] / _rep(l_ref[1], 2)
    o_ref[...] = _subln(o1 - lam * o2, sw_ref[...], lam_init).astype(o_ref.dtype)


def _flash_prompt(q, k, v, lam_p, sw, j_layer, lam_init):
    t = FLASH_T
    nq = SEQ // t
    w2 = 2 * A_HEAD_DIM
    qspec = pl.BlockSpec((t, w2), lambda b, p, qi: (b * nq + qi, p))
    kspec = pl.BlockSpec((SEQ, w2), lambda b, p, qi: (b, p))
    mspec = pl.BlockSpec((N_META, w2), lambda b, p, qi: (ROW_META // N_META + b, p))
    return pl.pallas_call(
        functools.partial(_flash_body, lam_init=lam_init),
        grid=(BATCH, A_HEADS, nq),
        in_specs=[qspec, kspec, kspec, mspec, mspec,
                  pl.BlockSpec((None, 4, A_HEAD_DIM), lambda b, p, qi: (j_layer, 0, 0)),
                  pl.BlockSpec((None, 1, w2), lambda b, p, qi: (j_layer, 0, 0))],
        out_specs=pl.BlockSpec((t, w2), lambda b, p, qi: (b * nq + qi, p)),
        out_shape=jax.ShapeDtypeStruct((ROWS_X, D_MODEL), BF16),
        scratch_shapes=[pltpu.VMEM((2, t, LANES), F32), pltpu.VMEM((2, t, LANES), F32),
                        pltpu.VMEM((2, t, w2), F32),
                        pltpu.VMEM((SEQ, w2), BF16), pltpu.VMEM((SEQ, w2), BF16),
                        pltpu.VMEM((LANES, w2), BF16), pltpu.VMEM((LANES, w2), BF16)],
        compiler_params=_cparams(3), name="flash_prompt",
    )(q, k, v, k, v, lam_p, sw)


SATTN_PG = 4


SATTN_SLOTS = 3


def _sattn_body(pt_ref, q_ref, ck_hbm, cv_hbm, kn_ref, vn_ref, lam_ref, sw_ref, o_ref,
                qf_ref, qb_ref, m_ref, l_ref, acc_ref, kb_ref, vb_ref, kring, vring0, vring1, ring_sem,
                *, lq, nb, n_steps, pg, lam_init):
    dh = A_HEAD_DIM
    n_qk = 2 * A_HEADS
    rows = n_qk * lq
    j = pl.program_id(1)
    k_rows = PAGE_SIZE * n_qk
    v_rows = PAGE_SIZE * A_HEADS
    total = nb * n_steps

    def group_copies(t, slot):
        seq = t // n_steps
        first = (t % n_steps) * pg
        copies = []
        for g in range(pg):
            page = pt_ref[seq, first + g]
            copies.append(pltpu.make_async_copy(
                ck_hbm.at[page], kring.at[slot, pl.ds(g * k_rows, k_rows)], ring_sem.at[0, slot]))
            copies.append(pltpu.make_async_copy(
                cv_hbm.at[page, :, pl.ds(0, LANES)], vring0.at[slot, pl.ds(g * v_rows, v_rows)],
                ring_sem.at[1, slot]))
            copies.append(pltpu.make_async_copy(
                cv_hbm.at[page, :, pl.ds(LANES, LANES)], vring1.at[slot, pl.ds(g * v_rows, v_rows)],
                ring_sem.at[2, slot]))
        return copies

    def start_group(t):
        for cp in group_copies(t, t % SATTN_SLOTS):
            cp.start()

    @pl.when(j == 0)
    def _init():
        qf_ref[...] = jnp.zeros(qf_ref.shape, F32)
        for hd in range(n_qk):
            qf_ref[hd * lq:(hd + 1) * lq, hd * dh:(hd + 1) * dh] = q_ref[:, hd * dh:(hd + 1) * dh]
        qb_ref[...] = qf_ref[...].astype(BF16)
        m_ref[...] = jnp.full(m_ref.shape, -jnp.inf, F32)
        l_ref[...] = jnp.zeros(l_ref.shape, F32)
        acc_ref[...] = jnp.zeros(acc_ref.shape, F32)

    def process(tk, mask):
        s = _dot_nt(qb_ref[...], kb_ref[0:tk, :])
        if mask is not None:
            s = jnp.where(mask, s, -jnp.inf)
        m_prev = m_ref[...]
        m_new = jnp.maximum(m_prev, jnp.max(s, axis=1, keepdims=True))
        alpha = jnp.exp(m_prev - m_new)
        pmat = jnp.exp(s - _rep(m_new, tk // LANES))
        l_ref[...] = alpha * l_ref[...] + jnp.sum(pmat, axis=1, keepdims=True)
        m_ref[...] = m_new
        pb = pmat.astype(BF16)
        alpha2 = _rep(alpha, 2)
        for pp in range(A_HEADS):
            rs = slice(2 * pp * lq, (2 * pp + 2) * lq)
            acc_ref[rs, :] = alpha2[rs] * acc_ref[rs, :] + _dot(pb[rs], vb_ref[0:tk, pp * 2 * dh:(pp + 1) * 2 * dh])

    @pl.when(j < n_steps)
    def _past_pages():
        if n_steps == 0:
            return
        t = pl.program_id(0) * n_steps + j
        slot = t % SATTN_SLOTS

        @pl.when(t == 0)
        def _prime():
            start_group(0)
            if total > 1:
                start_group(1)

        @pl.when(t + 2 < total)
        def _prefetch():
            start_group(t + 2)

        for cp in group_copies(t, slot):
            cp.wait()
        vrings = (vring0, vring1)
        for g in range(pg):
            ts = slice(g * PAGE_SIZE, (g + 1) * PAGE_SIZE)
            for hd in range(n_qk):
                kb_ref[ts, hd * dh:(hd + 1) * dh] = (
                    kring[slot, pl.ds(g * k_rows + hd, PAGE_SIZE, stride=n_qk), :].astype(BF16))
            for hv in range(A_HEADS):
                for half in range(2):
                    c0 = hv * 2 * dh + half * LANES
                    vb_ref[ts, c0:c0 + LANES] = (
                        vrings[half][slot, pl.ds(g * v_rows + hv, PAGE_SIZE, stride=A_HEADS), :].astype(BF16))
        process(pg * PAGE_SIZE, None)

    @pl.when(j == n_steps)
    def _new_keys():
        pad = jnp.zeros((PAGE_SIZE - lq, D_MODEL), F32)
        kb_ref[0:PAGE_SIZE, :] = jnp.concatenate([kn_ref[...], pad], axis=0).astype(BF16)
        vb_ref[0:PAGE_SIZE, :] = jnp.concatenate([vn_ref[...], pad], axis=0).astype(BF16)
        r = lax.broadcasted_iota(jnp.int32, (rows, LANES), 0)
        cc = lax.broadcasted_iota(jnp.int32, (rows, LANES), 1)
        process(PAGE_SIZE, cc <= jnp.bitwise_and(r, lq - 1))
        lam = _lambda_value(lam_ref, lam_init)
        for pp in range(A_HEADS):
            r1 = slice(2 * pp * lq, (2 * pp + 1) * lq)
            r2 = slice((2 * pp + 1) * lq, (2 * pp + 2) * lq)
            o1 = acc_ref[r1, :] / _rep(l_ref[r1, :], 2)
            o2 = acc_ref[r2, :] / _rep(l_ref[r2, :], 2)
            o_ref[:, pp * 2 * dh:(pp + 1) * 2 * dh] = _subln(o1 - lam * o2, sw_ref[...], lam_init).astype(o_ref.dtype)


def _small_attn(q, k, v, cache_k, cache_v, page_table, lam_p, sw, j_layer, lam_init,
                *, row0, lq, nb, n_pages, out_dtype):
    rb = row0 // lq
    pg = SATTN_PG if n_pages > 0 else 1
    assert n_pages % pg == 0
    n_steps = n_pages // pg
    rows = 2 * A_HEADS * lq
    qspec = pl.BlockSpec((lq, D_MODEL), lambda b, j, pt: (rb + b, 0))

    n_pool = cache_k.shape[0]
    k_rows = PAGE_SIZE * 2 * A_HEADS
    v_rows = PAGE_SIZE * A_HEADS
    cache_k2 = cache_k.reshape(n_pool, k_rows, A_HEAD_DIM)
    cache_v2 = cache_v.reshape(n_pool, v_rows, 2 * A_HEAD_DIM)
    grid_spec = pltpu.PrefetchScalarGridSpec(
        num_scalar_prefetch=1,
        grid=(nb, n_steps + 1),
        in_specs=[qspec, pl.BlockSpec(memory_space=pl.ANY), pl.BlockSpec(memory_space=pl.ANY),
                  qspec, qspec,
                  pl.BlockSpec((None, 4, A_HEAD_DIM), lambda b, j, pt: (j_layer, 0, 0)),
                  pl.BlockSpec((None, 1, 2 * A_HEAD_DIM), lambda b, j, pt: (j_layer, 0, 0))],
        out_specs=pl.BlockSpec((lq, D_MODEL), lambda b, j, pt: (b, 0)),
        scratch_shapes=[pltpu.VMEM((rows, D_MODEL), F32), pltpu.VMEM((rows, D_MODEL), BF16),
                        pltpu.VMEM((rows, LANES), F32), pltpu.VMEM((rows, LANES), F32),
                        pltpu.VMEM((rows, 2 * A_HEAD_DIM), F32),
                        pltpu.VMEM((pg * PAGE_SIZE, D_MODEL), BF16),
                        pltpu.VMEM((pg * PAGE_SIZE, D_MODEL), BF16),
                        pltpu.VMEM((SATTN_SLOTS, pg * k_rows, LANES), F32),
                        pltpu.VMEM((SATTN_SLOTS, pg * v_rows, LANES), F32),
                        pltpu.VMEM((SATTN_SLOTS, pg * v_rows, LANES), F32),
                        pltpu.SemaphoreType.DMA((3, SATTN_SLOTS))],
    )
    return pl.pallas_call(
        functools.partial(_sattn_body, lq=lq, nb=nb, n_steps=n_steps, pg=pg, lam_init=lam_init),
        grid_spec=grid_spec,
        out_shape=jax.ShapeDtypeStruct((nb * lq, D_MODEL), out_dtype),
        compiler_params=_cparams(2), name="small_attn",
    )(page_table, q, cache_k2, cache_v2, k, v, lam_p, sw)


def _attn_layer(h, hb, k, v, rope_tabs, cache_k, cache_v, page_table, p, layer):
    j_layer = layer - N_A_LAYERS
    lam_init = 0.8 - 0.6 * math.exp(-0.3 * layer)
    q = _matmul(hb, p['a_w_q'], layer=j_layer, n_out=D_MODEL, tn=1024, rope=rope_tabs,
                scale=A_HEAD_DIM ** -0.5)
    lam_p = p['a_lambda'].astype(F32)
    sw = p['a_subln_w'].astype(F32).reshape(-1, 1, 2 * A_HEAD_DIM)
    o_x = _flash_prompt(q, k, v, lam_p, sw, j_layer, lam_init)
    dummy_pt = jnp.zeros((BATCH, 1), jnp.int32)
    o_m = _small_attn(q, k, v, cache_k, cache_v, dummy_pt, lam_p, sw, j_layer, lam_init,
                      row0=ROW_META, lq=N_META, nb=BATCH, n_pages=0, out_dtype=BF16)
    o_s = _small_attn(q, k, v, cache_k, cache_v, page_table, lam_p, sw, j_layer, lam_init,
                      row0=ROW_SAMPLE, lq=DEC_SEQ, nb=DEC_BATCH, n_pages=page_table.shape[1],
                      out_dtype=F32)
    o = jnp.concatenate([o_x, o_m, o_s.astype(BF16)], axis=0)
    mix = _matmul(o, p['a_w_o'], layer=j_layer, n_out=D_MODEL, tn=1024)
    return _ln_residual(h, mix, p['ln_g'], p['ln_b'], layer)


def _rope_tables():
    pos = jnp.concatenate([
        jnp.tile(N_META + jnp.arange(SEQ, dtype=jnp.int32), BATCH),
        jnp.tile(jnp.arange(N_META, dtype=jnp.int32), BATCH),
        jnp.tile(PAST_LEN + jnp.arange(DEC_SEQ, dtype=jnp.int32), DEC_BATCH)])
    half = A_HEAD_DIM // 2
    inv = ROPE_THETA ** (-jnp.arange(half, dtype=F32) / half)
    ang = pos.astype(F32)[:, None] * inv[None, :]
    cos = jnp.cos(ang)
    sin = jnp.sin(ang)
    return jnp.concatenate([cos, cos], axis=1), jnp.concatenate([-sin, sin], axis=1)


def kernel(x_prompt, x_sample, cache_k, cache_v, page_table, state_ssm, state_conv, meta_tokens, m_w_in, m_conv_w, m_conv_b, m_dt_bias, m_a_log, m_d, m_norm_w, m_w_out, a_w_q, a_w_k, a_w_v, a_lambda, a_subln_w, a_w_o, moe_w_router, moe_b_router, moe_w_gu, moe_b_gu, moe_w_down, moe_b_down, ln_g, ln_b):
    p = dict(m_w_in=m_w_in, m_conv_w=m_conv_w, m_conv_b=m_conv_b, m_dt_bias=m_dt_bias, m_a_log=m_a_log,
             m_d=m_d, m_norm_w=m_norm_w, m_w_out=m_w_out, a_w_q=a_w_q, a_w_k=a_w_k, a_w_v=a_w_v,
             a_lambda=a_lambda, a_subln_w=a_subln_w, a_w_o=a_w_o, moe_w_router=moe_w_router,
             moe_b_router=moe_b_router, moe_w_gu=moe_w_gu, moe_b_gu=moe_b_gu, moe_w_down=moe_w_down,
             moe_b_down=moe_b_down, ln_g=ln_g, ln_b=ln_b)
    assert page_table.shape[1] * PAGE_SIZE == PAST_LEN
    meta = jnp.broadcast_to(meta_tokens[None].astype(F32), (BATCH, N_META, D_MODEL))
    h = jnp.concatenate([x_prompt.reshape(ROWS_X, D_MODEL), meta.reshape(ROWS_META, D_MODEL),
                         x_sample.reshape(ROWS_SAMPLE, D_MODEL)], axis=0)
    hb = h.astype(BF16)
    rope_tabs = _rope_tables()

    ssm_p, conv_p, ssm_s, conv_s = [], [], [], []
    k = v = None
    for layer in range(DEPTH):
        if layer < N_A_LAYERS:
            h, hp, (h_x, c_x, h_s, c_s) = _mamba_layer(h, hb, state_ssm, state_conv, p, layer)
            ssm_p.append(h_x)
            conv_p.append(c_x)
            ssm_s.append(h_s)
            conv_s.append(c_s)
        else:
            if layer == N_A_LAYERS:
                k = _matmul(hb, a_w_k, n_out=D_MODEL, tn=1024, rope=rope_tabs)
                v = _matmul(hb, a_w_v, n_out=D_MODEL, tn=1024)
            h, hp = _attn_layer(h, hb, k, v, rope_tabs, cache_k, cache_v, page_table, p, layer)
        h, hb = _moe_layer(h, hp, p, layer, with_bf16=layer + 1 < DEPTH)

    def prompt_rows(a):
        return jnp.concatenate([a[ROW_META:ROW_SAMPLE].reshape(BATCH, N_META, D_MODEL),
                                a[:ROWS_X].reshape(BATCH, SEQ, D_MODEL)], axis=1)

    y_prompt = h[:ROWS_X].reshape(BATCH, SEQ, D_MODEL)
    y_sample = h[ROW_SAMPLE:].reshape(DEC_BATCH, DEC_SEQ, D_MODEL)
    k_prompt = prompt_rows(k).reshape(BATCH, SEQ + N_META, 2 * A_HEADS, A_HEAD_DIM)
    v_prompt = prompt_rows(v).reshape(BATCH, SEQ + N_META, A_HEADS, 2 * A_HEAD_DIM)
    k_sample = k[ROW_SAMPLE:].reshape(DEC_BATCH, DEC_SEQ, 2 * A_HEADS, A_HEAD_DIM)
    v_sample = v[ROW_SAMPLE:].reshape(DEC_BATCH, DEC_SEQ, A_HEADS, 2 * A_HEAD_DIM)
    return (y_prompt, y_sample, k_prompt, v_prompt, jnp.stack(ssm_p), jnp.stack(conv_p),
            k_sample, v_sample, jnp.stack(ssm_s), jnp.stack(conv_s))
```

```python
import functools
import math

import jax
import jax.numpy as jnp
from jax import lax
from jax.experimental import pallas as pl
from jax.experimental.pallas import tpu as pltpu

F32 = jnp.float32
BF16 = jnp.bfloat16

D_MODEL = 2048
BATCH = 4
SEQ = 2048
DEPTH = 4
DEC_BATCH = 8
DEC_SEQ = 8
PAGE_SIZE = 128
PAST_LEN = 16384
N_A_LAYERS = DEPTH // 2
N_META = 16
M_D_INNER = 2 * D_MODEL
M_HEADDIM = 64
M_NHEADS = M_D_INNER // M_HEADDIM
M_NGROUPS = 8
M_DSTATE = 128
M_CONV = 4
M_CONV_DIM = M_D_INNER + 2 * M_NGROUPS * M_DSTATE
A_HEADS = 8
A_HEAD_DIM = D_MODEL // (2 * A_HEADS)
ROPE_THETA = 10000.0
N_EXPERTS = 32
TOP_K = 4
D_FF = D_MODEL // 4
SWIGLU_LIMIT = 7.0
SWIGLU_ALPHA = 1.702
DN_ALPHA = (2 * DEPTH) ** 0.25
NORM_EPS = 1e-5

ROWS_X = BATCH * SEQ
ROW_META = ROWS_X
ROWS_META = BATCH * N_META
ROW_SAMPLE = ROW_META + ROWS_META
ROWS_SAMPLE = DEC_BATCH * DEC_SEQ
N_ROWS = ROW_SAMPLE + ROWS_SAMPLE

LANES = 128
VMEM_LIMIT = 56 * 1024 * 1024

SSD_CHUNK = 128
MOE_TM = 256
MOE_BLOCKS = N_ROWS * TOP_K // MOE_TM + N_EXPERTS
MOE_ROWS = MOE_BLOCKS * MOE_TM
MM_TM = 640
LN_TR = 320
FLASH_T = 512


def _cparams(n_axes):
    return pltpu.CompilerParams(dimension_semantics=("arbitrary",) * n_axes,
                                vmem_limit_bytes=VMEM_LIMIT)


def _dot(a, b):
    return jnp.dot(a, b, preferred_element_type=F32)


def _dot_nt(a, b):
    return lax.dot_general(a, b, (((1,), (1,)), ((), ())), preferred_element_type=F32)


def _split2(x):
    hi = x.astype(BF16)
    lo = (x - hi.astype(F32)).astype(BF16)
    return hi, lo


def _split3(x):
    hi = x.astype(BF16)
    r = x - hi.astype(F32)
    mid = r.astype(BF16)
    lo = (r - mid.astype(F32)).astype(BF16)
    return hi, mid, lo


def _mm_body(*refs, rope, scale):
    if rope:
        x_ref, w_ref, cos_ref, sin_ref, o_ref, wbf_ref = refs
    else:
        x_ref, w_ref, o_ref, wbf_ref = refs

    @pl.when(pl.program_id(1) == 0)
    def _cast_weight():
        wbf_ref[...] = w_ref[...].astype(BF16)

    acc = _dot(x_ref[...].astype(BF16), wbf_ref[...])
    if rope:
        cos = cos_ref[...]
        sin = sin_ref[...]
        for j in range(acc.shape[1] // LANES):
            blk = acc[:, j * LANES:(j + 1) * LANES]
            out = (blk * cos + pltpu.roll(blk, LANES // 2, 1) * sin) * scale
            o_ref[:, j * LANES:(j + 1) * LANES] = out.astype(o_ref.dtype)
    else:
        o_ref[...] = acc.astype(o_ref.dtype)


def _matmul(x, w, *, layer=None, col0=0, n_out, tn, out_dtype=F32, rope=None, scale=1.0):
    m, k = x.shape
    tm = MM_TM
    assert m % tm == 0 and n_out % tn == 0 and col0 % tn == 0
    jb = col0 // tn
    if layer is None:
        w_spec = pl.BlockSpec((k, tn), lambda j, i: (0, j + jb))
    else:
        w_spec = pl.BlockSpec((None, k, tn), lambda j, i: (layer, 0, j + jb))
    in_specs = [pl.BlockSpec((tm, k), lambda j, i: (i, 0)), w_spec]
    args = [x, w]
    if rope is not None:
        in_specs += [pl.BlockSpec((tm, LANES), lambda j, i: (i, 0))] * 2
        args += list(rope)
    return pl.pallas_call(
        functools.partial(_mm_body, rope=rope is not None, scale=scale),
        grid=(n_out // tn, m // tm),
        in_specs=in_specs,
        out_specs=pl.BlockSpec((tm, tn), lambda j, i: (i, j)),
        out_shape=jax.ShapeDtypeStruct((m, n_out), out_dtype),
        scratch_shapes=[pltpu.VMEM((k, tn), BF16)],
        compiler_params=_cparams(2),
        name="dense_matmul",
    )(*args)


ROUTE_SUB = 128


def _lane_pack(cols, lane):
    out = jnp.zeros(lane.shape, cols[0].dtype)
    for k, c in enumerate(cols):
        out = jnp.where(lane == k, c, out)
    return out


def _route_body(x_ref, w_ref, b_ref, idx_ref, gate_ref, rank_ref, cnt_ref, run_ref):
    tm = x_ref.shape[0]

    @pl.when(pl.program_id(0) == 0)
    def _init():
        run_ref[...] = jnp.zeros(run_ref.shape, F32)

    xh, xl = _split2(x_ref[...])
    wh, wl = _split2(w_ref[...])
    logits = _dot(xh, wh) + _dot(xh, wl) + _dot(xl, wh) + b_ref[...]
    lane = lax.broadcasted_iota(jnp.int32, (ROUTE_SUB, LANES), 1)
    r_i = lax.broadcasted_iota(jnp.int32, (ROUTE_SUB, ROUTE_SUB), 0)
    c_i = lax.broadcasted_iota(jnp.int32, (ROUTE_SUB, ROUTE_SUB), 1)
    below = jnp.where(r_i > c_i, 1.0, 0.0).astype(BF16)
    running = run_ref[0:1, :]
    for sb in range(tm // ROUTE_SUB):
        rs = slice(sb * ROUTE_SUB, (sb + 1) * ROUTE_SUB)
        rest = logits[rs]
        vals, idxs, hots = [], [], []
        for _ in range(TOP_K):
            mx = jnp.max(rest, axis=1, keepdims=True)
            idx = jnp.min(jnp.where(rest == mx, lane, LANES), axis=1, keepdims=True)
            hot = lane == idx
            rest = jnp.where(hot, -jnp.inf, rest)
            vals.append(mx)
            idxs.append(idx)
            hots.append(hot)
        exps = [jnp.exp(v - vals[0]) for v in vals]
        denom = exps[0] + exps[1] + exps[2] + exps[3]
        idx_ref[rs, :] = _lane_pack(idxs, lane)
        gate_ref[rs, :] = _lane_pack([e / denom for e in exps], lane)
        sel = jnp.zeros((ROUTE_SUB, LANES), F32)
        for hot in hots:
            sel = sel + jnp.where(hot, 1.0, 0.0)
        before = _dot(below, sel.astype(BF16)) + running
        ranks = [jnp.sum(jnp.where(hot, before, 0.0), axis=1, keepdims=True) for hot in hots]
        rank_ref[rs, :] = _lane_pack(ranks, lane).astype(jnp.int32)
        running = running + jnp.sum(sel, axis=0, keepdims=True)
    run_ref[...] = jnp.broadcast_to(running, run_ref.shape)
    cnt_ref[...] = jnp.broadcast_to(running, cnt_ref.shape)


def _route(h, w_router, b_router, layer):
    tm = MM_TM
    w_pad = jnp.pad(w_router, ((0, 0), (0, 0), (0, LANES - N_EXPERTS)))
    b_pad = jnp.pad(b_router.astype(F32), ((0, 0), (0, LANES - N_EXPERTS)),
                    constant_values=-1e30).reshape(DEPTH, 1, LANES)
    row = pl.BlockSpec((tm, LANES), lambda i: (i, 0))
    return pl.pallas_call(
        _route_body,
        grid=(N_ROWS // tm,),
        in_specs=[pl.BlockSpec((tm, D_MODEL), lambda i: (i, 0)),
                  pl.BlockSpec((None, D_MODEL, LANES), lambda i: (layer, 0, 0)),
                  pl.BlockSpec((None, 1, LANES), lambda i: (layer, 0, 0))],
        out_specs=(row, row, row, pl.BlockSpec((8, LANES), lambda i: (0, 0))),
        out_shape=(jax.ShapeDtypeStruct((N_ROWS, LANES), jnp.int32),
                   jax.ShapeDtypeStruct((N_ROWS, LANES), F32),
                   jax.ShapeDtypeStruct((N_ROWS, LANES), jnp.int32),
                   jax.ShapeDtypeStruct((8, LANES), F32)),
        scratch_shapes=[pltpu.VMEM((8, LANES), F32)],
        compiler_params=_cparams(1),
        name="moe_route",
    )(h, w_pad, b_pad)


def _layer_norm_rows(x, g, b):
    mu = jnp.mean(x, axis=-1, keepdims=True)
    xc = x - mu
    var = jnp.mean(xc * xc, axis=-1, keepdims=True)
    return xc * lax.rsqrt(var + NORM_EPS) * g + b


PACK_W = D_MODEL // 2
PACK_T = PACK_W // LANES
ROW_T = D_MODEL // LANES


def _pack_rows(y, out_ref):
    rows = y.shape[0]
    bits = pltpu.bitcast(y.astype(BF16).astype(F32), jnp.uint32)
    packed = bits[:, :PACK_W] | lax.shift_right_logical(bits[:, PACK_W:], jnp.uint32(16))
    for c in range(PACK_T):
        out_ref[pl.ds(c, rows, stride=PACK_T), :] = packed[:, c * LANES:(c + 1) * LANES]


def _unpack_rows(x_ref, out_ref, rows):
    for c in range(PACK_T):
        w = x_ref[pl.ds(c, rows, stride=PACK_T), :]
        hi = pltpu.bitcast(w & jnp.uint32(0xFFFF0000), F32)
        lo = pltpu.bitcast(lax.shift_left(w, jnp.uint32(16)), F32)
        out_ref[:, c * LANES:(c + 1) * LANES] = hi.astype(BF16)
        out_ref[:, PACK_W + c * LANES:PACK_W + (c + 1) * LANES] = lo.astype(BF16)


def _ln_body(h_ref, m_ref, g_ref, b_ref, o_ref, op_ref):
    y = _layer_norm_rows(DN_ALPHA * h_ref[...] + m_ref[...], g_ref[...], b_ref[...])
    o_ref[...] = y
    _pack_rows(y, op_ref)


def _ln_residual(h, mix, ln_g, ln_b, layer):
    tr = LN_TR
    row = pl.BlockSpec((tr, D_MODEL), lambda i: (i, 0))
    par = pl.BlockSpec((None, None, 1, D_MODEL), lambda i: (layer, 0, 0, 0))
    return pl.pallas_call(
        _ln_body, grid=(N_ROWS // tr,),
        in_specs=[row, row, par, par],
        out_specs=(row, pl.BlockSpec((tr * PACK_T, LANES), lambda i: (i, 0))),
        out_shape=(jax.ShapeDtypeStruct((N_ROWS, D_MODEL), F32),
                   jax.ShapeDtypeStruct((N_ROWS * PACK_T, LANES), jnp.uint32)),
        compiler_params=_cparams(1), name="ln_residual",
    )(h, mix, ln_g.reshape(DEPTH, 2, 1, D_MODEL), ln_b.reshape(DEPTH, 2, 1, D_MODEL))


MOE_TB = 128
MOE_NBLK = N_ROWS // MOE_TB
MOE_IDX = MOE_TB * TOP_K


def _idx_copy(dest_hbm, idx_smem, isem, blk, slot):
    return pltpu.make_async_copy(dest_hbm.at[blk], idx_smem.at[slot], isem.at[slot])


def _tile_rows(row, n):
    return pl.ds(pl.multiple_of(row * n, n), n)


def _dispatch_body(pstart_ref, pcnt_ref, nused_ref, dest_hbm, h_hbm, xs_hbm, idx_smem, zero_ref, hbuf,
                   isem, lsem, rsem, psem, tsem):
    i = pl.program_id(0)
    slot = i % 2
    blk_rows = MOE_TM * PACK_T
    step_rows = MOE_TB * PACK_T

    def load_tokens(blk, s):
        return pltpu.make_async_copy(h_hbm.at[_tile_rows(blk, step_rows)], hbuf.at[s], lsem.at[s])

    @pl.when(i == 0)
    def _first():
        _idx_copy(dest_hbm, idx_smem, isem, 0, 0).start()
        load_tokens(0, 0).start()
        load_tokens(1, 1).start()
        zero_ref[...] = jnp.zeros(zero_ref.shape, jnp.uint32)

        def tail_copy(blk):
            return pltpu.make_async_copy(zero_ref, xs_hbm.at[_tile_rows(blk, blk_rows)], tsem)

        def tail_start(blk, carry):
            tail_copy(blk).start()
            return carry

        def tail_wait(blk, carry):
            tail_copy(blk).wait()
            return carry

        lax.fori_loop(nused_ref[0], MOE_BLOCKS, tail_start, 0)
        total = jnp.int32(0)
        for e in range(N_EXPERTS):
            first = pstart_ref[e]
            cnt = pcnt_ref[e]

            def pad_row(j, carry, first=first):
                pltpu.make_async_copy(zero_ref.at[pl.ds(0, PACK_T)], xs_hbm.at[_tile_rows(first + j, PACK_T)],
                                      psem).start()
                return carry

            lax.fori_loop(0, cnt, pad_row, 0)
            total = total + cnt

        def wait_pad(j, carry):
            pltpu.make_async_copy(zero_ref.at[pl.ds(0, PACK_T)], xs_hbm.at[pl.ds(0, PACK_T)], psem).wait()
            return carry

        lax.fori_loop(0, total, wait_pad, 0)
        lax.fori_loop(nused_ref[0], MOE_BLOCKS, tail_wait, 0)

    _idx_copy(dest_hbm, idx_smem, isem, i, slot).wait()

    @pl.when(i + 1 < MOE_NBLK)
    def _next_idx():
        _idx_copy(dest_hbm, idx_smem, isem, i + 1, 1 - slot).start()

    hslot = i % 3
    load_tokens(i, hslot).wait()
    for r in range(MOE_TB):
        for k in range(TOP_K):
            d = idx_smem[slot, r * TOP_K + k]
            pltpu.make_async_copy(hbuf.at[hslot, pl.ds(r * PACK_T, PACK_T)], xs_hbm.at[_tile_rows(d, PACK_T)],
                                  rsem.at[slot]).start(priority=k % 2)

    def wait_tiles(s):
        n_rows = MOE_IDX * PACK_T
        pltpu.make_async_copy(xs_hbm.at[pl.ds(0, n_rows)], xs_hbm.at[pl.ds(0, n_rows)], rsem.at[s]).wait()

    @pl.when(i > 0)
    def _wait_previous():
        wait_tiles(1 - slot)

    @pl.when(i + 2 < MOE_NBLK)
    def _load_ahead():
        load_tokens(i + 2, (i + 2) % 3).start()

    @pl.when(i == MOE_NBLK - 1)
    def _wait_last():
        wait_tiles(slot)


def _dispatch(hp, dest2, pad_start, pad_cnt, n_used):
    grid_spec = pltpu.PrefetchScalarGridSpec(
        num_scalar_prefetch=3,
        grid=(MOE_NBLK,),
        in_specs=[pl.BlockSpec(memory_space=pl.ANY), pl.BlockSpec(memory_space=pl.ANY)],
        out_specs=pl.BlockSpec(memory_space=pl.ANY),
        scratch_shapes=[pltpu.SMEM((2, MOE_IDX), jnp.int32),
                        pltpu.VMEM((MOE_TM * PACK_T, LANES), jnp.uint32),
                        pltpu.VMEM((3, MOE_TB * PACK_T, LANES), jnp.uint32),
                        pltpu.SemaphoreType.DMA((2,)), pltpu.SemaphoreType.DMA((3,)),
                        pltpu.SemaphoreType.DMA((2,)),
                        pltpu.SemaphoreType.DMA(()), pltpu.SemaphoreType.DMA(())],
    )
    return pl.pallas_call(
        _dispatch_body, grid_spec=grid_spec,
        out_shape=jax.ShapeDtypeStruct((MOE_ROWS * PACK_T, LANES), jnp.uint32),
        compiler_params=_cparams(1), name="moe_dispatch",
    )(pad_start, pad_cnt, n_used, dest2, hp)


def _combine_body(dest_hbm, eo_hbm, h_ref, gate_ref, g_ref, b_ref, *rest, with_bf16):
    if with_bf16:
        o_ref, obf_ref, idx_smem, ebuf, xbuf, isem, rsem = rest
    else:
        o_ref, idx_smem, ebuf, xbuf, isem, rsem = rest
    i = pl.program_id(0)
    slot = i % 2

    def issue_rows(s):
        for r in range(MOE_TB):
            for k in range(TOP_K):
                d = idx_smem[s, r * TOP_K + k]
                pltpu.make_async_copy(eo_hbm.at[_tile_rows(d, ROW_T)], ebuf.at[s, k, pl.ds(r * ROW_T, ROW_T)],
                                      rsem.at[s]).start(priority=k % 2)

    @pl.when(i == 0)
    def _prime():
        first = _idx_copy(dest_hbm, idx_smem, isem, 0, 0)
        first.start()
        first.wait()
        issue_rows(0)
        _idx_copy(dest_hbm, idx_smem, isem, 1, 1).start()

    @pl.when(i + 1 < MOE_NBLK)
    def _prefetch():
        _idx_copy(dest_hbm, idx_smem, isem, i + 1, 1 - slot).wait()
        issue_rows(1 - slot)

        @pl.when(i + 2 < MOE_NBLK)
        def _next_idx():
            _idx_copy(dest_hbm, idx_smem, isem, i + 2, slot).start()

    pltpu.make_async_copy(ebuf.at[slot], ebuf.at[slot], rsem.at[slot]).wait()
    gates = gate_ref[...]
    for c in range(ROW_T):
        cs_ = slice(c * LANES, (c + 1) * LANES)
        f = ebuf[slot, 0, pl.ds(c, MOE_TB, stride=ROW_T), :] * gates[:, 0:1]
        for k in range(1, TOP_K):
            f = f + ebuf[slot, k, pl.ds(c, MOE_TB, stride=ROW_T), :] * gates[:, k:k + 1]
        xbuf[:, cs_] = DN_ALPHA * h_ref[:, cs_] + f
    y = _layer_norm_rows(xbuf[...], g_ref[...], b_ref[...])
    o_ref[...] = y
    if with_bf16:
        obf_ref[...] = y.astype(BF16)


def _ln_moe(h, eo, dest2, gates, ln_g, ln_b, layer, with_bf16):
    row = pl.BlockSpec((MOE_TB, D_MODEL), lambda i: (i, 0))
    par = pl.BlockSpec((None, None, 1, D_MODEL), lambda i: (layer, 1, 0, 0))
    out_shape = [jax.ShapeDtypeStruct((N_ROWS, D_MODEL), F32)]
    if with_bf16:
        out_shape.append(jax.ShapeDtypeStruct((N_ROWS, D_MODEL), BF16))
    outs = pl.pallas_call(
        functools.partial(_combine_body, with_bf16=with_bf16),
        grid=(MOE_NBLK,),
        in_specs=[pl.BlockSpec(memory_space=pl.ANY), pl.BlockSpec(memory_space=pl.ANY), row,
                  pl.BlockSpec((MOE_TB, LANES), lambda i: (i, 0)), par, par],
        out_specs=tuple([row] * len(out_shape)),
        out_shape=tuple(out_shape),
        scratch_shapes=[pltpu.SMEM((2, MOE_IDX), jnp.int32),
                        pltpu.VMEM((2, TOP_K, MOE_TB * ROW_T, LANES), F32),
                        pltpu.VMEM((MOE_TB, D_MODEL), F32),
                        pltpu.SemaphoreType.DMA((2,)), pltpu.SemaphoreType.DMA((2,))],
        compiler_params=_cparams(1), name="ln_moe_combine",
    )(dest2, eo, h, gates, ln_g.reshape(DEPTH, 2, 1, D_MODEL), ln_b.reshape(DEPTH, 2, 1, D_MODEL))
    return outs if with_bf16 else (outs[0], None)


def _moe_body(be_ref, par_ref, nxt_ref, nb_ref, x_ref, wgu_hbm, bgu_ref, wd_hbm, bd_ref, o_ref,
              wgu_f32, wd_f32, wgu_bf, wd_bf, xb_ref, wsem, *, layer):
    i = pl.program_id(0)
    e = be_ref[i]
    prev = be_ref[jnp.maximum(i - 1, 0)]
    slot = par_ref[i]
    used = i < nb_ref[0]

    def fetch(expert, s):
        return (pltpu.make_async_copy(wgu_hbm.at[layer, expert], wgu_f32.at[s], wsem.at[0, s]),
                pltpu.make_async_copy(wd_hbm.at[layer, expert], wd_f32.at[s], wsem.at[1, s]))

    @pl.when(i == 0)
    def _cold_start():
        for cp in fetch(e, 0):
            cp.start()

    @pl.when(jnp.logical_and(used, jnp.logical_or(i == 0, e != prev)))
    def _switch_expert():
        for cp in fetch(e, slot):
            cp.wait()
        nxt = nxt_ref[i]

        @pl.when(nxt >= 0)
        def _prefetch_next():
            for cp in fetch(nxt, 1 - slot):
                cp.start()

        wgu_bf[...] = wgu_f32[slot].astype(BF16)
        wd_bf[...] = wd_f32[slot].astype(BF16)

    @pl.when(used)
    def _compute():
        _unpack_rows(x_ref, xb_ref, MOE_TM)
        gu = _dot(xb_ref[...], wgu_bf[...]) + bgu_ref[...]
        gate = jnp.minimum(gu[:, :D_FF], SWIGLU_LIMIT)
        up = jnp.clip(gu[:, D_FF:], -SWIGLU_LIMIT, SWIGLU_LIMIT)
        act = (up + 1.0) * gate * jax.nn.sigmoid(SWIGLU_ALPHA * gate)
        out = _dot(act.astype(BF16), wd_bf[...]) + bd_ref[...]
        for c in range(ROW_T):
            o_ref[pl.ds(c, MOE_TM, stride=ROW_T), :] = out[:, c * LANES:(c + 1) * LANES]

    @pl.when(jnp.logical_not(used))
    def _unused_block():
        o_ref[...] = jnp.zeros(o_ref.shape, o_ref.dtype)


def _moe_experts(xs, blk_e, blk_par, blk_nxt, n_used, w_gu, b_gu, w_down, b_down, layer):
    grid_spec = pltpu.PrefetchScalarGridSpec(
        num_scalar_prefetch=4,
        grid=(MOE_BLOCKS,),
        in_specs=[
            pl.BlockSpec((MOE_TM * PACK_T, LANES), lambda i, be, pa, nx, nb: (jnp.minimum(i, nb[0] - 1), 0)),
            pl.BlockSpec(memory_space=pl.ANY),
            pl.BlockSpec((None, None, 1, 2 * D_FF), lambda i, be, pa, nx, nb: (layer, be[i], 0, 0)),
            pl.BlockSpec(memory_space=pl.ANY),
            pl.BlockSpec((None, None, 1, D_MODEL), lambda i, be, pa, nx, nb: (layer, be[i], 0, 0)),
        ],
        out_specs=pl.BlockSpec((MOE_TM * ROW_T, LANES), lambda i, be, pa, nx, nb: (i, 0)),
        scratch_shapes=[pltpu.VMEM((2, D_MODEL, 2 * D_FF), F32), pltpu.VMEM((2, D_FF, D_MODEL), F32),
                        pltpu.VMEM((D_MODEL, 2 * D_FF), BF16), pltpu.VMEM((D_FF, D_MODEL), BF16),
                        pltpu.VMEM((MOE_TM, D_MODEL), BF16),
                        pltpu.SemaphoreType.DMA((2, 2))],
    )
    return pl.pallas_call(
        functools.partial(_moe_body, layer=layer), grid_spec=grid_spec,
        out_shape=jax.ShapeDtypeStruct((MOE_ROWS * ROW_T, LANES), F32),
        compiler_params=_cparams(1), name="moe_experts",
    )(blk_e, blk_par, blk_nxt, n_used, xs, w_gu, b_gu.reshape(DEPTH, N_EXPERTS, 1, 2 * D_FF),
      w_down, b_down.reshape(DEPTH, N_EXPERTS, 1, D_MODEL))


def _moe_layer(h, hp, p, layer, with_bf16):
    idx, gates, rank, cnt = _route(h, p['moe_w_router'], p['moe_b_router'], layer)
    counts = cnt[0, :N_EXPERTS].astype(jnp.int32)
    psz = (counts + MOE_TM - 1) // MOE_TM * MOE_TM
    pend = jnp.cumsum(psz)
    pstart = pend - psz
    expert_ids = jnp.arange(N_EXPERTS, dtype=jnp.int32)
    top_i = idx[:, :TOP_K]
    base = jnp.sum(jnp.where(top_i[:, :, None] == expert_ids, pstart, 0), axis=-1)
    dest2 = (base + rank[:, :TOP_K]).astype(jnp.int32).reshape(MOE_NBLK, MOE_IDX)
    n_used = (pend[-1] // MOE_TM).astype(jnp.int32)
    blk = jnp.arange(MOE_BLOCKS, dtype=jnp.int32)
    blk_e = jnp.sum((blk[:, None] * MOE_TM >= pend[None, :]).astype(jnp.int32), axis=1)
    blk_e = jnp.minimum(blk_e, N_EXPERTS - 1)
    blk_e = jnp.where(blk < n_used, blk_e, blk_e[jnp.maximum(n_used - 1, 0)]).astype(jnp.int32)
    used = counts > 0
    ordinal = jnp.cumsum(used.astype(jnp.int32)) - 1
    cand = jnp.where(used, expert_ids, N_EXPERTS)
    later = jnp.concatenate([lax.cummin(cand[::-1])[::-1][1:], jnp.full((1,), N_EXPERTS, jnp.int32)])
    nxt_e = jnp.where(later >= N_EXPERTS, -1, later)
    blk_hot = blk_e[:, None] == expert_ids
    blk_par = jnp.sum(jnp.where(blk_hot, ordinal % 2, 0), axis=1).astype(jnp.int32)
    blk_nxt = jnp.sum(jnp.where(blk_hot, nxt_e, 0), axis=1).astype(jnp.int32)
    n_used = n_used.reshape(1)
    xs = _dispatch(hp, dest2, (pstart + counts).astype(jnp.int32), (psz - counts).astype(jnp.int32), n_used)
    eo = _moe_experts(xs, blk_e, blk_par, blk_nxt, n_used, p['moe_w_gu'], p['moe_b_gu'],
                      p['moe_w_down'], p['moe_b_down'], layer)
    return _ln_moe(h, eo, dest2, gates, p['ln_g'], p['ln_b'], layer, with_bf16)


def _ssd_body(z_ref, xbc_ref, dt_ref, h0_ref, c0_ref, cw_ref, cb_ref, dtb_ref, alog_ref, de_ref, nw_ref,
              eexp_ref, y_ref, hfin_ref, cfin_ref, ht_ref, ext_ref, act_ref, ybuf_ref, *, lv, nc):
    L = SSD_CHUNK
    N = M_DSTATE
    c = pl.program_id(1)
    n_pairs = M_NHEADS // 2
    x_cols = M_D_INNER
    b_col0 = M_D_INNER
    c_col0 = M_D_INNER + M_NGROUPS * M_DSTATE

    @pl.when(c == 0)
    def _init():
        for i in range(n_pairs):
            ht_ref[i] = h0_ref[2 * i:2 * i + 2].reshape(2 * M_HEADDIM, N).T
        ext_ref[0:8, :] = jnp.zeros((8, M_CONV_DIM), F32)
        ext_ref[5:8, :] = c0_ref[...]

    raw = xbc_ref[...]
    if lv < L:
        raw = jnp.concatenate([raw, jnp.zeros((L - lv, M_CONV_DIM), F32)], axis=0)
    ext_ref[8:8 + L, :] = raw
    cw_step = 256
    for j in range(M_CONV_DIM // cw_step):
        cs_ = slice(j * cw_step, (j + 1) * cw_step)
        ext = ext_ref[:, cs_]
        acc = cb_ref[:, cs_] + cw_ref[3:4, cs_] * ext[8:]
        for back in range(1, M_CONV):
            acc = acc + cw_ref[M_CONV - 1 - back:M_CONV - back, cs_] * pltpu.roll(ext, back, 0)[8:]
        act_ref[:, cs_] = jax.nn.silu(acc)
    carry = ext_ref[5 + lv:8 + lv, :]
    ext_ref[5:8, :] = carry
    cfin_ref[...] = carry

    dtr = dt_ref[...]
    if lv < L:
        dtr = jnp.concatenate([dtr, jnp.zeros((L - lv, LANES), F32)], axis=0)
    dt = jax.nn.softplus(dtr + dtb_ref[...])
    row = lax.broadcasted_iota(jnp.int32, (L, L), 0)
    col = lax.broadcasted_iota(jnp.int32, (L, L), 1)
    if lv < L:
        dt = jnp.where(row < lv, dt, 0.0)
    da = dt * (-jnp.exp(alog_ref[...]))
    causal = row >= col
    tri = jnp.where(causal, 1.0, 0.0).astype(BF16)
    d_hi, d_mid, d_lo = _split3(da)
    cs = _dot(tri, d_hi) + _dot(tri, d_mid) + _dot(tri, d_lo)
    cs_end = cs[L - 1:L, :]
    cst = cs.T
    dtt = dt.T
    wt = (dt * jnp.exp(cs_end - cs)).T
    e_hi, e_lo = _split2(jnp.broadcast_to(jnp.exp(cs_end), (8, LANES)))
    dend = _dot(e_hi, eexp_ref[...]) + _dot(e_lo, eexp_ref[...])
    lane = lax.broadcasted_iota(jnp.int32, (L, LANES), 1)
    first_head = lane < M_HEADDIM

    for g in range(M_NGROUPS):
        bg = act_ref[:, b_col0 + g * N:b_col0 + (g + 1) * N]
        cg = act_ref[:, c_col0 + g * N:c_col0 + (g + 1) * N]
        btg = bg.T
        cb = _dot(cg.astype(BF16), btg.astype(BF16))
        for pr in range(M_NHEADS // M_NGROUPS // 2):
            i = g * (M_NHEADS // M_NGROUPS // 2) + pr
            ps = slice(i * LANES, (i + 1) * LANES)
            xs_pair = act_ref[:, ps]
            ht_pair = ht_ref[i]
            lhs_rows = []
            bts_rows = []
            for j in range(2):
                hd = 2 * i + j
                colb = jnp.broadcast_to(cs[:, hd:hd + 1], (L, L))
                seg = colb - cst[hd:hd + 1, :]
                decay = jnp.exp(jnp.where(causal, seg, -jnp.inf))
                m_h = cb * decay * dtt[hd:hd + 1, :]
                c_h = cg * jnp.exp(colb)
                lhs_rows.append(jnp.concatenate([m_h, c_h], axis=1))
                bts_rows.append(btg * wt[hd:hd + 1, :])
            lhs = jnp.concatenate(lhs_rows, axis=0).astype(BF16)
            rhs = jnp.concatenate([xs_pair, ht_pair], axis=0).astype(BF16)
            out = _dot(lhs, rhs)
            y_pair = jnp.where(first_head, out[:L], out[L:]) + xs_pair * de_ref[:, ps]
            ybuf_ref[:, ps] = y_pair
            bts = jnp.concatenate(bts_rows, axis=0).astype(BF16)
            upd = _dot(bts, xs_pair.astype(BF16))
            ht_ref[i] = dend[0:1, ps] * ht_pair + jnp.where(first_head, upd[:N], upd[N:])

    gw = M_D_INNER // M_NGROUPS
    for g in range(M_NGROUPS):
        gs = slice(g * gw, (g + 1) * gw)
        yg = ybuf_ref[0:lv, gs] * jax.nn.silu(z_ref[:, gs])
        ms = jnp.mean(yg * yg, axis=-1, keepdims=True)
        y_ref[:, gs] = (yg * lax.rsqrt(ms + NORM_EPS) * nw_ref[:, gs]).astype(y_ref.dtype)

    @pl.when(c == nc - 1)
    def _final():
        for i in range(n_pairs):
            hfin_ref[2 * i:2 * i + 2] = ht_ref[i].T.reshape(2, M_HEADDIM, N)


def _ssd(z, xbc, dtr, h0, h0_layer, c0, c0_layer, prm, *, row0, lv, nb, nc, out_dtype):
    rb = row0 // lv

    def rows(width):
        return pl.BlockSpec((lv, width), lambda b, c: (rb + b * nc + c, 0))

    def par(shape):
        return pl.BlockSpec(shape, lambda b, c: (0, 0))

    if h0_layer is None:
        h0_spec = pl.BlockSpec((None, M_NHEADS, M_HEADDIM, M_DSTATE), lambda b, c: (b, 0, 0, 0))
        c0_spec = pl.BlockSpec((None, M_CONV - 1, M_CONV_DIM), lambda b, c: (b, 0, 0))
    else:
        h0_spec = pl.BlockSpec((None, None, M_NHEADS, M_HEADDIM, M_DSTATE),
                               lambda b, c: (h0_layer, b, 0, 0, 0))
        c0_spec = pl.BlockSpec((None, None, M_CONV - 1, M_CONV_DIM), lambda b, c: (c0_layer, b, 0, 0))
    out_rows = nb * nc * lv
    return pl.pallas_call(
        functools.partial(_ssd_body, lv=lv, nc=nc),
        grid=(nb, nc),
        in_specs=[rows(M_D_INNER), rows(M_CONV_DIM), rows(LANES), h0_spec, c0_spec,
                  par((M_CONV, M_CONV_DIM)), par((1, M_CONV_DIM)), par((1, LANES)), par((1, LANES)),
                  par((1, M_D_INNER)), par((1, M_D_INNER)), par((LANES, M_D_INNER))],
        out_specs=(pl.BlockSpec((lv, M_D_INNER), lambda b, c: (b * nc + c, 0)),
                   pl.BlockSpec((None, M_NHEADS, M_HEADDIM, M_DSTATE), lambda b, c: (b, 0, 0, 0)),
                   pl.BlockSpec((None, M_CONV - 1, M_CONV_DIM), lambda b, c: (b, 0, 0))),
        out_shape=(jax.ShapeDtypeStruct((out_rows, M_D_INNER), out_dtype),
                   jax.ShapeDtypeStruct((nb, M_NHEADS, M_HEADDIM, M_DSTATE), F32),
                   jax.ShapeDtypeStruct((nb, M_CONV - 1, M_CONV_DIM), F32)),
        scratch_shapes=[pltpu.VMEM((M_NHEADS // 2, M_DSTATE, LANES), F32),
                        pltpu.VMEM((8 + SSD_CHUNK, M_CONV_DIM), F32),
                        pltpu.VMEM((SSD_CHUNK, M_CONV_DIM), F32),
                        pltpu.VMEM((SSD_CHUNK, M_D_INNER), F32)],
        compiler_params=_cparams(2), name="ssd_mixer",
    )(z, xbc, dtr, h0, c0, *prm)


def _mamba_layer(h, hb, state_ssm, state_conv, p, layer):
    w_in = p['m_w_in']
    z = _matmul(hb, w_in, layer=layer, col0=0, n_out=M_D_INNER, tn=1024)
    xbc = _matmul(hb, w_in, layer=layer, col0=M_D_INNER, n_out=M_CONV_DIM, tn=1024)
    w_dt = jnp.pad(w_in[layer, :, M_D_INNER + M_CONV_DIM:], ((0, 0), (0, LANES - M_NHEADS)))
    dtr = _matmul(hb, w_dt, n_out=LANES, tn=LANES)

    def pad_heads(v):
        return jnp.pad(v[layer].astype(F32), (0, LANES - M_NHEADS)).reshape(1, LANES)

    head_of_col = jnp.arange(M_D_INNER, dtype=jnp.int32) // M_HEADDIM
    expand = (jnp.arange(LANES, dtype=jnp.int32)[:, None] == head_of_col[None, :]).astype(BF16)
    prm = (p['m_conv_w'][layer], p['m_conv_b'][layer].reshape(1, M_CONV_DIM),
           pad_heads(p['m_dt_bias']), pad_heads(p['m_a_log']),
           jnp.repeat(p['m_d'][layer].astype(F32), M_HEADDIM).reshape(1, M_D_INNER),
           p['m_norm_w'][layer].reshape(1, M_D_INNER), expand)
    zeros_h = jnp.zeros((BATCH, M_NHEADS, M_HEADDIM, M_DSTATE), F32)
    zeros_c = jnp.zeros((BATCH, M_CONV - 1, M_CONV_DIM), F32)
    y_m, h_m, c_m = _ssd(z, xbc, dtr, zeros_h, None, zeros_c, None, prm,
                         row0=ROW_META, lv=N_META, nb=BATCH, nc=1, out_dtype=BF16)
    y_x, h_x, c_x = _ssd(z, xbc, dtr, h_m, None, c_m, None, prm,
                         row0=0, lv=SSD_CHUNK, nb=BATCH, nc=SEQ // SSD_CHUNK, out_dtype=BF16)
    y_s, h_s, c_s = _ssd(z, xbc, dtr, state_ssm, layer, state_conv, layer, prm,
                         row0=ROW_SAMPLE, lv=DEC_SEQ, nb=DEC_BATCH, nc=1, out_dtype=F32)
    yn = jnp.concatenate([y_x, y_m, y_s.astype(BF16)], axis=0)
    mix = _matmul(yn, p['m_w_out'], layer=layer, n_out=D_MODEL, tn=512)
    h, hp = _ln_residual(h, mix, p['ln_g'], p['ln_b'], layer)
    return h, hp, (h_x, c_x, h_s, c_s)


def _lambda_value(lam_ref, lam_init):
    lp = lam_ref[...]
    s1 = jnp.sum(lp[0:1, :] * lp[1:2, :], axis=-1, keepdims=True)
    s2 = jnp.sum(lp[2:3, :] * lp[3:4, :], axis=-1, keepdims=True)
    return jnp.exp(s1) - jnp.exp(s2) + lam_init


def _subln(o, sw, lam_init):
    ms = jnp.mean(o * o, axis=-1, keepdims=True)
    return o * lax.rsqrt(ms + NORM_EPS) * sw * (1.0 - lam_init)


def _rep(x, n):
    return x if n == 1 else jnp.concatenate([x] * n, axis=1)


def _flash_body(q_ref, k_ref, v_ref, km_ref, vm_ref, lam_ref, sw_ref, o_ref,
                m_ref, l_ref, acc_ref, kb_ref, vb_ref, kpad_ref, vpad_ref, *, lam_init):
    t = FLASH_T
    dh = A_HEAD_DIM
    qi = pl.program_id(2)

    @pl.when(qi == 0)
    def _load_kv():
        kb_ref[...] = k_ref[...].astype(BF16)
        vb_ref[...] = v_ref[...].astype(BF16)
        kpad_ref[...] = jnp.zeros(kpad_ref.shape, BF16)
        vpad_ref[...] = jnp.zeros(vpad_ref.shape, BF16)
        kpad_ref[0:N_META, :] = km_ref[...].astype(BF16)
        vpad_ref[0:N_META, :] = vm_ref[...].astype(BF16)

    m_ref[...] = jnp.full(m_ref.shape, -jnp.inf, F32)
    l_ref[...] = jnp.zeros(l_ref.shape, F32)
    acc_ref[...] = jnp.zeros(acc_ref.shape, F32)
    qs = [q_ref[:, j * dh:(j + 1) * dh].astype(BF16) for j in range(2)]

    def scores(kb):
        return [_dot_nt(qs[j], kb[:, j * dh:(j + 1) * dh]) for j in range(2)]

    def update(s_heads, vb, mask):
        for j in range(2):
            s = s_heads[j]
            if mask is not None:
                s = jnp.where(mask, s, -jnp.inf)
            m_prev = m_ref[j]
            m_new = jnp.maximum(m_prev, jnp.max(s, axis=1, keepdims=True))
            alpha = jnp.exp(m_prev - m_new)
            pmat = jnp.exp(s - _rep(m_new, s.shape[1] // LANES))
            l_ref[j] = alpha * l_ref[j] + jnp.sum(pmat, axis=1, keepdims=True)
            acc_ref[j] = _rep(alpha, 2) * acc_ref[j] + _dot(pmat.astype(BF16), vb)
            m_ref[j] = m_new

    def kv_block(kj):
        off = pl.multiple_of(kj * t, t)
        return kb_ref[pl.ds(off, t), :], vb_ref[pl.ds(off, t), :]

    colm = lax.broadcasted_iota(jnp.int32, (t, LANES), 1)
    r = lax.broadcasted_iota(jnp.int32, (t, t), 0)
    cc = lax.broadcasted_iota(jnp.int32, (t, t), 1)
    k_diag, v_diag = kv_block(qi)
    s_meta = scores(kpad_ref[...])
    s_diag = scores(k_diag)
    update(s_meta, vpad_ref[...], colm < N_META)
    update(s_diag, v_diag, cc <= r)

    def two_blocks(i, carry):
        k_a, v_a = kv_block(2 * i)
        k_b, v_b = kv_block(2 * i + 1)
        s_a = scores(k_a)
        s_b = scores(k_b)
        update(s_a, v_a, None)
        update(s_b, v_b, None)
        return carry

    lax.fori_loop(0, qi // 2, two_blocks, 0)

    @pl.when(qi % 2 == 1)
    def _odd_block():
        k_a, v_a = kv_block(qi - 1)
        update(scores(k_a), v_a, None)

    lam = _lambda_value(lam_ref, lam_init)
    o1 = acc_ref[0] / _rep(l_ref[0], 2)
    o2 = acc_ref[1] / _rep(l_ref[1], 2)
    o_ref[...] = _subln(o1 - lam * o2, sw_ref[...], lam_init).astype(o_ref.dtype)


def _flash_prompt(q, k, v, lam_p, sw, j_layer, lam_init):
    t = FLASH_T
    nq = SEQ // t
    w2 = 2 * A_HEAD_DIM
    qspec = pl.BlockSpec((t, w2), lambda b, p, qi: (b * nq + qi, p))
    kspec = pl.BlockSpec((SEQ, w2), lambda b, p, qi: (b, p))
    mspec = pl.BlockSpec((N_META, w2), lambda b, p, qi: (ROW_META // N_META + b, p))
    return pl.pallas_call(
        functools.partial(_flash_body, lam_init=lam_init),
        grid=(BATCH, A_HEADS, nq),
        in_specs=[qspec, kspec, kspec, mspec, mspec,
                  pl.BlockSpec((None, 4, A_HEAD_DIM), lambda b, p, qi: (j_layer, 0, 0)),
                  pl.BlockSpec((None, 1, w2), lambda b, p, qi: (j_layer, 0, 0))],
        out_specs=pl.BlockSpec((t, w2), lambda b, p, qi: (b * nq + qi, p)),
        out_shape=jax.ShapeDtypeStruct((ROWS_X, D_MODEL), BF16),
        scratch_shapes=[pltpu.VMEM((2, t, LANES), F32), pltpu.VMEM((2, t, LANES), F32),
                        pltpu.VMEM((2, t, w2), F32),
                        pltpu.VMEM((SEQ, w2), BF16), pltpu.VMEM((SEQ, w2), BF16),
                        pltpu.VMEM((LANES, w2), BF16), pltpu.VMEM((LANES, w2), BF16)],
        compiler_params=_cparams(3), name="flash_prompt",
    )(q, k, v, k, v, lam_p, sw)


SATTN_PG = 4


SATTN_SLOTS = 3


def _sattn_body(pt_ref, q_ref, ck_hbm, cv_hbm, kn_ref, vn_ref, lam_ref, sw_ref, o_ref,
                qf_ref, qb_ref, m_ref, l_ref, acc_ref, kb_ref, vb_ref, kring, vring0, vring1, ring_sem,
                *, lq, nb, n_steps, pg, lam_init):
    dh = A_HEAD_DIM
    n_qk = 2 * A_HEADS
    rows = n_qk * lq
    j = pl.program_id(1)
    k_rows = PAGE_SIZE * n_qk
    v_rows = PAGE_SIZE * A_HEADS
    total = nb * n_steps

    def group_copies(t, slot):
        seq = t // n_steps
        first = (t % n_steps) * pg
        copies = []
        for g in range(pg):
            page = pt_ref[seq, first + g]
            copies.append(pltpu.make_async_copy(
                ck_hbm.at[page], kring.at[slot, pl.ds(g * k_rows, k_rows)], ring_sem.at[0, slot]))
            copies.append(pltpu.make_async_copy(
                cv_hbm.at[page, :, pl.ds(0, LANES)], vring0.at[slot, pl.ds(g * v_rows, v_rows)],
                ring_sem.at[1, slot]))
            copies.append(pltpu.make_async_copy(
                cv_hbm.at[page, :, pl.ds(LANES, LANES)], vring1.at[slot, pl.ds(g * v_rows, v_rows)],
                ring_sem.at[2, slot]))
        return copies

    def start_group(t):
        for cp in group_copies(t, t % SATTN_SLOTS):
            cp.start()

    @pl.when(j == 0)
    def _init():
        qf_ref[...] = jnp.zeros(qf_ref.shape, F32)
        for hd in range(n_qk):
            qf_ref[hd * lq:(hd + 1) * lq, hd * dh:(hd + 1) * dh] = q_ref[:, hd * dh:(hd + 1) * dh]
        qb_ref[...] = qf_ref[...].astype(BF16)
        m_ref[...] = jnp.full(m_ref.shape, -jnp.inf, F32)
        l_ref[...] = jnp.zeros(l_ref.shape, F32)
        acc_ref[...] = jnp.zeros(acc_ref.shape, F32)

    def process(tk, mask):
        s = _dot_nt(qb_ref[...], kb_ref[0:tk, :])
        if mask is not None:
            s = jnp.where(mask, s, -jnp.inf)
        m_prev = m_ref[...]
        m_new = jnp.maximum(m_prev, jnp.max(s, axis=1, keepdims=True))
        alpha = jnp.exp(m_prev - m_new)
        pmat = jnp.exp(s - _rep(m_new, tk // LANES))
        l_ref[...] = alpha * l_ref[...] + jnp.sum(pmat, axis=1, keepdims=True)
        m_ref[...] = m_new
        pb = pmat.astype(BF16)
        alpha2 = _rep(alpha, 2)
        for pp in range(A_HEADS):
            rs = slice(2 * pp * lq, (2 * pp + 2) * lq)
            acc_ref[rs, :] = alpha2[rs] * acc_ref[rs, :] + _dot(pb[rs], vb_ref[0:tk, pp * 2 * dh:(pp + 1) * 2 * dh])

    @pl.when(j < n_steps)
    def _past_pages():
        if n_steps == 0:
            return
        t = pl.program_id(0) * n_steps + j
        slot = t % SATTN_SLOTS

        @pl.when(t == 0)
        def _prime():
            start_group(0)
            if total > 1:
                start_group(1)

        @pl.when(t + 2 < total)
        def _prefetch():
            start_group(t + 2)

        for cp in group_copies(t, slot):
            cp.wait()
        vrings = (vring0, vring1)
        for g in range(pg):
            ts = slice(g * PAGE_SIZE, (g + 1) * PAGE_SIZE)
            for hd in range(n_qk):
                kb_ref[ts, hd * dh:(hd + 1) * dh] = (
                    kring[slot, pl.ds(g * k_rows + hd, PAGE_SIZE, stride=n_qk), :].astype(BF16))
            for hv in range(A_HEADS):
                for half in range(2):
                    c0 = hv * 2 * dh + half * LANES
                    vb_ref[ts, c0:c0 + LANES] = (
                        vrings[half][slot, pl.ds(g * v_rows + hv, PAGE_SIZE, stride=A_HEADS), :].astype(BF16))
        process(pg * PAGE_SIZE, None)

    @pl.when(j == n_steps)
    def _new_keys():
        pad = jnp.zeros((PAGE_SIZE - lq, D_MODEL), F32)
        kb_ref[0:PAGE_SIZE, :] = jnp.concatenate([kn_ref[...], pad], axis=0).astype(BF16)
        vb_ref[0:PAGE_SIZE, :] = jnp.concatenate([vn_ref[...], pad], axis=0).astype(BF16)
        r = lax.broadcasted_iota(jnp.int32, (rows, LANES), 0)
        cc = lax.broadcasted_iota(jnp.int32, (rows, LANES), 1)
        process(PAGE_SIZE, cc <= jnp.bitwise_and(r, lq - 1))
        lam = _lambda_value(lam_ref, lam_init)
        for pp in range(A_HEADS):
            r1 = slice(2 * pp * lq, (2 * pp + 1) * lq)
            r2 = slice((2 * pp + 1) * lq, (2 * pp + 2) * lq)
            o1 = acc_ref[r1, :] / _rep(l_ref[r1, :], 2)
            o2 = acc_ref[r2, :] / _rep(l_ref[r2, :], 2)
            o_ref[:, pp * 2 * dh:(pp + 1) * 2 * dh] = _subln(o1 - lam * o2, sw_ref[...], lam_init).astype(o_ref.dtype)


def _small_attn(q, k, v, cache_k, cache_v, page_table, lam_p, sw, j_layer, lam_init,
                *, row0, lq, nb, n_pages, out_dtype):
    rb = row0 // lq
    pg = SATTN_PG if n_pages > 0 else 1
    assert n_pages % pg == 0
    n_steps = n_pages // pg
    rows = 2 * A_HEADS * lq
    qspec = pl.BlockSpec((lq, D_MODEL), lambda b, j, pt: (rb + b, 0))

    n_pool = cache_k.shape[0]
    k_rows = PAGE_SIZE * 2 * A_HEADS
    v_rows = PAGE_SIZE * A_HEADS
    cache_k2 = cache_k.reshape(n_pool, k_rows, A_HEAD_DIM)
    cache_v2 = cache_v.reshape(n_pool, v_rows, 2 * A_HEAD_DIM)
    grid_spec = pltpu.PrefetchScalarGridSpec(
        num_scalar_prefetch=1,
        grid=(nb, n_steps + 1),
        in_specs=[qspec, pl.BlockSpec(memory_space=pl.ANY), pl.BlockSpec(memory_space=pl.ANY),
                  qspec, qspec,
                  pl.BlockSpec((None, 4, A_HEAD_DIM), lambda b, j, pt: (j_layer, 0, 0)),
                  pl.BlockSpec((None, 1, 2 * A_HEAD_DIM), lambda b, j, pt: (j_layer, 0, 0))],
        out_specs=pl.BlockSpec((lq, D_MODEL), lambda b, j, pt: (b, 0)),
        scratch_shapes=[pltpu.VMEM((rows, D_MODEL), F32), pltpu.VMEM((rows, D_MODEL), BF16),
                        pltpu.VMEM((rows, LANES), F32), pltpu.VMEM((rows, LANES), F32),
                        pltpu.VMEM((rows, 2 * A_HEAD_DIM), F32),
                        pltpu.VMEM((pg * PAGE_SIZE, D_MODEL), BF16),
                        pltpu.VMEM((pg * PAGE_SIZE, D_MODEL), BF16),
                        pltpu.VMEM((SATTN_SLOTS, pg * k_rows, LANES), F32),
                        pltpu.VMEM((SATTN_SLOTS, pg * v_rows, LANES), F32),
                        pltpu.VMEM((SATTN_SLOTS, pg * v_rows, LANES), F32),
                        pltpu.SemaphoreType.DMA((3, SATTN_SLOTS))],
    )
    return pl.pallas_call(
        functools.partial(_sattn_body, lq=lq, nb=nb, n_steps=n_steps, pg=pg, lam_init=lam_init),
        grid_spec=grid_spec,
        out_shape=jax.ShapeDtypeStruct((nb * lq, D_MODEL), out_dtype),
        compiler_params=_cparams(2), name="small_attn",
    )(page_table, q, cache_k2, cache_v2, k, v, lam_p, sw)


def _attn_layer(h, hb, k, v, rope_tabs, cache_k, cache_v, page_table, p, layer):
    j_layer = layer - N_A_LAYERS
    lam_init = 0.8 - 0.6 * math.exp(-0.3 * layer)
    q = _matmul(hb, p['a_w_q'], layer=j_layer, n_out=D_MODEL, tn=1024, rope=rope_tabs,
                scale=A_HEAD_DIM ** -0.5)
    lam_p = p['a_lambda'].astype(F32)
    sw = p['a_subln_w'].astype(F32).reshape(-1, 1, 2 * A_HEAD_DIM)
    o_x = _flash_prompt(q, k, v, lam_p, sw, j_layer, lam_init)
    dummy_pt = jnp.zeros((BATCH, 1), jnp.int32)
    o_m = _small_attn(q, k, v, cache_k, cache_v, dummy_pt, lam_p, sw, j_layer, lam_init,
                      row0=ROW_META, lq=N_META, nb=BATCH, n_pages=0, out_dtype=BF16)
    o_s = _small_attn(q, k, v, cache_k, cache_v, page_table, lam_p, sw, j_layer, lam_init,
                      row0=ROW_SAMPLE, lq=DEC_SEQ, nb=DEC_BATCH, n_pages=page_table.shape[1],
                      out_dtype=F32)
    o = jnp.concatenate([o_x, o_m, o_s.astype(BF16)], axis=0)
    mix = _matmul(o, p['a_w_o'], layer=j_layer, n_out=D_MODEL, tn=1024)
    return _ln_residual(h, mix, p['ln_g'], p['ln_b'], layer)


def _rope_tables():
    pos = jnp.concatenate([
        jnp.tile(N_META + jnp.arange(SEQ, dtype=jnp.int32), BATCH),
        jnp.tile(jnp.arange(N_META, dtype=jnp.int32), BATCH),
        jnp.tile(PAST_LEN + jnp.arange(DEC_SEQ, dtype=jnp.int32), DEC_BATCH)])
    half = A_HEAD_DIM // 2
    inv = ROPE_THETA ** (-jnp.arange(half, dtype=F32) / half)
    ang = pos.astype(F32)[:, None] * inv[None, :]
    cos = jnp.cos(ang)
    sin = jnp.sin(ang)
    return jnp.concatenate([cos, cos], axis=1), jnp.concatenate([-sin, sin], axis=1)


def kernel(x_prompt, x_sample, cache_k, cache_v, page_table, state_ssm, state_conv, meta_tokens, m_w_in, m_conv_w, m_conv_b, m_dt_bias, m_a_log, m_d, m_norm_w, m_w_out, a_w_q, a_w_k, a_w_v, a_lambda, a_subln_w, a_w_o, moe_w_router, moe_b_router, moe_w_gu, moe_b_gu, moe_w_down, moe_b_down, ln_g, ln_b):
    p = dict(m_w_in=m_w_in, m_conv_w=m_conv_w, m_conv_b=m_conv_b, m_dt_bias=m_dt_bias, m_a_log=m_a_log,
             m_d=m_d, m_norm_w=m_norm_w, m_w_out=m_w_out, a_w_q=a_w_q, a_w_k=a_w_k, a_w_v=a_w_v,
             a_lambda=a_lambda, a_subln_w=a_subln_w, a_w_o=a_w_o, moe_w_router=moe_w_router,
             moe_b_router=moe_b_router, moe_w_gu=moe_w_gu, moe_b_gu=moe_b_gu, moe_w_down=moe_w_down,
             moe_b_down=moe_b_down, ln_g=ln_g, ln_b=ln_b)
    assert page_table.shape[1] * PAGE_SIZE == PAST_LEN
    meta = jnp.broadcast_to(meta_tokens[None].astype(F32), (BATCH, N_META, D_MODEL))
    h = jnp.concatenate([x_prompt.reshape(ROWS_X, D_MODEL), meta.reshape(ROWS_META, D_MODEL),
                         x_sample.reshape(ROWS_SAMPLE, D_MODEL)], axis=0)
    hb = h.astype(BF16)
    rope_tabs = _rope_tables()

    ssm_p, conv_p, ssm_s, conv_s = [], [], [], []
    k = v = None
    for layer in range(DEPTH):
        if layer < N_A_LAYERS:
            h, hp, (h_x, c_x, h_s, c_s) = _mamba_layer(h, hb, state_ssm, state_conv, p, layer)
            ssm_p.append(h_x)
            conv_p.append(c_x)
            ssm_s.append(h_s)
            conv_s.append(c_s)
        else:
            if layer == N_A_LAYERS:
                k = _matmul(hb, a_w_k, n_out=D_MODEL, tn=1024, rope=rope_tabs)
                v = _matmul(hb, a_w_v, n_out=D_MODEL, tn=1024)
            h, hp = _attn_layer(h, hb, k, v, rope_tabs, cache_k, cache_v, page_table, p, layer)
        h, hb = _moe_layer(h, hp, p, layer, with_bf16=layer + 1 < DEPTH)

    def prompt_rows(a):
        return jnp.concatenate([a[ROW_META:ROW_SAMPLE].reshape(BATCH, N_META, D_MODEL),
                                a[:ROWS_X].reshape(BATCH, SEQ, D_MODEL)], axis=1)

    y_prompt = h[:ROWS_X].reshape(BATCH, SEQ, D_MODEL)
    y_sample = h[ROW_SAMPLE:].reshape(DEC_BATCH, DEC_SEQ, D_MODEL)
    k_prompt = prompt_rows(k).reshape(BATCH, SEQ + N_META, 2 * A_HEADS, A_HEAD_DIM)
    v_prompt = prompt_rows(v).reshape(BATCH, SEQ + N_META, A_HEADS, 2 * A_HEAD_DIM)
    k_sample = k[ROW_SAMPLE:].reshape(DEC_BATCH, DEC_SEQ, 2 * A_HEADS, A_HEAD_DIM)
    v_sample = v[ROW_SAMPLE:].reshape(DEC_BATCH, DEC_SEQ, A_HEADS, 2 * A_HEAD_DIM)
    return (y_prompt, y_sample, k_prompt, v_prompt, jnp.stack(ssm_p), jnp.stack(conv_p),
            k_sample, v_sample, jnp.stack(ssm_s), jnp.stack(conv_s))
```

```python
import functools
import math

import jax
import jax.numpy as jnp
from jax import lax
from jax.experimental import pallas as pl
from jax.experimental.pallas import tpu as pltpu

F32 = jnp.float32
BF16 = jnp.bfloat16

D_MODEL = 2048
BATCH = 4
SEQ = 2048
DEPTH = 4
DEC_BATCH = 8
DEC_SEQ = 8
PAGE_SIZE = 128
PAST_LEN = 16384
N_A_LAYERS = DEPTH // 2
N_META = 16
M_D_INNER = 2 * D_MODEL
M_HEADDIM = 64
M_NHEADS = M_D_INNER // M_HEADDIM
M_NGROUPS = 8
M_DSTATE = 128
M_CONV = 4
M_CONV_DIM = M_D_INNER + 2 * M_NGROUPS * M_DSTATE
A_HEADS = 8
A_HEAD_DIM = D_MODEL // (2 * A_HEADS)
ROPE_THETA = 10000.0
N_EXPERTS = 32
TOP_K = 4
D_FF = D_MODEL // 4
SWIGLU_LIMIT = 7.0
SWIGLU_ALPHA = 1.702
DN_ALPHA = (2 * DEPTH) ** 0.25
NORM_EPS = 1e-5

ROWS_X = BATCH * SEQ
ROW_META = ROWS_X
ROWS_META = BATCH * N_META
ROW_SAMPLE = ROW_META + ROWS_META
ROWS_SAMPLE = DEC_BATCH * DEC_SEQ
N_ROWS = ROW_SAMPLE + ROWS_SAMPLE

LANES = 128
VMEM_LIMIT = 56 * 1024 * 1024

SSD_CHUNK = 128
MOE_TM = 256
MOE_BLOCKS = N_ROWS * TOP_K // MOE_TM + N_EXPERTS
MOE_ROWS = MOE_BLOCKS * MOE_TM
MM_TM = 1040
ROUTE_TM = 640
LN_TR = 320
FLASH_T = 512


def _cparams(n_axes):
    return pltpu.CompilerParams(dimension_semantics=("arbitrary",) * n_axes,
                                vmem_limit_bytes=VMEM_LIMIT)


def _dot(a, b):
    return jnp.dot(a, b, preferred_element_type=F32)


def _dot_nt(a, b):
    return lax.dot_general(a, b, (((1,), (1,)), ((), ())), preferred_element_type=F32)


def _split2(x):
    hi = x.astype(BF16)
    lo = (x - hi.astype(F32)).astype(BF16)
    return hi, lo


def _split3(x):
    hi = x.astype(BF16)
    r = x - hi.astype(F32)
    mid = r.astype(BF16)
    lo = (r - mid.astype(F32)).astype(BF16)
    return hi, mid, lo


def _mm_body(*refs, rope, scale):
    if rope:
        x_ref, w_ref, cos_ref, sin_ref, o_ref, wbf_ref = refs
    else:
        x_ref, w_ref, o_ref, wbf_ref = refs

    @pl.when(pl.program_id(1) == 0)
    def _cast_weight():
        wbf_ref[...] = w_ref[...].astype(BF16)

    acc = _dot(x_ref[...].astype(BF16), wbf_ref[...])
    if rope:
        cos = cos_ref[...]
        sin = sin_ref[...]
        for j in range(acc.shape[1] // LANES):
            blk = acc[:, j * LANES:(j + 1) * LANES]
            out = (blk * cos + pltpu.roll(blk, LANES // 2, 1) * sin) * scale
            o_ref[:, j * LANES:(j + 1) * LANES] = out.astype(o_ref.dtype)
    else:
        o_ref[...] = acc.astype(o_ref.dtype)


def _matmul(x, w, *, layer=None, col0=0, n_out, tn, out_dtype=F32, rope=None, scale=1.0):
    m, k = x.shape
    tm = MM_TM
    assert m % tm == 0 and n_out % tn == 0 and col0 % tn == 0
    jb = col0 // tn
    if layer is None:
        w_spec = pl.BlockSpec((k, tn), lambda j, i: (0, j + jb))
    else:
        w_spec = pl.BlockSpec((None, k, tn), lambda j, i: (layer, 0, j + jb))
    in_specs = [pl.BlockSpec((tm, k), lambda j, i: (i, 0)), w_spec]
    args = [x, w]
    if rope is not None:
        in_specs += [pl.BlockSpec((tm, LANES), lambda j, i: (i, 0))] * 2
        args += list(rope)
    return pl.pallas_call(
        functools.partial(_mm_body, rope=rope is not None, scale=scale),
        grid=(n_out // tn, m // tm),
        in_specs=in_specs,
        out_specs=pl.BlockSpec((tm, tn), lambda j, i: (i, j)),
        out_shape=jax.ShapeDtypeStruct((m, n_out), out_dtype),
        scratch_shapes=[pltpu.VMEM((k, tn), BF16)],
        compiler_params=_cparams(2),
        name="dense_matmul",
    )(*args)


ROUTE_SUB = 128


def _lane_pack(cols, lane):
    out = jnp.zeros(lane.shape, cols[0].dtype)
    for k, c in enumerate(cols):
        out = jnp.where(lane == k, c, out)
    return out


def _route_body(x_ref, w_ref, b_ref, idx_ref, gate_ref, rank_ref, cnt_ref, run_ref):
    tm = x_ref.shape[0]

    @pl.when(pl.program_id(0) == 0)
    def _init():
        run_ref[...] = jnp.zeros(run_ref.shape, F32)

    xh, xl = _split2(x_ref[...])
    wh, wl = _split2(w_ref[...])
    logits = _dot(xh, wh) + _dot(xh, wl) + _dot(xl, wh) + b_ref[...]
    lane = lax.broadcasted_iota(jnp.int32, (ROUTE_SUB, LANES), 1)
    r_i = lax.broadcasted_iota(jnp.int32, (ROUTE_SUB, ROUTE_SUB), 0)
    c_i = lax.broadcasted_iota(jnp.int32, (ROUTE_SUB, ROUTE_SUB), 1)
    below = jnp.where(r_i > c_i, 1.0, 0.0).astype(BF16)
    running = run_ref[0:1, :]
    for sb in range(tm // ROUTE_SUB):
        rs = slice(sb * ROUTE_SUB, (sb + 1) * ROUTE_SUB)
        rest = logits[rs]
        vals, idxs, hots = [], [], []
        for _ in range(TOP_K):
            mx = jnp.max(rest, axis=1, keepdims=True)
            idx = jnp.min(jnp.where(rest == mx, lane, LANES), axis=1, keepdims=True)
            hot = lane == idx
            rest = jnp.where(hot, -jnp.inf, rest)
            vals.append(mx)
            idxs.append(idx)
            hots.append(hot)
        exps = [jnp.exp(v - vals[0]) for v in vals]
        denom = exps[0] + exps[1] + exps[2] + exps[3]
        idx_ref[rs, :] = _lane_pack(idxs, lane)
        gate_ref[rs, :] = _lane_pack([e / denom for e in exps], lane)
        sel = jnp.zeros((ROUTE_SUB, LANES), F32)
        for hot in hots:
            sel = sel + jnp.where(hot, 1.0, 0.0)
        before = _dot(below, sel.astype(BF16)) + running
        ranks = [jnp.sum(jnp.where(hot, before, 0.0), axis=1, keepdims=True) for hot in hots]
        rank_ref[rs, :] = _lane_pack(ranks, lane).astype(jnp.int32)
        running = running + jnp.sum(sel, axis=0, keepdims=True)
    run_ref[...] = jnp.broadcast_to(running, run_ref.shape)
    cnt_ref[...] = jnp.broadcast_to(running, cnt_ref.shape)


def _route(h, w_router, b_router, layer):
    tm = ROUTE_TM
    w_pad = jnp.pad(w_router, ((0, 0), (0, 0), (0, LANES - N_EXPERTS)))
    b_pad = jnp.pad(b_router.astype(F32), ((0, 0), (0, LANES - N_EXPERTS)),
                    constant_values=-1e30).reshape(DEPTH, 1, LANES)
    row = pl.BlockSpec((tm, LANES), lambda i: (i, 0))
    return pl.pallas_call(
        _route_body,
        grid=(N_ROWS // tm,),
        in_specs=[pl.BlockSpec((tm, D_MODEL), lambda i: (i, 0)),
                  pl.BlockSpec((None, D_MODEL, LANES), lambda i: (layer, 0, 0)),
                  pl.BlockSpec((None, 1, LANES), lambda i: (layer, 0, 0))],
        out_specs=(row, row, row, pl.BlockSpec((8, LANES), lambda i: (0, 0))),
        out_shape=(jax.ShapeDtypeStruct((N_ROWS, LANES), jnp.int32),
                   jax.ShapeDtypeStruct((N_ROWS, LANES), F32),
                   jax.ShapeDtypeStruct((N_ROWS, LANES), jnp.int32),
                   jax.ShapeDtypeStruct((8, LANES), F32)),
        scratch_shapes=[pltpu.VMEM((8, LANES), F32)],
        compiler_params=_cparams(1),
        name="moe_route",
    )(h, w_pad, b_pad)


def _layer_norm_rows(x, g, b):
    mu = jnp.mean(x, axis=-1, keepdims=True)
    xc = x - mu
    var = jnp.mean(xc * xc, axis=-1, keepdims=True)
    return xc * lax.rsqrt(var + NORM_EPS) * g + b


PACK_W = D_MODEL // 2
PACK_T = PACK_W // LANES
ROW_T = D_MODEL // LANES


def _pack_rows(y, out_ref):
    rows = y.shape[0]
    bits = pltpu.bitcast(y.astype(BF16).astype(F32), jnp.uint32)
    packed = bits[:, :PACK_W] | lax.shift_right_logical(bits[:, PACK_W:], jnp.uint32(16))
    for c in range(PACK_T):
        out_ref[pl.ds(c, rows, stride=PACK_T), :] = packed[:, c * LANES:(c + 1) * LANES]


def _unpack_rows(x_ref, out_ref, rows):
    for c in range(PACK_T):
        w = x_ref[pl.ds(c, rows, stride=PACK_T), :]
        hi = pltpu.bitcast(w & jnp.uint32(0xFFFF0000), F32)
        lo = pltpu.bitcast(lax.shift_left(w, jnp.uint32(16)), F32)
        out_ref[:, c * LANES:(c + 1) * LANES] = hi.astype(BF16)
        out_ref[:, PACK_W + c * LANES:PACK_W + (c + 1) * LANES] = lo.astype(BF16)


def _ln_body(h_ref, m_ref, g_ref, b_ref, o_ref, op_ref):
    y = _layer_norm_rows(DN_ALPHA * h_ref[...] + m_ref[...], g_ref[...], b_ref[...])
    o_ref[...] = y
    _pack_rows(y, op_ref)


def _ln_residual(h, mix, ln_g, ln_b, layer):
    tr = LN_TR
    row = pl.BlockSpec((tr, D_MODEL), lambda i: (i, 0))
    par = pl.BlockSpec((None, None, 1, D_MODEL), lambda i: (layer, 0, 0, 0))
    return pl.pallas_call(
        _ln_body, grid=(N_ROWS // tr,),
        in_specs=[row, row, par, par],
        out_specs=(row, pl.BlockSpec((tr * PACK_T, LANES), lambda i: (i, 0))),
        out_shape=(jax.ShapeDtypeStruct((N_ROWS, D_MODEL), F32),
                   jax.ShapeDtypeStruct((N_ROWS * PACK_T, LANES), jnp.uint32)),
        compiler_params=_cparams(1), name="ln_residual",
    )(h, mix, ln_g.reshape(DEPTH, 2, 1, D_MODEL), ln_b.reshape(DEPTH, 2, 1, D_MODEL))


MOE_TB = 128
MOE_NBLK = N_ROWS // MOE_TB
MOE_IDX = MOE_TB * TOP_K


def _idx_copy(dest_hbm, idx_smem, isem, blk, slot):
    return pltpu.make_async_copy(dest_hbm.at[blk], idx_smem.at[slot], isem.at[slot])


def _tile_rows(row, n):
    return pl.ds(pl.multiple_of(row * n, n), n)


def _dispatch_body(pstart_ref, pcnt_ref, nused_ref, dest_hbm, h_hbm, xs_hbm, idx_smem, zero_ref, hbuf,
                   isem, lsem, rsem, psem, tsem):
    i = pl.program_id(0)
    slot = i % 2
    blk_rows = MOE_TM * PACK_T
    step_rows = MOE_TB * PACK_T

    def load_tokens(blk, s):
        return pltpu.make_async_copy(h_hbm.at[_tile_rows(blk, step_rows)], hbuf.at[s], lsem.at[s])

    @pl.when(i == 0)
    def _first():
        _idx_copy(dest_hbm, idx_smem, isem, 0, 0).start()
        load_tokens(0, 0).start()
        load_tokens(1, 1).start()
        zero_ref[...] = jnp.zeros(zero_ref.shape, jnp.uint32)

        def tail_copy(blk):
            return pltpu.make_async_copy(zero_ref, xs_hbm.at[_tile_rows(blk, blk_rows)], tsem)

        def tail_start(blk, carry):
            tail_copy(blk).start()
            return carry

        def tail_wait(blk, carry):
            tail_copy(blk).wait()
            return carry

        lax.fori_loop(nused_ref[0], MOE_BLOCKS, tail_start, 0)
        total = jnp.int32(0)
        for e in range(N_EXPERTS):
            first = pstart_ref[e]
            cnt = pcnt_ref[e]

            def pad_row(j, carry, first=first):
                pltpu.make_async_copy(zero_ref.at[pl.ds(0, PACK_T)], xs_hbm.at[_tile_rows(first + j, PACK_T)],
                                      psem).start()
                return carry

            lax.fori_loop(0, cnt, pad_row, 0)
            total = total + cnt

        def wait_pad(j, carry):
            pltpu.make_async_copy(zero_ref.at[pl.ds(0, PACK_T)], xs_hbm.at[pl.ds(0, PACK_T)], psem).wait()
            return carry

        lax.fori_loop(0, total, wait_pad, 0)
        lax.fori_loop(nused_ref[0], MOE_BLOCKS, tail_wait, 0)

    _idx_copy(dest_hbm, idx_smem, isem, i, slot).wait()

    @pl.when(i + 1 < MOE_NBLK)
    def _next_idx():
        _idx_copy(dest_hbm, idx_smem, isem, i + 1, 1 - slot).start()

    hslot = i % 3
    load_tokens(i, hslot).wait()
    for r in range(MOE_TB):
        for k in range(TOP_K):
            d = idx_smem[slot, r * TOP_K + k]
            pltpu.make_async_copy(hbuf.at[hslot, pl.ds(r * PACK_T, PACK_T)], xs_hbm.at[_tile_rows(d, PACK_T)],
                                  rsem.at[slot]).start(priority=k % 2)

    def wait_tiles(s):
        n_rows = MOE_IDX * PACK_T
        pltpu.make_async_copy(xs_hbm.at[pl.ds(0, n_rows)], xs_hbm.at[pl.ds(0, n_rows)], rsem.at[s]).wait()

    @pl.when(i > 0)
    def _wait_previous():
        wait_tiles(1 - slot)

    @pl.when(i + 2 < MOE_NBLK)
    def _load_ahead():
        load_tokens(i + 2, (i + 2) % 3).start()

    @pl.when(i == MOE_NBLK - 1)
    def _wait_last():
        wait_tiles(slot)


def _dispatch(hp, dest2, pad_start, pad_cnt, n_used):
    grid_spec = pltpu.PrefetchScalarGridSpec(
        num_scalar_prefetch=3,
        grid=(MOE_NBLK,),
        in_specs=[pl.BlockSpec(memory_space=pl.ANY), pl.BlockSpec(memory_space=pl.ANY)],
        out_specs=pl.BlockSpec(memory_space=pl.ANY),
        scratch_shapes=[pltpu.SMEM((2, MOE_IDX), jnp.int32),
                        pltpu.VMEM((MOE_TM * PACK_T, LANES), jnp.uint32),
                        pltpu.VMEM((3, MOE_TB * PACK_T, LANES), jnp.uint32),
                        pltpu.SemaphoreType.DMA((2,)), pltpu.SemaphoreType.DMA((3,)),
                        pltpu.SemaphoreType.DMA((2,)),
                        pltpu.SemaphoreType.DMA(()), pltpu.SemaphoreType.DMA(())],
    )
    return pl.pallas_call(
        _dispatch_body, grid_spec=grid_spec,
        out_shape=jax.ShapeDtypeStruct((MOE_ROWS * PACK_T, LANES), jnp.uint32),
        compiler_params=_cparams(1), name="moe_dispatch",
    )(pad_start, pad_cnt, n_used, dest2, hp)


def _combine_body(dest_hbm, eo_hbm, h_ref, gate_ref, g_ref, b_ref, *rest, with_bf16):
    if with_bf16:
        o_ref, obf_ref, idx_smem, ebuf, xbuf, isem, rsem = rest
    else:
        o_ref, idx_smem, ebuf, xbuf, isem, rsem = rest
    i = pl.program_id(0)
    slot = i % 2

    def issue_rows(s):
        for r in range(MOE_TB):
            for k in range(TOP_K):
                d = idx_smem[s, r * TOP_K + k]
                pltpu.make_async_copy(eo_hbm.at[_tile_rows(d, ROW_T)], ebuf.at[s, k, pl.ds(r * ROW_T, ROW_T)],
                                      rsem.at[s]).start(priority=k % 2)

    @pl.when(i == 0)
    def _prime():
        first = _idx_copy(dest_hbm, idx_smem, isem, 0, 0)
        first.start()
        first.wait()
        issue_rows(0)
        _idx_copy(dest_hbm, idx_smem, isem, 1, 1).start()

    @pl.when(i + 1 < MOE_NBLK)
    def _prefetch():
        _idx_copy(dest_hbm, idx_smem, isem, i + 1, 1 - slot).wait()
        issue_rows(1 - slot)

        @pl.when(i + 2 < MOE_NBLK)
        def _next_idx():
            _idx_copy(dest_hbm, idx_smem, isem, i + 2, slot).start()

    pltpu.make_async_copy(ebuf.at[slot], ebuf.at[slot], rsem.at[slot]).wait()
    gates = gate_ref[...]
    for c in range(ROW_T):
        cs_ = slice(c * LANES, (c + 1) * LANES)
        f = ebuf[slot, 0, pl.ds(c, MOE_TB, stride=ROW_T), :] * gates[:, 0:1]
        for k in range(1, TOP_K):
            f = f + ebuf[slot, k, pl.ds(c, MOE_TB, stride=ROW_T), :] * gates[:, k:k + 1]
        xbuf[:, cs_] = DN_ALPHA * h_ref[:, cs_] + f
    y = _layer_norm_rows(xbuf[...], g_ref[...], b_ref[...])
    o_ref[...] = y
    if with_bf16:
        obf_ref[...] = y.astype(BF16)


def _ln_moe(h, eo, dest2, gates, ln_g, ln_b, layer, with_bf16):
    row = pl.BlockSpec((MOE_TB, D_MODEL), lambda i: (i, 0))
    par = pl.BlockSpec((None, None, 1, D_MODEL), lambda i: (layer, 1, 0, 0))
    out_shape = [jax.ShapeDtypeStruct((N_ROWS, D_MODEL), F32)]
    if with_bf16:
        out_shape.append(jax.ShapeDtypeStruct((N_ROWS, D_MODEL), BF16))
    outs = pl.pallas_call(
        functools.partial(_combine_body, with_bf16=with_bf16),
        grid=(MOE_NBLK,),
        in_specs=[pl.BlockSpec(memory_space=pl.ANY), pl.BlockSpec(memory_space=pl.ANY), row,
                  pl.BlockSpec((MOE_TB, LANES), lambda i: (i, 0)), par, par],
        out_specs=tuple([row] * len(out_shape)),
        out_shape=tuple(out_shape),
        scratch_shapes=[pltpu.SMEM((2, MOE_IDX), jnp.int32),
                        pltpu.VMEM((2, TOP_K, MOE_TB * ROW_T, LANES), F32),
                        pltpu.VMEM((MOE_TB, D_MODEL), F32),
                        pltpu.SemaphoreType.DMA((2,)), pltpu.SemaphoreType.DMA((2,))],
        compiler_params=_cparams(1), name="ln_moe_combine",
    )(dest2, eo, h, gates, ln_g.reshape(DEPTH, 2, 1, D_MODEL), ln_b.reshape(DEPTH, 2, 1, D_MODEL))
    return outs if with_bf16 else (outs[0], None)


def _moe_body(be_ref, par_ref, nxt_ref, nb_ref, x_ref, wgu_hbm, bgu_ref, wd_hbm, bd_ref, o_ref,
              wgu_f32, wd_f32, wgu_bf, wd_bf, xb_ref, wsem, *, layer):
    i = pl.program_id(0)
    e = be_ref[i]
    prev = be_ref[jnp.maximum(i - 1, 0)]
    slot = par_ref[i]
    used = i < nb_ref[0]

    def fetch(expert, s):
        return (pltpu.make_async_copy(wgu_hbm.at[layer, expert], wgu_f32.at[s], wsem.at[0, s]),
                pltpu.make_async_copy(wd_hbm.at[layer, expert], wd_f32.at[s], wsem.at[1, s]))

    @pl.when(i == 0)
    def _cold_start():
        for cp in fetch(e, 0):
            cp.start()

    @pl.when(jnp.logical_and(used, jnp.logical_or(i == 0, e != prev)))
    def _switch_expert():
        for cp in fetch(e, slot):
            cp.wait()
        nxt = nxt_ref[i]

        @pl.when(nxt >= 0)
        def _prefetch_next():
            for cp in fetch(nxt, 1 - slot):
                cp.start()

        wgu_bf[...] = wgu_f32[slot].astype(BF16)
        wd_bf[...] = wd_f32[slot].astype(BF16)

    @pl.when(used)
    def _compute():
        _unpack_rows(x_ref, xb_ref, MOE_TM)
        gu = _dot(xb_ref[...], wgu_bf[...]) + bgu_ref[...]
        gate = jnp.minimum(gu[:, :D_FF], SWIGLU_LIMIT)
        up = jnp.clip(gu[:, D_FF:], -SWIGLU_LIMIT, SWIGLU_LIMIT)
        act = (up + 1.0) * gate * jax.nn.sigmoid(SWIGLU_ALPHA * gate)
        out = _dot(act.astype(BF16), wd_bf[...]) + bd_ref[...]
        for c in range(ROW_T):
            o_ref[pl.ds(c, MOE_TM, stride=ROW_T), :] = out[:, c * LANES:(c + 1) * LANES]

    @pl.when(jnp.logical_not(used))
    def _unused_block():
        o_ref[...] = jnp.zeros(o_ref.shape, o_ref.dtype)


def _moe_experts(xs, blk_e, blk_par, blk_nxt, n_used, w_gu, b_gu, w_down, b_down, layer):
    grid_spec = pltpu.PrefetchScalarGridSpec(
        num_scalar_prefetch=4,
        grid=(MOE_BLOCKS,),
        in_specs=[
            pl.BlockSpec((MOE_TM * PACK_T, LANES), lambda i, be, pa, nx, nb: (jnp.minimum(i, nb[0] - 1), 0)),
            pl.BlockSpec(memory_space=pl.ANY),
            pl.BlockSpec((None, None, 1, 2 * D_FF), lambda i, be, pa, nx, nb: (layer, be[i], 0, 0)),
            pl.BlockSpec(memory_space=pl.ANY),
            pl.BlockSpec((None, None, 1, D_MODEL), lambda i, be, pa, nx, nb: (layer, be[i], 0, 0)),
        ],
        out_specs=pl.BlockSpec((MOE_TM * ROW_T, LANES), lambda i, be, pa, nx, nb: (i, 0)),
        scratch_shapes=[pltpu.VMEM((2, D_MODEL, 2 * D_FF), F32), pltpu.VMEM((2, D_FF, D_MODEL), F32),
                        pltpu.VMEM((D_MODEL, 2 * D_FF), BF16), pltpu.VMEM((D_FF, D_MODEL), BF16),
                        pltpu.VMEM((MOE_TM, D_MODEL), BF16),
                        pltpu.SemaphoreType.DMA((2, 2))],
    )
    return pl.pallas_call(
        functools.partial(_moe_body, layer=layer), grid_spec=grid_spec,
        out_shape=jax.ShapeDtypeStruct((MOE_ROWS * ROW_T, LANES), F32),
        compiler_params=_cparams(1), name="moe_experts",
    )(blk_e, blk_par, blk_nxt, n_used, xs, w_gu, b_gu.reshape(DEPTH, N_EXPERTS, 1, 2 * D_FF),
      w_down, b_down.reshape(DEPTH, N_EXPERTS, 1, D_MODEL))


def _moe_layer(h, hp, p, layer, with_bf16):
    idx, gates, rank, cnt = _route(h, p['moe_w_router'], p['moe_b_router'], layer)
    counts = cnt[0, :N_EXPERTS].astype(jnp.int32)
    psz = (counts + MOE_TM - 1) // MOE_TM * MOE_TM
    pend = jnp.cumsum(psz)
    pstart = pend - psz
    expert_ids = jnp.arange(N_EXPERTS, dtype=jnp.int32)
    top_i = idx[:, :TOP_K]
    base = jnp.sum(jnp.where(top_i[:, :, None] == expert_ids, pstart, 0), axis=-1)
    dest2 = (base + rank[:, :TOP_K]).astype(jnp.int32).reshape(MOE_NBLK, MOE_IDX)
    n_used = (pend[-1] // MOE_TM).astype(jnp.int32)
    blk = jnp.arange(MOE_BLOCKS, dtype=jnp.int32)
    blk_e = jnp.sum((blk[:, None] * MOE_TM >= pend[None, :]).astype(jnp.int32), axis=1)
    blk_e = jnp.minimum(blk_e, N_EXPERTS - 1)
    blk_e = jnp.where(blk < n_used, blk_e, blk_e[jnp.maximum(n_used - 1, 0)]).astype(jnp.int32)
    used = counts > 0
    ordinal = jnp.cumsum(used.astype(jnp.int32)) - 1
    cand = jnp.where(used, expert_ids, N_EXPERTS)
    later = jnp.concatenate([lax.cummin(cand[::-1])[::-1][1:], jnp.full((1,), N_EXPERTS, jnp.int32)])
    nxt_e = jnp.where(later >= N_EXPERTS, -1, later)
    blk_hot = blk_e[:, None] == expert_ids
    blk_par = jnp.sum(jnp.where(blk_hot, ordinal % 2, 0), axis=1).astype(jnp.int32)
    blk_nxt = jnp.sum(jnp.where(blk_hot, nxt_e, 0), axis=1).astype(jnp.int32)
    n_used = n_used.reshape(1)
    xs = _dispatch(hp, dest2, (pstart + counts).astype(jnp.int32), (psz - counts).astype(jnp.int32), n_used)
    eo = _moe_experts(xs, blk_e, blk_par, blk_nxt, n_used, p['moe_w_gu'], p['moe_b_gu'],
                      p['moe_w_down'], p['moe_b_down'], layer)
    return _ln_moe(h, eo, dest2, gates, p['ln_g'], p['ln_b'], layer, with_bf16)


def _ssd_body(z_ref, xbc_ref, dt_ref, h0_ref, c0_ref, cw_ref, cb_ref, dtb_ref, alog_ref, de_ref, nw_ref,
              eexp_ref, y_ref, hfin_ref, cfin_ref, ht_ref, ext_ref, act_ref, ybuf_ref, *, lv, nc):
    L = SSD_CHUNK
    N = M_DSTATE
    c = pl.program_id(1)
    n_pairs = M_NHEADS // 2
    x_cols = M_D_INNER
    b_col0 = M_D_INNER
    c_col0 = M_D_INNER + M_NGROUPS * M_DSTATE

    @pl.when(c == 0)
    def _init():
        for i in range(n_pairs):
            ht_ref[i] = h0_ref[2 * i:2 * i + 2].reshape(2 * M_HEADDIM, N).T
        ext_ref[0:8, :] = jnp.zeros((8, M_CONV_DIM), F32)
        ext_ref[5:8, :] = c0_ref[...]

    raw = xbc_ref[...]
    if lv < L:
        raw = jnp.concatenate([raw, jnp.zeros((L - lv, M_CONV_DIM), F32)], axis=0)
    ext_ref[8:8 + L, :] = raw
    cw_step = 256
    for j in range(M_CONV_DIM // cw_step):
        cs_ = slice(j * cw_step, (j + 1) * cw_step)
        ext = ext_ref[:, cs_]
        acc = cb_ref[:, cs_] + cw_ref[3:4, cs_] * ext[8:]
        for back in range(1, M_CONV):
            acc = acc + cw_ref[M_CONV - 1 - back:M_CONV - back, cs_] * pltpu.roll(ext, back, 0)[8:]
        act_ref[:, cs_] = jax.nn.silu(acc)
    carry = ext_ref[5 + lv:8 + lv, :]
    ext_ref[5:8, :] = carry
    cfin_ref[...] = carry

    dtr = dt_ref[...]
    if lv < L:
        dtr = jnp.concatenate([dtr, jnp.zeros((L - lv, LANES), F32)], axis=0)
    dt = jax.nn.softplus(dtr + dtb_ref[...])
    row = lax.broadcasted_iota(jnp.int32, (L, L), 0)
    col = lax.broadcasted_iota(jnp.int32, (L, L), 1)
    if lv < L:
        dt = jnp.where(row < lv, dt, 0.0)
    da = dt * (-jnp.exp(alog_ref[...]))
    causal = row >= col
    tri = jnp.where(causal, 1.0, 0.0).astype(BF16)
    d_hi, d_mid, d_lo = _split3(da)
    cs = _dot(tri, d_hi) + _dot(tri, d_mid) + _dot(tri, d_lo)
    cs_end = cs[L - 1:L, :]
    cst = cs.T
    dtt = dt.T
    wt = (dt * jnp.exp(cs_end - cs)).T
    e_hi, e_lo = _split2(jnp.broadcast_to(jnp.exp(cs_end), (8, LANES)))
    dend = _dot(e_hi, eexp_ref[...]) + _dot(e_lo, eexp_ref[...])
    lane = lax.broadcasted_iota(jnp.int32, (L, LANES), 1)
    first_head = lane < M_HEADDIM

    for g in range(M_NGROUPS):
        bg = act_ref[:, b_col0 + g * N:b_col0 + (g + 1) * N]
        cg = act_ref[:, c_col0 + g * N:c_col0 + (g + 1) * N]
        btg = bg.T
        cb = _dot(cg.astype(BF16), btg.astype(BF16))
        for pr in range(M_NHEADS // M_NGROUPS // 2):
            i = g * (M_NHEADS // M_NGROUPS // 2) + pr
            ps = slice(i * LANES, (i + 1) * LANES)
            xs_pair = act_ref[:, ps]
            ht_pair = ht_ref[i]
            lhs_rows = []
            bts_rows = []
            for j in range(2):
                hd = 2 * i + j
                colb = jnp.broadcast_to(cs[:, hd:hd + 1], (L, L))
                seg = colb - cst[hd:hd + 1, :]
                decay = jnp.exp(jnp.where(causal, seg, -jnp.inf))
                m_h = cb * decay * dtt[hd:hd + 1, :]
                c_h = cg * jnp.exp(colb)
                lhs_rows.append(jnp.concatenate([m_h, c_h], axis=1))
                bts_rows.append(btg * wt[hd:hd + 1, :])
            lhs = jnp.concatenate(lhs_rows, axis=0).astype(BF16)
            rhs = jnp.concatenate([xs_pair, ht_pair], axis=0).astype(BF16)
            out = _dot(lhs, rhs)
            y_pair = jnp.where(first_head, out[:L], out[L:]) + xs_pair * de_ref[:, ps]
            ybuf_ref[:, ps] = y_pair
            bts = jnp.concatenate(bts_rows, axis=0).astype(BF16)
            upd = _dot(bts, xs_pair.astype(BF16))
            ht_ref[i] = dend[0:1, ps] * ht_pair + jnp.where(first_head, upd[:N], upd[N:])

    gw = M_D_INNER // M_NGROUPS
    for g in range(M_NGROUPS):
        gs = slice(g * gw, (g + 1) * gw)
        yg = ybuf_ref[0:lv, gs] * jax.nn.silu(z_ref[:, gs])
        ms = jnp.mean(yg * yg, axis=-1, keepdims=True)
        y_ref[:, gs] = (yg * lax.rsqrt(ms + NORM_EPS) * nw_ref[:, gs]).astype(y_ref.dtype)

    @pl.when(c == nc - 1)
    def _final():
        for i in range(n_pairs):
            hfin_ref[2 * i:2 * i + 2] = ht_ref[i].T.reshape(2, M_HEADDIM, N)


def _ssd(z, xbc, dtr, h0, h0_layer, c0, c0_layer, prm, *, row0, lv, nb, nc, out_dtype):
    rb = row0 // lv

    def rows(width):
        return pl.BlockSpec((lv, width), lambda b, c: (rb + b * nc + c, 0))

    def par(shape):
        return pl.BlockSpec(shape, lambda b, c: (0, 0))

    if h0_layer is None:
        h0_spec = pl.BlockSpec((None, M_NHEADS, M_HEADDIM, M_DSTATE), lambda b, c: (b, 0, 0, 0))
        c0_spec = pl.BlockSpec((None, M_CONV - 1, M_CONV_DIM), lambda b, c: (b, 0, 0))
    else:
        h0_spec = pl.BlockSpec((None, None, M_NHEADS, M_HEADDIM, M_DSTATE),
                               lambda b, c: (h0_layer, b, 0, 0, 0))
        c0_spec = pl.BlockSpec((None, None, M_CONV - 1, M_CONV_DIM), lambda b, c: (c0_layer, b, 0, 0))
    out_rows = nb * nc * lv
    return pl.pallas_call(
        functools.partial(_ssd_body, lv=lv, nc=nc),
        grid=(nb, nc),
        in_specs=[rows(M_D_INNER), rows(M_CONV_DIM), rows(LANES), h0_spec, c0_spec,
                  par((M_CONV, M_CONV_DIM)), par((1, M_CONV_DIM)), par((1, LANES)), par((1, LANES)),
                  par((1, M_D_INNER)), par((1, M_D_INNER)), par((LANES, M_D_INNER))],
        out_specs=(pl.BlockSpec((lv, M_D_INNER), lambda b, c: (b * nc + c, 0)),
                   pl.BlockSpec((None, M_NHEADS, M_HEADDIM, M_DSTATE), lambda b, c: (b, 0, 0, 0)),
                   pl.BlockSpec((None, M_CONV - 1, M_CONV_DIM), lambda b, c: (b, 0, 0))),
        out_shape=(jax.ShapeDtypeStruct((out_rows, M_D_INNER), out_dtype),
                   jax.ShapeDtypeStruct((nb, M_NHEADS, M_HEADDIM, M_DSTATE), F32),
                   jax.ShapeDtypeStruct((nb, M_CONV - 1, M_CONV_DIM), F32)),
        scratch_shapes=[pltpu.VMEM((M_NHEADS // 2, M_DSTATE, LANES), F32),
                        pltpu.VMEM((8 + SSD_CHUNK, M_CONV_DIM), F32),
                        pltpu.VMEM((SSD_CHUNK, M_CONV_DIM), F32),
                        pltpu.VMEM((SSD_CHUNK, M_D_INNER), F32)],
        compiler_params=_cparams(2), name="ssd_mixer",
    )(z, xbc, dtr, h0, c0, *prm)


def _mamba_layer(h, hb, state_ssm, state_conv, p, layer):
    w_in = p['m_w_in']
    z = _matmul(hb, w_in, layer=layer, col0=0, n_out=M_D_INNER, tn=1024)
    xbc = _matmul(hb, w_in, layer=layer, col0=M_D_INNER, n_out=M_CONV_DIM, tn=1024)
    w_dt = jnp.pad(w_in[layer, :, M_D_INNER + M_CONV_DIM:], ((0, 0), (0, LANES - M_NHEADS)))
    dtr = _matmul(hb, w_dt, n_out=LANES, tn=LANES)

    def pad_heads(v):
        return jnp.pad(v[layer].astype(F32), (0, LANES - M_NHEADS)).reshape(1, LANES)

    head_of_col = jnp.arange(M_D_INNER, dtype=jnp.int32) // M_HEADDIM
    expand = (jnp.arange(LANES, dtype=jnp.int32)[:, None] == head_of_col[None, :]).astype(BF16)
    prm = (p['m_conv_w'][layer], p['m_conv_b'][layer].reshape(1, M_CONV_DIM),
           pad_heads(p['m_dt_bias']), pad_heads(p['m_a_log']),
           jnp.repeat(p['m_d'][layer].astype(F32), M_HEADDIM).reshape(1, M_D_INNER),
           p['m_norm_w'][layer].reshape(1, M_D_INNER), expand)
    zeros_h = jnp.zeros((BATCH, M_NHEADS, M_HEADDIM, M_DSTATE), F32)
    zeros_c = jnp.zeros((BATCH, M_CONV - 1, M_CONV_DIM), F32)
    y_m, h_m, c_m = _ssd(z, xbc, dtr, zeros_h, None, zeros_c, None, prm,
                         row0=ROW_META, lv=N_META, nb=BATCH, nc=1, out_dtype=BF16)
    y_x, h_x, c_x = _ssd(z, xbc, dtr, h_m, None, c_m, None, prm,
                         row0=0, lv=SSD_CHUNK, nb=BATCH, nc=SEQ // SSD_CHUNK, out_dtype=BF16)
    y_s, h_s, c_s = _ssd(z, xbc, dtr, state_ssm, layer, state_conv, layer, prm,
                         row0=ROW_SAMPLE, lv=DEC_SEQ, nb=DEC_BATCH, nc=1, out_dtype=F32)
    yn = jnp.concatenate([y_x, y_m, y_s.astype(BF16)], axis=0)
    mix = _matmul(yn, p['m_w_out'], layer=layer, n_out=D_MODEL, tn=512)
    h, hp = _ln_residual(h, mix, p['ln_g'], p['ln_b'], layer)
    return h, hp, (h_x, c_x, h_s, c_s)


def _lambda_value(lam_ref, lam_init):
    lp = lam_ref[...]
    s1 = jnp.sum(lp[0:1, :] * lp[1:2, :], axis=-1, keepdims=True)
    s2 = jnp.sum(lp[2:3, :] * lp[3:4, :], axis=-1, keepdims=True)
    return jnp.exp(s1) - jnp.exp(s2) + lam_init


def _subln(o, sw, lam_init):
    ms = jnp.mean(o * o, axis=-1, keepdims=True)
    return o * lax.rsqrt(ms + NORM_EPS) * sw * (1.0 - lam_init)


def _rep(x, n):
    return x if n == 1 else jnp.concatenate([x] * n, axis=1)


def _flash_body(q_ref, k_ref, v_ref, km_ref, vm_ref, lam_ref, sw_ref, o_ref,
                m_ref, l_ref, acc_ref, kb_ref, vb_ref, kpad_ref, vpad_ref, *, lam_init):
    t = FLASH_T
    dh = A_HEAD_DIM
    qi = pl.program_id(2)

    @pl.when(qi == 0)
    def _load_kv():
        kb_ref[...] = k_ref[...].astype(BF16)
        vb_ref[...] = v_ref[...].astype(BF16)
        kpad_ref[...] = jnp.zeros(kpad_ref.shape, BF16)
        vpad_ref[...] = jnp.zeros(vpad_ref.shape, BF16)
        kpad_ref[0:N_META, :] = km_ref[...].astype(BF16)
        vpad_ref[0:N_META, :] = vm_ref[...].astype(BF16)

    m_ref[...] = jnp.full(m_ref.shape, -jnp.inf, F32)
    l_ref[...] = jnp.zeros(l_ref.shape, F32)
    acc_ref[...] = jnp.zeros(acc_ref.shape, F32)
    qs = [q_ref[:, j * dh:(j + 1) * dh].astype(BF16) for j in range(2)]

    def scores(kb):
        return [_dot_nt(qs[j], kb[:, j * dh:(j + 1) * dh]) for j in range(2)]

    def update(s_heads, vb, mask):
        for j in range(2):
            s = s_heads[j]
            if mask is not None:
                s = jnp.where(mask, s, -jnp.inf)
            m_prev = m_ref[j]
            m_new = jnp.maximum(m_prev, jnp.max(s, axis=1, keepdims=True))
            alpha = jnp.exp(m_prev - m_new)
            pmat = jnp.exp(s - _rep(m_new, s.shape[1] // LANES))
            l_ref[j] = alpha * l_ref[j] + jnp.sum(pmat, axis=1, keepdims=True)
            acc_ref[j] = _rep(alpha, 2) * acc_ref[j] + _dot(pmat.astype(BF16), vb)
            m_ref[j] = m_new

    def kv_block(kj):
        off = pl.multiple_of(kj * t, t)
        return kb_ref[pl.ds(off, t), :], vb_ref[pl.ds(off, t), :]

    colm = lax.broadcasted_iota(jnp.int32, (t, LANES), 1)
    r = lax.broadcasted_iota(jnp.int32, (t, t), 0)
    cc = lax.broadcasted_iota(jnp.int32, (t, t), 1)
    k_diag, v_diag = kv_block(qi)
    s_meta = scores(kpad_ref[...])
    s_diag = scores(k_diag)
    update(s_meta, vpad_ref[...], colm < N_META)
    update(s_diag, v_diag, cc <= r)

    def two_blocks(i, carry):
        k_a, v_a = kv_block(2 * i)
        k_b, v_b = kv_block(2 * i + 1)
        s_a = scores(k_a)
        s_b = scores(k_b)
        update(s_a, v_a, None)
        update(s_b, v_b, None)
        return carry

    lax.fori_loop(0, qi // 2, two_blocks, 0)

    @pl.when(qi % 2 == 1)
    def _odd_block():
        k_a, v_a = kv_block(qi - 1)
        update(scores(k_a), v_a, None)

    lam = _lambda_value(lam_ref, lam_init)
    o1 = acc_ref[0] / _rep(l_ref[0], 2)
    o2 = acc_ref[1] / _rep(l_ref[1], 2)
    o_ref[...] = _subln(o1 - lam * o2, sw_ref[...], lam_init).astype(o_ref.dtype)


def _flash_prompt(q, k, v, lam_p, sw, j_layer, lam_init):
    t = FLASH_T
    nq = SEQ // t
    w2 = 2 * A_HEAD_DIM
    qspec = pl.BlockSpec((t, w2), lambda b, p, qi: (b * nq + qi, p))
    kspec = pl.BlockSpec((SEQ, w2), lambda b, p, qi: (b, p))
    mspec = pl.BlockSpec((N_META, w2), lambda b, p, qi: (ROW_META // N_META + b, p))
    return pl.pallas_call(
        functools.partial(_flash_body, lam_init=lam_init),
        grid=(BATCH, A_HEADS, nq),
        in_specs=[qspec, kspec, kspec, mspec, mspec,
                  pl.BlockSpec((None, 4, A_HEAD_DIM), lambda b, p, qi: (j_layer, 0, 0)),
                  pl.BlockSpec((None, 1, w2), lambda b, p, qi: (j_layer, 0, 0))],
        out_specs=pl.BlockSpec((t, w2), lambda b, p, qi: (b * nq + qi, p)),
        out_shape=jax.ShapeDtypeStruct((ROWS_X, D_MODEL), BF16),
        scratch_shapes=[pltpu.VMEM((2, t, LANES), F32), pltpu.VMEM((2, t, LANES), F32),
                        pltpu.VMEM((2, t, w2), F32),
                        pltpu.VMEM((SEQ, w2), BF16), pltpu.VMEM((SEQ, w2), BF16),
                        pltpu.VMEM((LANES, w2), BF16), pltpu.VMEM((LANES, w2), BF16)],
        compiler_params=_cparams(3), name="flash_prompt",
    )(q, k, v, k, v, lam_p, sw)


SATTN_PG = 4


SATTN_SLOTS = 3


def _sattn_body(pt_ref, q_ref, ck_hbm, cv_hbm, kn_ref, vn_ref, lam_ref, sw_ref, o_ref,
                qf_ref, qb_ref, m_ref, l_ref, acc_ref, kb_ref, vb_ref, kring, vring0, vring1, ring_sem,
                *, lq, nb, n_steps, pg, lam_init):
    dh = A_HEAD_DIM
    n_qk = 2 * A_HEADS
    rows = n_qk * lq
    j = pl.program_id(1)
    k_rows = PAGE_SIZE * n_qk
    v_rows = PAGE_SIZE * A_HEADS
    total = nb * n_steps

    def group_copies(t, slot):
        seq = t // n_steps
        first = (t % n_steps) * pg
        copies = []
        for g in range(pg):
            page = pt_ref[seq, first + g]
            copies.append(pltpu.make_async_copy(
                ck_hbm.at[page], kring.at[slot, pl.ds(g * k_rows, k_rows)], ring_sem.at[0, slot]))
            copies.append(pltpu.make_async_copy(
                cv_hbm.at[page, :, pl.ds(0, LANES)], vring0.at[slot, pl.ds(g * v_rows, v_rows)],
                ring_sem.at[1, slot]))
            copies.append(pltpu.make_async_copy(
                cv_hbm.at[page, :, pl.ds(LANES, LANES)], vring1.at[slot, pl.ds(g * v_rows, v_rows)],
                ring_sem.at[2, slot]))
        return copies

    def start_group(t):
        for cp in group_copies(t, t % SATTN_SLOTS):
            cp.start()

    @pl.when(j == 0)
    def _init():
        qf_ref[...] = jnp.zeros(qf_ref.shape, F32)
        for hd in range(n_qk):
            qf_ref[hd * lq:(hd + 1) * lq, hd * dh:(hd + 1) * dh] = q_ref[:, hd * dh:(hd + 1) * dh]
        qb_ref[...] = qf_ref[...].astype(BF16)
        m_ref[...] = jnp.full(m_ref.shape, -jnp.inf, F32)
        l_ref[...] = jnp.zeros(l_ref.shape, F32)
        acc_ref[...] = jnp.zeros(acc_ref.shape, F32)

    def process(tk, mask):
        s = _dot_nt(qb_ref[...], kb_ref[0:tk, :])
        if mask is not None:
            s = jnp.where(mask, s, -jnp.inf)
        m_prev = m_ref[...]
        m_new = jnp.maximum(m_prev, jnp.max(s, axis=1, keepdims=True))
        alpha = jnp.exp(m_prev - m_new)
        pmat = jnp.exp(s - _rep(m_new, tk // LANES))
        l_ref[...] = alpha * l_ref[...] + jnp.sum(pmat, axis=1, keepdims=True)
        m_ref[...] = m_new
        pb = pmat.astype(BF16)
        alpha2 = _rep(alpha, 2)
        for pp in range(A_HEADS):
            rs = slice(2 * pp * lq, (2 * pp + 2) * lq)
            acc_ref[rs, :] = alpha2[rs] * acc_ref[rs, :] + _dot(pb[rs], vb_ref[0:tk, pp * 2 * dh:(pp + 1) * 2 * dh])

    @pl.when(j < n_steps)
    def _past_pages():
        if n_steps == 0:
            return
        t = pl.program_id(0) * n_steps + j
        slot = t % SATTN_SLOTS

        @pl.when(t == 0)
        def _prime():
            start_group(0)
            if total > 1:
                start_group(1)

        @pl.when(t + 2 < total)
        def _prefetch():
            start_group(t + 2)

        for cp in group_copies(t, slot):
            cp.wait()
        vrings = (vring0, vring1)
        for g in range(pg):
            ts = slice(g * PAGE_SIZE, (g + 1) * PAGE_SIZE)
            for hd in range(n_qk):
                kb_ref[ts, hd * dh:(hd + 1) * dh] = (
                    kring[slot, pl.ds(g * k_rows + hd, PAGE_SIZE, stride=n_qk), :].astype(BF16))
            for hv in range(A_HEADS):
                for half in range(2):
                    c0 = hv * 2 * dh + half * LANES
                    vb_ref[ts, c0:c0 + LANES] = (
                        vrings[half][slot, pl.ds(g * v_rows + hv, PAGE_SIZE, stride=A_HEADS), :].astype(BF16))
        process(pg * PAGE_SIZE, None)

    @pl.when(j == n_steps)
    def _new_keys():
        pad = jnp.zeros((PAGE_SIZE - lq, D_MODEL), F32)
        kb_ref[0:PAGE_SIZE, :] = jnp.concatenate([kn_ref[...], pad], axis=0).astype(BF16)
        vb_ref[0:PAGE_SIZE, :] = jnp.concatenate([vn_ref[...], pad], axis=0).astype(BF16)
        r = lax.broadcasted_iota(jnp.int32, (rows, LANES), 0)
        cc = lax.broadcasted_iota(jnp.int32, (rows, LANES), 1)
        process(PAGE_SIZE, cc <= jnp.bitwise_and(r, lq - 1))
        lam = _lambda_value(lam_ref, lam_init)
        for pp in range(A_HEADS):
            r1 = slice(2 * pp * lq, (2 * pp + 1) * lq)
            r2 = slice((2 * pp + 1) * lq, (2 * pp + 2) * lq)
            o1 = acc_ref[r1, :] / _rep(l_ref[r1, :], 2)
            o2 = acc_ref[r2, :] / _rep(l_ref[r2, :], 2)
            o_ref[:, pp * 2 * dh:(pp + 1) * 2 * dh] = _subln(o1 - lam * o2, sw_ref[...], lam_init).astype(o_ref.dtype)


def _small_attn(q, k, v, cache_k, cache_v, page_table, lam_p, sw, j_layer, lam_init,
                *, row0, lq, nb, n_pages, out_dtype):
    rb = row0 // lq
    pg = SATTN_PG if n_pages > 0 else 1
    assert n_pages % pg == 0
    n_steps = n_pages // pg
    rows = 2 * A_HEADS * lq
    qspec = pl.BlockSpec((lq, D_MODEL), lambda b, j, pt: (rb + b, 0))

    n_pool = cache_k.shape[0]
    k_rows = PAGE_SIZE * 2 * A_HEADS
    v_rows = PAGE_SIZE * A_HEADS
    cache_k2 = cache_k.reshape(n_pool, k_rows, A_HEAD_DIM)
    cache_v2 = cache_v.reshape(n_pool, v_rows, 2 * A_HEAD_DIM)
    grid_spec = pltpu.PrefetchScalarGridSpec(
        num_scalar_prefetch=1,
        grid=(nb, n_steps + 1),
        in_specs=[qspec, pl.BlockSpec(memory_space=pl.ANY), pl.BlockSpec(memory_space=pl.ANY),
                  qspec, qspec,
                  pl.BlockSpec((None, 4, A_HEAD_DIM), lambda b, j, pt: (j_layer, 0, 0)),
                  pl.BlockSpec((None, 1, 2 * A_HEAD_DIM), lambda b, j, pt: (j_layer, 0, 0))],
        out_specs=pl.BlockSpec((lq, D_MODEL), lambda b, j, pt: (b, 0)),
        scratch_shapes=[pltpu.VMEM((rows, D_MODEL), F32), pltpu.VMEM((rows, D_MODEL), BF16),
                        pltpu.VMEM((rows, LANES), F32), pltpu.VMEM((rows, LANES), F32),
                        pltpu.VMEM((rows, 2 * A_HEAD_DIM), F32),
                        pltpu.VMEM((pg * PAGE_SIZE, D_MODEL), BF16),
                        pltpu.VMEM((pg * PAGE_SIZE, D_MODEL), BF16),
                        pltpu.VMEM((SATTN_SLOTS, pg * k_rows, LANES), F32),
                        pltpu.VMEM((SATTN_SLOTS, pg * v_rows, LANES), F32),
                        pltpu.VMEM((SATTN_SLOTS, pg * v_rows, LANES), F32),
                        pltpu.SemaphoreType.DMA((3, SATTN_SLOTS))],
    )
    return pl.pallas_call(
        functools.partial(_sattn_body, lq=lq, nb=nb, n_steps=n_steps, pg=pg, lam_init=lam_init),
        grid_spec=grid_spec,
        out_shape=jax.ShapeDtypeStruct((nb * lq, D_MODEL), out_dtype),
        compiler_params=_cparams(2), name="small_attn",
    )(page_table, q, cache_k2, cache_v2, k, v, lam_p, sw)


def _attn_layer(h, hb, k, v, rope_tabs, cache_k, cache_v, page_table, p, layer):
    j_layer = layer - N_A_LAYERS
    lam_init = 0.8 - 0.6 * math.exp(-0.3 * layer)
    q = _matmul(hb, p['a_w_q'], layer=j_layer, n_out=D_MODEL, tn=1024, rope=rope_tabs,
                scale=A_HEAD_DIM ** -0.5)
    lam_p = p['a_lambda'].astype(F32)
    sw = p['a_subln_w'].astype(F32).reshape(-1, 1, 2 * A_HEAD_DIM)
    o_x = _flash_prompt(q, k, v, lam_p, sw, j_layer, lam_init)
    dummy_pt = jnp.zeros((BATCH, 1), jnp.int32)
    o_m = _small_attn(q, k, v, cache_k, cache_v, dummy_pt, lam_p, sw, j_layer, lam_init,
                      row0=ROW_META, lq=N_META, nb=BATCH, n_pages=0, out_dtype=BF16)
    o_s = _small_attn(q, k, v, cache_k, cache_v, page_table, lam_p, sw, j_layer, lam_init,
                      row0=ROW_SAMPLE, lq=DEC_SEQ, nb=DEC_BATCH, n_pages=page_table.shape[1],
                      out_dtype=F32)
    o = jnp.concatenate([o_x, o_m, o_s.astype(BF16)], axis=0)
    mix = _matmul(o, p['a_w_o'], layer=j_layer, n_out=D_MODEL, tn=1024)
    return _ln_residual(h, mix, p['ln_g'], p['ln_b'], layer)


def _rope_tables():
    pos = jnp.concatenate([
        jnp.tile(N_META + jnp.arange(SEQ, dtype=jnp.int32), BATCH),
        jnp.tile(jnp.arange(N_META, dtype=jnp.int32), BATCH),
        jnp.tile(PAST_LEN + jnp.arange(DEC_SEQ, dtype=jnp.int32), DEC_BATCH)])
    half = A_HEAD_DIM // 2
    inv = ROPE_THETA ** (-jnp.arange(half, dtype=F32) / half)
    ang = pos.astype(F32)[:, None] * inv[None, :]
    cos = jnp.cos(ang)
    sin = jnp.sin(ang)
    return jnp.concatenate([cos, cos], axis=1), jnp.concatenate([-sin, sin], axis=1)


def kernel(x_prompt, x_sample, cache_k, cache_v, page_table, state_ssm, state_conv, meta_tokens, m_w_in, m_conv_w, m_conv_b, m_dt_bias, m_a_log, m_d, m_norm_w, m_w_out, a_w_q, a_w_k, a_w_v, a_lambda, a_subln_w, a_w_o, moe_w_router, moe_b_router, moe_w_gu, moe_b_gu, moe_w_down, moe_b_down, ln_g, ln_b):
    p = dict(m_w_in=m_w_in, m_conv_w=m_conv_w, m_conv_b=m_conv_b, m_dt_bias=m_dt_bias, m_a_log=m_a_log,
             m_d=m_d, m_norm_w=m_norm_w, m_w_out=m_w_out, a_w_q=a_w_q, a_w_k=a_w_k, a_w_v=a_w_v,
             a_lambda=a_lambda, a_subln_w=a_subln_w, a_w_o=a_w_o, moe_w_router=moe_w_router,
             moe_b_router=moe_b_router, moe_w_gu=moe_w_gu, moe_b_gu=moe_b_gu, moe_w_down=moe_w_down,
             moe_b_down=moe_b_down, ln_g=ln_g, ln_b=ln_b)
    assert page_table.shape[1] * PAGE_SIZE == PAST_LEN
    meta = jnp.broadcast_to(meta_tokens[None].astype(F32), (BATCH, N_META, D_MODEL))
    h = jnp.concatenate([x_prompt.reshape(ROWS_X, D_MODEL), meta.reshape(ROWS_META, D_MODEL),
                         x_sample.reshape(ROWS_SAMPLE, D_MODEL)], axis=0)
    hb = h.astype(BF16)
    rope_tabs = _rope_tables()

    ssm_p, conv_p, ssm_s, conv_s = [], [], [], []
    k = v = None
    for layer in range(DEPTH):
        if layer < N_A_LAYERS:
            h, hp, (h_x, c_x, h_s, c_s) = _mamba_layer(h, hb, state_ssm, state_conv, p, layer)
            ssm_p.append(h_x)
            conv_p.append(c_x)
            ssm_s.append(h_s)
            conv_s.append(c_s)
        else:
            if layer == N_A_LAYERS:
                k = _matmul(hb, a_w_k, n_out=D_MODEL, tn=1024, rope=rope_tabs)
                v = _matmul(hb, a_w_v, n_out=D_MODEL, tn=1024)
            h, hp = _attn_layer(h, hb, k, v, rope_tabs, cache_k, cache_v, page_table, p, layer)
        h, hb = _moe_layer(h, hp, p, layer, with_bf16=layer + 1 < DEPTH)

    def prompt_rows(a):
        return jnp.concatenate([a[ROW_META:ROW_SAMPLE].reshape(BATCH, N_META, D_MODEL),
                                a[:ROWS_X].reshape(BATCH, SEQ, D_MODEL)], axis=1)

    y_prompt = h[:ROWS_X].reshape(BATCH, SEQ, D_MODEL)
    y_sample = h[ROW_SAMPLE:].reshape(DEC_BATCH, DEC_SEQ, D_MODEL)
    k_prompt = prompt_rows(k).reshape(BATCH, SEQ + N_META, 2 * A_HEADS, A_HEAD_DIM)
    v_prompt = prompt_rows(v).reshape(BATCH, SEQ + N_META, A_HEADS, 2 * A_HEAD_DIM)
    k_sample = k[ROW_SAMPLE:].reshape(DEC_BATCH, DEC_SEQ, 2 * A_HEADS, A_HEAD_DIM)
    v_sample = v[ROW_SAMPLE:].reshape(DEC_BATCH, DEC_SEQ, A_HEADS, 2 * A_HEAD_DIM)
    return (y_prompt, y_sample, k_prompt, v_prompt, jnp.stack(ssm_p), jnp.stack(conv_p),
            k_sample, v_sample, jnp.stack(ssm_s), jnp.stack(conv_s))
```

```python
import functools
import math

import jax
import jax.numpy as jnp
from jax import lax
from jax.experimental import pallas as pl
from jax.experimental.pallas import tpu as pltpu

F32 = jnp.float32
BF16 = jnp.bfloat16

D_MODEL = 2048
BATCH = 4
SEQ = 2048
DEPTH = 4
DEC_BATCH = 8
DEC_SEQ = 8
PAGE_SIZE = 128
PAST_LEN = 16384
N_A_LAYERS = DEPTH // 2
N_META = 16
M_D_INNER = 2 * D_MODEL
M_HEADDIM = 64
M_NHEADS = M_D_INNER // M_HEADDIM
M_NGROUPS = 8
M_DSTATE = 128
M_CONV = 4
M_CONV_DIM = M_D_INNER + 2 * M_NGROUPS * M_DSTATE
A_HEADS = 8
A_HEAD_DIM = D_MODEL // (2 * A_HEADS)
ROPE_THETA = 10000.0
N_EXPERTS = 32
TOP_K = 4
D_FF = D_MODEL // 4
SWIGLU_LIMIT = 7.0
SWIGLU_ALPHA = 1.702
DN_ALPHA = (2 * DEPTH) ** 0.25
NORM_EPS = 1e-5

ROWS_X = BATCH * SEQ
ROW_META = ROWS_X
ROWS_META = BATCH * N_META
ROW_SAMPLE = ROW_META + ROWS_META
ROWS_SAMPLE = DEC_BATCH * DEC_SEQ
N_ROWS = ROW_SAMPLE + ROWS_SAMPLE

LANES = 128
VMEM_LIMIT = 56 * 1024 * 1024

SSD_CHUNK = 128
MOE_TM = 256
MOE_BLOCKS = N_ROWS * TOP_K // MOE_TM + N_EXPERTS
MOE_ROWS = MOE_BLOCKS * MOE_TM
MM_TM = 1040
ROUTE_TM = 640
LN_TR = 320
FLASH_T = 512


def _cparams(n_axes):
    return pltpu.CompilerParams(dimension_semantics=("arbitrary",) * n_axes,
                                vmem_limit_bytes=VMEM_LIMIT)


def _dot(a, b):
    return jnp.dot(a, b, preferred_element_type=F32)


def _dot_nt(a, b):
    return lax.dot_general(a, b, (((1,), (1,)), ((), ())), preferred_element_type=F32)


def _split2(x):
    hi = x.astype(BF16)
    lo = (x - hi.astype(F32)).astype(BF16)
    return hi, lo


def _split3(x):
    hi = x.astype(BF16)
    r = x - hi.astype(F32)
    mid = r.astype(BF16)
    lo = (r - mid.astype(F32)).astype(BF16)
    return hi, mid, lo


def _mm_body(*refs, rope, scale):
    if rope:
        x_ref, w_ref, cos_ref, sin_ref, o_ref, wbf_ref = refs
    else:
        x_ref, w_ref, o_ref, wbf_ref = refs

    @pl.when(pl.program_id(1) == 0)
    def _cast_weight():
        wbf_ref[...] = w_ref[...].astype(BF16)

    acc = _dot(x_ref[...].astype(BF16), wbf_ref[...])
    if rope:
        cos = cos_ref[...]
        sin = sin_ref[...]
        for j in range(acc.shape[1] // LANES):
            blk = acc[:, j * LANES:(j + 1) * LANES]
            out = (blk * cos + pltpu.roll(blk, LANES // 2, 1) * sin) * scale
            o_ref[:, j * LANES:(j + 1) * LANES] = out.astype(o_ref.dtype)
    else:
        o_ref[...] = acc.astype(o_ref.dtype)


def _matmul(x, w, *, layer=None, col0=0, n_out, tn, out_dtype=F32, rope=None, scale=1.0):
    m, k = x.shape
    tm = MM_TM
    assert m % tm == 0 and n_out % tn == 0 and col0 % tn == 0
    jb = col0 // tn
    if layer is None:
        w_spec = pl.BlockSpec((k, tn), lambda j, i: (0, j + jb))
    else:
        w_spec = pl.BlockSpec((None, k, tn), lambda j, i: (layer, 0, j + jb))
    in_specs = [pl.BlockSpec((tm, k), lambda j, i: (i, 0)), w_spec]
    args = [x, w]
    if rope is not None:
        in_specs += [pl.BlockSpec((tm, LANES), lambda j, i: (i, 0))] * 2
        args += list(rope)
    return pl.pallas_call(
        functools.partial(_mm_body, rope=rope is not None, scale=scale),
        grid=(n_out // tn, m // tm),
        in_specs=in_specs,
        out_specs=pl.BlockSpec((tm, tn), lambda j, i: (i, j)),
        out_shape=jax.ShapeDtypeStruct((m, n_out), out_dtype),
        scratch_shapes=[pltpu.VMEM((k, tn), BF16)],
        compiler_params=_cparams(2),
        name="dense_matmul",
    )(*args)


ROUTE_SUB = 128


def _lane_pack(cols, lane):
    out = jnp.zeros(lane.shape, cols[0].dtype)
    for k, c in enumerate(cols):
        out = jnp.where(lane == k, c, out)
    return out


def _route_body(x_ref, w_ref, b_ref, idx_ref, gate_ref, rank_ref, cnt_ref, run_ref):
    tm = x_ref.shape[0]

    @pl.when(pl.program_id(0) == 0)
    def _init():
        run_ref[...] = jnp.zeros(run_ref.shape, F32)

    xh, xl = _split2(x_ref[...])
    wh, wl = _split2(w_ref[...])
    logits = _dot(xh, wh) + _dot(xh, wl) + _dot(xl, wh) + b_ref[...]
    lane = lax.broadcasted_iota(jnp.int32, (ROUTE_SUB, LANES), 1)
    r_i = lax.broadcasted_iota(jnp.int32, (ROUTE_SUB, ROUTE_SUB), 0)
    c_i = lax.broadcasted_iota(jnp.int32, (ROUTE_SUB, ROUTE_SUB), 1)
    below = jnp.where(r_i > c_i, 1.0, 0.0).astype(BF16)
    running = run_ref[0:1, :]
    for sb in range(tm // ROUTE_SUB):
        rs = slice(sb * ROUTE_SUB, (sb + 1) * ROUTE_SUB)
        rest = logits[rs]
        vals, idxs, hots = [], [], []
        for _ in range(TOP_K):
            mx = jnp.max(rest, axis=1, keepdims=True)
            idx = jnp.min(jnp.where(rest == mx, lane, LANES), axis=1, keepdims=True)
            hot = lane == idx
            rest = jnp.where(hot, -jnp.inf, rest)
            vals.append(mx)
            idxs.append(idx)
            hots.append(hot)
        exps = [jnp.exp(v - vals[0]) for v in vals]
        denom = exps[0] + exps[1] + exps[2] + exps[3]
        idx_ref[rs, :] = _lane_pack(idxs, lane)
        gate_ref[rs, :] = _lane_pack([e / denom for e in exps], lane)
        sel = jnp.zeros((ROUTE_SUB, LANES), F32)
        for hot in hots:
            sel = sel + jnp.where(hot, 1.0, 0.0)
        before = _dot(below, sel.astype(BF16)) + running
        ranks = [jnp.sum(jnp.where(hot, before, 0.0), axis=1, keepdims=True) for hot in hots]
        rank_ref[rs, :] = _lane_pack(ranks, lane).astype(jnp.int32)
        running = running + jnp.sum(sel, axis=0, keepdims=True)
    run_ref[...] = jnp.broadcast_to(running, run_ref.shape)
    cnt_ref[...] = jnp.broadcast_to(running, cnt_ref.shape)


def _route(h, w_router, b_router, layer):
    tm = ROUTE_TM
    w_pad = jnp.pad(w_router, ((0, 0), (0, 0), (0, LANES - N_EXPERTS)))
    b_pad = jnp.pad(b_router.astype(F32), ((0, 0), (0, LANES - N_EXPERTS)),
                    constant_values=-jnp.inf).reshape(DEPTH, 1, LANES)
    row = pl.BlockSpec((tm, LANES), lambda i: (i, 0))
    return pl.pallas_call(
        _route_body,
        grid=(N_ROWS // tm,),
        in_specs=[pl.BlockSpec((tm, D_MODEL), lambda i: (i, 0)),
                  pl.BlockSpec((None, D_MODEL, LANES), lambda i: (layer, 0, 0)),
                  pl.BlockSpec((None, 1, LANES), lambda i: (layer, 0, 0))],
        out_specs=(row, row, row, pl.BlockSpec((8, LANES), lambda i: (0, 0))),
        out_shape=(jax.ShapeDtypeStruct((N_ROWS, LANES), jnp.int32),
                   jax.ShapeDtypeStruct((N_ROWS, LANES), F32),
                   jax.ShapeDtypeStruct((N_ROWS, LANES), jnp.int32),
                   jax.ShapeDtypeStruct((8, LANES), F32)),
        scratch_shapes=[pltpu.VMEM((8, LANES), F32)],
        compiler_params=_cparams(1),
        name="moe_route",
    )(h, w_pad, b_pad)


def _layer_norm_rows(x, g, b):
    mu = jnp.mean(x, axis=-1, keepdims=True)
    xc = x - mu
    var = jnp.mean(xc * xc, axis=-1, keepdims=True)
    return xc * lax.rsqrt(var + NORM_EPS) * g + b


PACK_W = D_MODEL // 2
PACK_T = PACK_W // LANES
ROW_T = D_MODEL // LANES


def _pack_rows(y, out_ref):
    rows = y.shape[0]
    bits = pltpu.bitcast(y.astype(BF16).astype(F32), jnp.uint32)
    packed = bits[:, :PACK_W] | lax.shift_right_logical(bits[:, PACK_W:], jnp.uint32(16))
    for c in range(PACK_T):
        out_ref[pl.ds(c, rows, stride=PACK_T), :] = packed[:, c * LANES:(c + 1) * LANES]


def _unpack_rows(x_ref, out_ref, rows):
    for c in range(PACK_T):
        w = x_ref[pl.ds(c, rows, stride=PACK_T), :]
        hi = pltpu.bitcast(w & jnp.uint32(0xFFFF0000), F32)
        lo = pltpu.bitcast(lax.shift_left(w, jnp.uint32(16)), F32)
        out_ref[:, c * LANES:(c + 1) * LANES] = hi.astype(BF16)
        out_ref[:, PACK_W + c * LANES:PACK_W + (c + 1) * LANES] = lo.astype(BF16)


def _ln_body(h_ref, m_ref, g_ref, b_ref, o_ref, op_ref):
    y = _layer_norm_rows(DN_ALPHA * h_ref[...] + m_ref[...], g_ref[...], b_ref[...])
    o_ref[...] = y
    _pack_rows(y, op_ref)


def _ln_residual(h, mix, ln_g, ln_b, layer):
    tr = LN_TR
    row = pl.BlockSpec((tr, D_MODEL), lambda i: (i, 0))
    par = pl.BlockSpec((None, None, 1, D_MODEL), lambda i: (layer, 0, 0, 0))
    return pl.pallas_call(
        _ln_body, grid=(N_ROWS // tr,),
        in_specs=[row, row, par, par],
        out_specs=(row, pl.BlockSpec((tr * PACK_T, LANES), lambda i: (i, 0))),
        out_shape=(jax.ShapeDtypeStruct((N_ROWS, D_MODEL), F32),
                   jax.ShapeDtypeStruct((N_ROWS * PACK_T, LANES), jnp.uint32)),
        compiler_params=_cparams(1), name="ln_residual",
    )(h, mix, ln_g.reshape(DEPTH, 2, 1, D_MODEL), ln_b.reshape(DEPTH, 2, 1, D_MODEL))


MOE_TB = 128
MOE_NBLK = N_ROWS // MOE_TB
MOE_IDX = MOE_TB * TOP_K


def _idx_copy(dest_hbm, idx_smem, isem, blk, slot):
    return pltpu.make_async_copy(dest_hbm.at[blk], idx_smem.at[slot], isem.at[slot])


def _tile_rows(row, n):
    return pl.ds(pl.multiple_of(row * n, n), n)


def _dispatch_body(pstart_ref, pcnt_ref, nused_ref, dest_hbm, h_hbm, xs_hbm, idx_smem, zero_ref, hbuf,
                   isem, lsem, rsem, psem, tsem):
    i = pl.program_id(0)
    slot = i % 2
    blk_rows = MOE_TM * PACK_T
    step_rows = MOE_TB * PACK_T

    def load_tokens(blk, s):
        return pltpu.make_async_copy(h_hbm.at[_tile_rows(blk, step_rows)], hbuf.at[s], lsem.at[s])

    @pl.when(i == 0)
    def _first():
        _idx_copy(dest_hbm, idx_smem, isem, 0, 0).start()
        load_tokens(0, 0).start()
        load_tokens(1, 1).start()
        zero_ref[...] = jnp.zeros(zero_ref.shape, jnp.uint32)

        def tail_copy(blk):
            return pltpu.make_async_copy(zero_ref, xs_hbm.at[_tile_rows(blk, blk_rows)], tsem)

        def tail_start(blk, carry):
            tail_copy(blk).start()
            return carry

        def tail_wait(blk, carry):
            tail_copy(blk).wait()
            return carry

        lax.fori_loop(nused_ref[0], MOE_BLOCKS, tail_start, 0)
        total = jnp.int32(0)
        for e in range(N_EXPERTS):
            first = pstart_ref[e]
            cnt = pcnt_ref[e]

            def pad_row(j, carry, first=first):
                pltpu.make_async_copy(zero_ref.at[pl.ds(0, PACK_T)], xs_hbm.at[_tile_rows(first + j, PACK_T)],
                                      psem).start()
                return carry

            lax.fori_loop(0, cnt, pad_row, 0)
            total = total + cnt

        def wait_pad(j, carry):
            pltpu.make_async_copy(zero_ref.at[pl.ds(0, PACK_T)], xs_hbm.at[pl.ds(0, PACK_T)], psem).wait()
            return carry

        lax.fori_loop(0, total, wait_pad, 0)
        lax.fori_loop(nused_ref[0], MOE_BLOCKS, tail_wait, 0)

    _idx_copy(dest_hbm, idx_smem, isem, i, slot).wait()

    @pl.when(i + 1 < MOE_NBLK)
    def _next_idx():
        _idx_copy(dest_hbm, idx_smem, isem, i + 1, 1 - slot).start()

    hslot = i % 3
    load_tokens(i, hslot).wait()
    for r in range(MOE_TB):
        for k in range(TOP_K):
            d = idx_smem[slot, r * TOP_K + k]
            pltpu.make_async_copy(hbuf.at[hslot, pl.ds(r * PACK_T, PACK_T)], xs_hbm.at[_tile_rows(d, PACK_T)],
                                  rsem.at[slot]).start(priority=k % 2)

    def wait_tiles(s):
        n_rows = MOE_IDX * PACK_T
        pltpu.make_async_copy(xs_hbm.at[pl.ds(0, n_rows)], xs_hbm.at[pl.ds(0, n_rows)], rsem.at[s]).wait()

    @pl.when(i > 0)
    def _wait_previous():
        wait_tiles(1 - slot)

    @pl.when(i + 2 < MOE_NBLK)
    def _load_ahead():
        load_tokens(i + 2, (i + 2) % 3).start()

    @pl.when(i == MOE_NBLK - 1)
    def _wait_last():
        wait_tiles(slot)


def _dispatch(hp, dest2, pad_start, pad_cnt, n_used):
    grid_spec = pltpu.PrefetchScalarGridSpec(
        num_scalar_prefetch=3,
        grid=(MOE_NBLK,),
        in_specs=[pl.BlockSpec(memory_space=pl.ANY), pl.BlockSpec(memory_space=pl.ANY)],
        out_specs=pl.BlockSpec(memory_space=pl.ANY),
        scratch_shapes=[pltpu.SMEM((2, MOE_IDX), jnp.int32),
                        pltpu.VMEM((MOE_TM * PACK_T, LANES), jnp.uint32),
                        pltpu.VMEM((3, MOE_TB * PACK_T, LANES), jnp.uint32),
                        pltpu.SemaphoreType.DMA((2,)), pltpu.SemaphoreType.DMA((3,)),
                        pltpu.SemaphoreType.DMA((2,)),
                        pltpu.SemaphoreType.DMA(()), pltpu.SemaphoreType.DMA(())],
    )
    return pl.pallas_call(
        _dispatch_body, grid_spec=grid_spec,
        out_shape=jax.ShapeDtypeStruct((MOE_ROWS * PACK_T, LANES), jnp.uint32),
        compiler_params=_cparams(1), name="moe_dispatch",
    )(pad_start, pad_cnt, n_used, dest2, hp)


def _combine_body(dest_hbm, eo_hbm, h_ref, gate_ref, g_ref, b_ref, *rest, with_bf16):
    if with_bf16:
        o_ref, obf_ref, idx_smem, ebuf, xbuf, isem, rsem = rest
    else:
        o_ref, idx_smem, ebuf, xbuf, isem, rsem = rest
    i = pl.program_id(0)
    slot = i % 2

    def issue_rows(s):
        for r in range(MOE_TB):
            for k in range(TOP_K):
                d = idx_smem[s, r * TOP_K + k]
                pltpu.make_async_copy(eo_hbm.at[_tile_rows(d, ROW_T)], ebuf.at[s, k, pl.ds(r * ROW_T, ROW_T)],
                                      rsem.at[s]).start(priority=k % 2)

    @pl.when(i == 0)
    def _prime():
        first = _idx_copy(dest_hbm, idx_smem, isem, 0, 0)
        first.start()
        first.wait()
        issue_rows(0)
        _idx_copy(dest_hbm, idx_smem, isem, 1, 1).start()

    @pl.when(i + 1 < MOE_NBLK)
    def _prefetch():
        _idx_copy(dest_hbm, idx_smem, isem, i + 1, 1 - slot).wait()
        issue_rows(1 - slot)

        @pl.when(i + 2 < MOE_NBLK)
        def _next_idx():
            _idx_copy(dest_hbm, idx_smem, isem, i + 2, slot).start()

    pltpu.make_async_copy(ebuf.at[slot], ebuf.at[slot], rsem.at[slot]).wait()
    gates = gate_ref[...]
    for c in range(ROW_T):
        cs_ = slice(c * LANES, (c + 1) * LANES)
        f = ebuf[slot, 0, pl.ds(c, MOE_TB, stride=ROW_T), :] * gates[:, 0:1]
        for k in range(1, TOP_K):
            f = f + ebuf[slot, k, pl.ds(c, MOE_TB, stride=ROW_T), :] * gates[:, k:k + 1]
        xbuf[:, cs_] = DN_ALPHA * h_ref[:, cs_] + f
    y = _layer_norm_rows(xbuf[...], g_ref[...], b_ref[...])
    o_ref[...] = y
    if with_bf16:
        obf_ref[...] = y.astype(BF16)


def _ln_moe(h, eo, dest2, gates, ln_g, ln_b, layer, with_bf16):
    row = pl.BlockSpec((MOE_TB, D_MODEL), lambda i: (i, 0))
    par = pl.BlockSpec((None, None, 1, D_MODEL), lambda i: (layer, 1, 0, 0))
    out_shape = [jax.ShapeDtypeStruct((N_ROWS, D_MODEL), F32)]
    if with_bf16:
        out_shape.append(jax.ShapeDtypeStruct((N_ROWS, D_MODEL), BF16))
    outs = pl.pallas_call(
        functools.partial(_combine_body, with_bf16=with_bf16),
        grid=(MOE_NBLK,),
        in_specs=[pl.BlockSpec(memory_space=pl.ANY), pl.BlockSpec(memory_space=pl.ANY), row,
                  pl.BlockSpec((MOE_TB, LANES), lambda i: (i, 0)), par, par],
        out_specs=tuple([row] * len(out_shape)),
        out_shape=tuple(out_shape),
        scratch_shapes=[pltpu.SMEM((2, MOE_IDX), jnp.int32),
                        pltpu.VMEM((2, TOP_K, MOE_TB * ROW_T, LANES), F32),
                        pltpu.VMEM((MOE_TB, D_MODEL), F32),
                        pltpu.SemaphoreType.DMA((2,)), pltpu.SemaphoreType.DMA((2,))],
        compiler_params=_cparams(1), name="ln_moe_combine",
    )(dest2, eo, h, gates, ln_g.reshape(DEPTH, 2, 1, D_MODEL), ln_b.reshape(DEPTH, 2, 1, D_MODEL))
    return outs if with_bf16 else (outs[0], None)


def _moe_body(be_ref, par_ref, nxt_ref, nb_ref, x_ref, wgu_hbm, bgu_ref, wd_hbm, bd_ref, o_ref,
              wgu_f32, wd_f32, wgu_bf, wd_bf, xb_ref, wsem, *, layer):
    i = pl.program_id(0)
    e = be_ref[i]
    prev = be_ref[jnp.maximum(i - 1, 0)]
    slot = par_ref[i]
    used = i < nb_ref[0]

    def fetch(expert, s):
        return (pltpu.make_async_copy(wgu_hbm.at[layer, expert], wgu_f32.at[s], wsem.at[0, s]),
                pltpu.make_async_copy(wd_hbm.at[layer, expert], wd_f32.at[s], wsem.at[1, s]))

    @pl.when(i == 0)
    def _cold_start():
        for cp in fetch(e, 0):
            cp.start()

    @pl.when(jnp.logical_and(used, jnp.logical_or(i == 0, e != prev)))
    def _switch_expert():
        for cp in fetch(e, slot):
            cp.wait()
        nxt = nxt_ref[i]

        @pl.when(nxt >= 0)
        def _prefetch_next():
            for cp in fetch(nxt, 1 - slot):
                cp.start()

        wgu_bf[...] = wgu_f32[slot].astype(BF16)
        wd_bf[...] = wd_f32[slot].astype(BF16)

    @pl.when(used)
    def _compute():
        _unpack_rows(x_ref, xb_ref, MOE_TM)
        gu = _dot(xb_ref[...], wgu_bf[...]) + bgu_ref[...]
        gate = jnp.minimum(gu[:, :D_FF], SWIGLU_LIMIT)
        up = jnp.clip(gu[:, D_FF:], -SWIGLU_LIMIT, SWIGLU_LIMIT)
        act = (up + 1.0) * gate * jax.nn.sigmoid(SWIGLU_ALPHA * gate)
        out = _dot(act.astype(BF16), wd_bf[...]) + bd_ref[...]
        for c in range(ROW_T):
            o_ref[pl.ds(c, MOE_TM, stride=ROW_T), :] = out[:, c * LANES:(c + 1) * LANES]

    @pl.when(jnp.logical_not(used))
    def _unused_block():
        o_ref[...] = jnp.zeros(o_ref.shape, o_ref.dtype)


def _moe_experts(xs, blk_e, blk_par, blk_nxt, n_used, w_gu, b_gu, w_down, b_down, layer):
    grid_spec = pltpu.PrefetchScalarGridSpec(
        num_scalar_prefetch=4,
        grid=(MOE_BLOCKS,),
        in_specs=[
            pl.BlockSpec((MOE_TM * PACK_T, LANES), lambda i, be, pa, nx, nb: (jnp.minimum(i, nb[0] - 1), 0)),
            pl.BlockSpec(memory_space=pl.ANY),
            pl.BlockSpec((None, None, 1, 2 * D_FF), lambda i, be, pa, nx, nb: (layer, be[i], 0, 0)),
            pl.BlockSpec(memory_space=pl.ANY),
            pl.BlockSpec((None, None, 1, D_MODEL), lambda i, be, pa, nx, nb: (layer, be[i], 0, 0)),
        ],
        out_specs=pl.BlockSpec((MOE_TM * ROW_T, LANES), lambda i, be, pa, nx, nb: (i, 0)),
        scratch_shapes=[pltpu.VMEM((2, D_MODEL, 2 * D_FF), F32), pltpu.VMEM((2, D_FF, D_MODEL), F32),
                        pltpu.VMEM((D_MODEL, 2 * D_FF), BF16), pltpu.VMEM((D_FF, D_MODEL), BF16),
                        pltpu.VMEM((MOE_TM, D_MODEL), BF16),
                        pltpu.SemaphoreType.DMA((2, 2))],
    )
    return pl.pallas_call(
        functools.partial(_moe_body, layer=layer), grid_spec=grid_spec,
        out_shape=jax.ShapeDtypeStruct((MOE_ROWS * ROW_T, LANES), F32),
        compiler_params=_cparams(1), name="moe_experts",
    )(blk_e, blk_par, blk_nxt, n_used, xs, w_gu, b_gu.reshape(DEPTH, N_EXPERTS, 1, 2 * D_FF),
      w_down, b_down.reshape(DEPTH, N_EXPERTS, 1, D_MODEL))


def _moe_layer(h, hp, p, layer, with_bf16):
    idx, gates, rank, cnt = _route(h, p['moe_w_router'], p['moe_b_router'], layer)
    counts = cnt[0, :N_EXPERTS].astype(jnp.int32)
    psz = (counts + MOE_TM - 1) // MOE_TM * MOE_TM
    pend = jnp.cumsum(psz)
    pstart = pend - psz
    expert_ids = jnp.arange(N_EXPERTS, dtype=jnp.int32)
    top_i = idx[:, :TOP_K]
    base = jnp.sum(jnp.where(top_i[:, :, None] == expert_ids, pstart, 0), axis=-1)
    dest2 = (base + rank[:, :TOP_K]).astype(jnp.int32).reshape(MOE_NBLK, MOE_IDX)
    n_used = (pend[-1] // MOE_TM).astype(jnp.int32)
    blk = jnp.arange(MOE_BLOCKS, dtype=jnp.int32)
    blk_e = jnp.sum((blk[:, None] * MOE_TM >= pend[None, :]).astype(jnp.int32), axis=1)
    blk_e = jnp.minimum(blk_e, N_EXPERTS - 1)
    blk_e = jnp.where(blk < n_used, blk_e, blk_e[jnp.maximum(n_used - 1, 0)]).astype(jnp.int32)
    used = counts > 0
    ordinal = jnp.cumsum(used.astype(jnp.int32)) - 1
    cand = jnp.where(used, expert_ids, N_EXPERTS)
    later = jnp.concatenate([lax.cummin(cand[::-1])[::-1][1:], jnp.full((1,), N_EXPERTS, jnp.int32)])
    nxt_e = jnp.where(later >= N_EXPERTS, -1, later)
    blk_hot = blk_e[:, None] == expert_ids
    blk_par = jnp.sum(jnp.where(blk_hot, ordinal % 2, 0), axis=1).astype(jnp.int32)
    blk_nxt = jnp.sum(jnp.where(blk_hot, nxt_e, 0), axis=1).astype(jnp.int32)
    n_used = n_used.reshape(1)
    xs = _dispatch(hp, dest2, (pstart + counts).astype(jnp.int32), (psz - counts).astype(jnp.int32), n_used)
    eo = _moe_experts(xs, blk_e, blk_par, blk_nxt, n_used, p['moe_w_gu'], p['moe_b_gu'],
                      p['moe_w_down'], p['moe_b_down'], layer)
    return _ln_moe(h, eo, dest2, gates, p['ln_g'], p['ln_b'], layer, with_bf16)


def _ssd_body(z_ref, xbc_ref, dt_ref, h0_ref, c0_ref, cw_ref, cb_ref, dtb_ref, alog_ref, de_ref, nw_ref,
              eexp_ref, y_ref, hfin_ref, cfin_ref, ht_ref, ext_ref, act_ref, ybuf_ref, *, lv, nc):
    L = SSD_CHUNK
    N = M_DSTATE
    c = pl.program_id(1)
    n_pairs = M_NHEADS // 2
    x_cols = M_D_INNER
    b_col0 = M_D_INNER
    c_col0 = M_D_INNER + M_NGROUPS * M_DSTATE

    @pl.when(c == 0)
    def _init():
        for i in range(n_pairs):
            ht_ref[i] = h0_ref[2 * i:2 * i + 2].reshape(2 * M_HEADDIM, N).T
        ext_ref[0:8, :] = jnp.zeros((8, M_CONV_DIM), F32)
        ext_ref[5:8, :] = c0_ref[...]

    raw = xbc_ref[...]
    if lv < L:
        raw = jnp.concatenate([raw, jnp.zeros((L - lv, M_CONV_DIM), F32)], axis=0)
    ext_ref[8:8 + L, :] = raw
    cw_step = 256
    for j in range(M_CONV_DIM // cw_step):
        cs_ = slice(j * cw_step, (j + 1) * cw_step)
        ext = ext_ref[:, cs_]
        acc = cb_ref[:, cs_] + cw_ref[3:4, cs_] * ext[8:]
        for back in range(1, M_CONV):
            acc = acc + cw_ref[M_CONV - 1 - back:M_CONV - back, cs_] * pltpu.roll(ext, back, 0)[8:]
        act_ref[:, cs_] = jax.nn.silu(acc)
    carry = ext_ref[5 + lv:8 + lv, :]
    ext_ref[5:8, :] = carry
    cfin_ref[...] = carry

    dtr = dt_ref[...]
    if lv < L:
        dtr = jnp.concatenate([dtr, jnp.zeros((L - lv, LANES), F32)], axis=0)
    dt = jax.nn.softplus(dtr + dtb_ref[...])
    row = lax.broadcasted_iota(jnp.int32, (L, L), 0)
    col = lax.broadcasted_iota(jnp.int32, (L, L), 1)
    if lv < L:
        dt = jnp.where(row < lv, dt, 0.0)
    da = dt * (-jnp.exp(alog_ref[...]))
    causal = row >= col
    tri = jnp.where(causal, 1.0, 0.0).astype(BF16)
    d_hi, d_mid, d_lo = _split3(da)
    cs = _dot(tri, d_hi) + _dot(tri, d_mid) + _dot(tri, d_lo)
    cs_end = cs[L - 1:L, :]
    cst = cs.T
    dtt = dt.T
    wt = (dt * jnp.exp(cs_end - cs)).T
    e_hi, e_lo = _split2(jnp.broadcast_to(jnp.exp(cs_end), (8, LANES)))
    dend = _dot(e_hi, eexp_ref[...]) + _dot(e_lo, eexp_ref[...])
    lane = lax.broadcasted_iota(jnp.int32, (L, LANES), 1)
    first_head = lane < M_HEADDIM

    for g in range(M_NGROUPS):
        bg = act_ref[:, b_col0 + g * N:b_col0 + (g + 1) * N]
        cg = act_ref[:, c_col0 + g * N:c_col0 + (g + 1) * N]
        btg = bg.T
        cb = _dot(cg.astype(BF16), btg.astype(BF16))
        for pr in range(M_NHEADS // M_NGROUPS // 2):
            i = g * (M_NHEADS // M_NGROUPS // 2) + pr
            ps = slice(i * LANES, (i + 1) * LANES)
            xs_pair = act_ref[:, ps]
            ht_pair = ht_ref[i]
            lhs_rows = []
            bts_rows = []
            for j in range(2):
                hd = 2 * i + j
                colb = jnp.broadcast_to(cs[:, hd:hd + 1], (L, L))
                seg = colb - cst[hd:hd + 1, :]
                decay = jnp.exp(jnp.where(causal, seg, -jnp.inf))
                m_h = cb * decay * dtt[hd:hd + 1, :]
                c_h = cg * jnp.exp(colb)
                lhs_rows.append(jnp.concatenate([m_h, c_h], axis=1))
                bts_rows.append(btg * wt[hd:hd + 1, :])
            lhs = jnp.concatenate(lhs_rows, axis=0).astype(BF16)
            rhs = jnp.concatenate([xs_pair, ht_pair], axis=0).astype(BF16)
            out = _dot(lhs, rhs)
            y_pair = jnp.where(first_head, out[:L], out[L:]) + xs_pair * de_ref[:, ps]
            ybuf_ref[:, ps] = y_pair
            bts = jnp.concatenate(bts_rows, axis=0).astype(BF16)
            upd = _dot(bts, xs_pair.astype(BF16))
            ht_ref[i] = dend[0:1, ps] * ht_pair + jnp.where(first_head, upd[:N], upd[N:])

    gw = M_D_INNER // M_NGROUPS
    for g in range(M_NGROUPS):
        gs = slice(g * gw, (g + 1) * gw)
        yg = ybuf_ref[0:lv, gs] * jax.nn.silu(z_ref[:, gs])
        ms = jnp.mean(yg * yg, axis=-1, keepdims=True)
        y_ref[:, gs] = (yg * lax.rsqrt(ms + NORM_EPS) * nw_ref[:, gs]).astype(y_ref.dtype)

    @pl.when(c == nc - 1)
    def _final():
        for i in range(n_pairs):
            hfin_ref[2 * i:2 * i + 2] = ht_ref[i].T.reshape(2, M_HEADDIM, N)


def _ssd(z, xbc, dtr, h0, h0_layer, c0, c0_layer, prm, *, row0, lv, nb, nc, out_dtype):
    rb = row0 // lv

    def rows(width):
        return pl.BlockSpec((lv, width), lambda b, c: (rb + b * nc + c, 0))

    def par(shape):
        return pl.BlockSpec(shape, lambda b, c: (0, 0))

    if h0_layer is None:
        h0_spec = pl.BlockSpec((None, M_NHEADS, M_HEADDIM, M_DSTATE), lambda b, c: (b, 0, 0, 0))
        c0_spec = pl.BlockSpec((None, M_CONV - 1, M_CONV_DIM), lambda b, c: (b, 0, 0))
    else:
        h0_spec = pl.BlockSpec((None, None, M_NHEADS, M_HEADDIM, M_DSTATE),
                               lambda b, c: (h0_layer, b, 0, 0, 0))
        c0_spec = pl.BlockSpec((None, None, M_CONV - 1, M_CONV_DIM), lambda b, c: (c0_layer, b, 0, 0))
    out_rows = nb * nc * lv
    return pl.pallas_call(
        functools.partial(_ssd_body, lv=lv, nc=nc),
        grid=(nb, nc),
        in_specs=[rows(M_D_INNER), rows(M_CONV_DIM), rows(LANES), h0_spec, c0_spec,
                  par((M_CONV, M_CONV_DIM)), par((1, M_CONV_DIM)), par((1, LANES)), par((1, LANES)),
                  par((1, M_D_INNER)), par((1, M_D_INNER)), par((LANES, M_D_INNER))],
        out_specs=(pl.BlockSpec((lv, M_D_INNER), lambda b, c: (b * nc + c, 0)),
                   pl.BlockSpec((None, M_NHEADS, M_HEADDIM, M_DSTATE), lambda b, c: (b, 0, 0, 0)),
                   pl.BlockSpec((None, M_CONV - 1, M_CONV_DIM), lambda b, c: (b, 0, 0))),
        out_shape=(jax.ShapeDtypeStruct((out_rows, M_D_INNER), out_dtype),
                   jax.ShapeDtypeStruct((nb, M_NHEADS, M_HEADDIM, M_DSTATE), F32),
                   jax.ShapeDtypeStruct((nb, M_CONV - 1, M_CONV_DIM), F32)),
        scratch_shapes=[pltpu.VMEM((M_NHEADS // 2, M_DSTATE, LANES), F32),
                        pltpu.VMEM((8 + SSD_CHUNK, M_CONV_DIM), F32),
                        pltpu.VMEM((SSD_CHUNK, M_CONV_DIM), F32),
                        pltpu.VMEM((SSD_CHUNK, M_D_INNER), F32)],
        compiler_params=_cparams(2), name="ssd_mixer",
    )(z, xbc, dtr, h0, c0, *prm)


def _mamba_layer(h, hb, state_ssm, state_conv, p, layer):
    w_in = p['m_w_in']
    z = _matmul(hb, w_in, layer=layer, col0=0, n_out=M_D_INNER, tn=1024)
    xbc = _matmul(hb, w_in, layer=layer, col0=M_D_INNER, n_out=M_CONV_DIM, tn=1024)
    w_dt = jnp.pad(w_in[layer, :, M_D_INNER + M_CONV_DIM:], ((0, 0), (0, LANES - M_NHEADS)))
    dtr = _matmul(hb, w_dt, n_out=LANES, tn=LANES)

    def pad_heads(v):
        return jnp.pad(v[layer].astype(F32), (0, LANES - M_NHEADS)).reshape(1, LANES)

    head_of_col = jnp.arange(M_D_INNER, dtype=jnp.int32) // M_HEADDIM
    expand = (jnp.arange(LANES, dtype=jnp.int32)[:, None] == head_of_col[None, :]).astype(BF16)
    prm = (p['m_conv_w'][layer], p['m_conv_b'][layer].reshape(1, M_CONV_DIM),
           pad_heads(p['m_dt_bias']), pad_heads(p['m_a_log']),
           jnp.repeat(p['m_d'][layer].astype(F32), M_HEADDIM).reshape(1, M_D_INNER),
           p['m_norm_w'][layer].reshape(1, M_D_INNER), expand)
    zeros_h = jnp.zeros((BATCH, M_NHEADS, M_HEADDIM, M_DSTATE), F32)
    zeros_c = jnp.zeros((BATCH, M_CONV - 1, M_CONV_DIM), F32)
    y_m, h_m, c_m = _ssd(z, xbc, dtr, zeros_h, None, zeros_c, None, prm,
                         row0=ROW_META, lv=N_META, nb=BATCH, nc=1, out_dtype=BF16)
    y_x, h_x, c_x = _ssd(z, xbc, dtr, h_m, None, c_m, None, prm,
                         row0=0, lv=SSD_CHUNK, nb=BATCH, nc=SEQ // SSD_CHUNK, out_dtype=BF16)
    y_s, h_s, c_s = _ssd(z, xbc, dtr, state_ssm, layer, state_conv, layer, prm,
                         row0=ROW_SAMPLE, lv=DEC_SEQ, nb=DEC_BATCH, nc=1, out_dtype=F32)
    yn = jnp.concatenate([y_x, y_m, y_s.astype(BF16)], axis=0)
    mix = _matmul(yn, p['m_w_out'], layer=layer, n_out=D_MODEL, tn=512)
    h, hp = _ln_residual(h, mix, p['ln_g'], p['ln_b'], layer)
    return h, hp, (h_x, c_x, h_s, c_s)


def _lambda_value(lam_ref, lam_init):
    lp = lam_ref[...]
    s1 = jnp.sum(lp[0:1, :] * lp[1:2, :], axis=-1, keepdims=True)
    s2 = jnp.sum(lp[2:3, :] * lp[3:4, :], axis=-1, keepdims=True)
    return jnp.exp(s1) - jnp.exp(s2) + lam_init


def _subln(o, sw, lam_init):
    ms = jnp.mean(o * o, axis=-1, keepdims=True)
    return o * lax.rsqrt(ms + NORM_EPS) * sw * (1.0 - lam_init)


def _rep(x, n):
    return x if n == 1 else jnp.concatenate([x] * n, axis=1)


def _flash_body(q_ref, k_ref, v_ref, km_ref, vm_ref, lam_ref, sw_ref, o_ref,
                m_ref, l_ref, acc_ref, kb_ref, vb_ref, kpad_ref, vpad_ref, *, lam_init):
    t = FLASH_T
    dh = A_HEAD_DIM
    qi = pl.program_id(2)

    @pl.when(qi == 0)
    def _load_kv():
        kb_ref[...] = k_ref[...].astype(BF16)
        vb_ref[...] = v_ref[...].astype(BF16)
        kpad_ref[...] = jnp.zeros(kpad_ref.shape, BF16)
        vpad_ref[...] = jnp.zeros(vpad_ref.shape, BF16)
        kpad_ref[0:N_META, :] = km_ref[...].astype(BF16)
        vpad_ref[0:N_META, :] = vm_ref[...].astype(BF16)

    m_ref[...] = jnp.full(m_ref.shape, -jnp.inf, F32)
    l_ref[...] = jnp.zeros(l_ref.shape, F32)
    acc_ref[...] = jnp.zeros(acc_ref.shape, F32)
    qs = [q_ref[:, j * dh:(j + 1) * dh].astype(BF16) for j in range(2)]

    def scores(kb):
        return [_dot_nt(qs[j], kb[:, j * dh:(j + 1) * dh]) for j in range(2)]

    def update(s_heads, vb, mask):
        for j in range(2):
            s = s_heads[j]
            if mask is not None:
                s = jnp.where(mask, s, -jnp.inf)
            m_prev = m_ref[j]
            m_new = jnp.maximum(m_prev, jnp.max(s, axis=1, keepdims=True))
            alpha = jnp.exp(m_prev - m_new)
            pmat = jnp.exp(s - _rep(m_new, s.shape[1] // LANES))
            l_ref[j] = alpha * l_ref[j] + jnp.sum(pmat, axis=1, keepdims=True)
            acc_ref[j] = _rep(alpha, 2) * acc_ref[j] + _dot(pmat.astype(BF16), vb)
            m_ref[j] = m_new

    def kv_block(kj):
        off = pl.multiple_of(kj * t, t)
        return kb_ref[pl.ds(off, t), :], vb_ref[pl.ds(off, t), :]

    colm = lax.broadcasted_iota(jnp.int32, (t, LANES), 1)
    r = lax.broadcasted_iota(jnp.int32, (t, t), 0)
    cc = lax.broadcasted_iota(jnp.int32, (t, t), 1)
    k_diag, v_diag = kv_block(qi)
    s_meta = scores(kpad_ref[...])
    s_diag = scores(k_diag)
    update(s_meta, vpad_ref[...], colm < N_META)
    update(s_diag, v_diag, cc <= r)

    def two_blocks(i, carry):
        k_a, v_a = kv_block(2 * i)
        k_b, v_b = kv_block(2 * i + 1)
        s_a = scores(k_a)
        s_b = scores(k_b)
        update(s_a, v_a, None)
        update(s_b, v_b, None)
        return carry

    lax.fori_loop(0, qi // 2, two_blocks, 0)

    @pl.when(qi % 2 == 1)
    def _odd_block():
        k_a, v_a = kv_block(qi - 1)
        update(scores(k_a), v_a, None)

    lam = _lambda_value(lam_ref, lam_init)
    o1 = acc_ref[0] / _rep(l_ref[0], 2)
    o2 = acc_ref[1] / _rep(l_ref[1], 2)
    o_ref[...] = _subln(o1 - lam * o2, sw_ref[...], lam_init).astype(o_ref.dtype)


def _flash_prompt(q, k, v, lam_p, sw, j_layer, lam_init):
    t = FLASH_T
    nq = SEQ // t
    w2 = 2 * A_HEAD_DIM
    qspec = pl.BlockSpec((t, w2), lambda b, p, qi: (b * nq + qi, p))
    kspec = pl.BlockSpec((SEQ, w2), lambda b, p, qi: (b, p))
    mspec = pl.BlockSpec((N_META, w2), lambda b, p, qi: (ROW_META // N_META + b, p))
    return pl.pallas_call(
        functools.partial(_flash_body, lam_init=lam_init),
        grid=(BATCH, A_HEADS, nq),
        in_specs=[qspec, kspec, kspec, mspec, mspec,
                  pl.BlockSpec((None, 4, A_HEAD_DIM), lambda b, p, qi: (j_layer, 0, 0)),
                  pl.BlockSpec((None, 1, w2), lambda b, p, qi: (j_layer, 0, 0))],
        out_specs=pl.BlockSpec((t, w2), lambda b, p, qi: (b * nq + qi, p)),
        out_shape=jax.ShapeDtypeStruct((ROWS_X, D_MODEL), BF16),
        scratch_shapes=[pltpu.VMEM((2, t, LANES), F32), pltpu.VMEM((2, t, LANES), F32),
                        pltpu.VMEM((2, t, w2), F32),
                        pltpu.VMEM((SEQ, w2), BF16), pltpu.VMEM((SEQ, w2), BF16),
                        pltpu.VMEM((LANES, w2), BF16), pltpu.VMEM((LANES, w2), BF16)],
        compiler_params=_cparams(3), name="flash_prompt",
    )(q, k, v, k, v, lam_p, sw)


SATTN_PG = 4


SATTN_SLOTS = 3


def _sattn_body(pt_ref, q_ref, ck_hbm, cv_hbm, kn_ref, vn_ref, lam_ref, sw_ref, o_ref,
                qf_ref, qb_ref, m_ref, l_ref, acc_ref, kb_ref, vb_ref, kring, vring0, vring1, ring_sem,
                *, lq, nb, n_steps, pg, lam_init):
    dh = A_HEAD_DIM
    n_qk = 2 * A_HEADS
    rows = n_qk * lq
    j = pl.program_id(1)
    k_rows = PAGE_SIZE * n_qk
    v_rows = PAGE_SIZE * A_HEADS
    total = nb * n_steps

    def group_copies(t, slot):
        seq = t // n_steps
        first = (t % n_steps) * pg
        copies = []
        for g in range(pg):
            page = pt_ref[seq, first + g]
            copies.append(pltpu.make_async_copy(
                ck_hbm.at[page], kring.at[slot, pl.ds(g * k_rows, k_rows)], ring_sem.at[0, slot]))
            copies.append(pltpu.make_async_copy(
                cv_hbm.at[page, :, pl.ds(0, LANES)], vring0.at[slot, pl.ds(g * v_rows, v_rows)],
                ring_sem.at[1, slot]))
            copies.append(pltpu.make_async_copy(
                cv_hbm.at[page, :, pl.ds(LANES, LANES)], vring1.at[slot, pl.ds(g * v_rows, v_rows)],
                ring_sem.at[2, slot]))
        return copies

    def start_group(t):
        for cp in group_copies(t, t % SATTN_SLOTS):
            cp.start()

    @pl.when(j == 0)
    def _init():
        qf_ref[...] = jnp.zeros(qf_ref.shape, F32)
        for hd in range(n_qk):
            qf_ref[hd * lq:(hd + 1) * lq, hd * dh:(hd + 1) * dh] = q_ref[:, hd * dh:(hd + 1) * dh]
        qb_ref[...] = qf_ref[...].astype(BF16)
        m_ref[...] = jnp.full(m_ref.shape, -jnp.inf, F32)
        l_ref[...] = jnp.zeros(l_ref.shape, F32)
        acc_ref[...] = jnp.zeros(acc_ref.shape, F32)

    def process(tk, mask):
        s = _dot_nt(qb_ref[...], kb_ref[0:tk, :])
        if mask is not None:
            s = jnp.where(mask, s, -jnp.inf)
        m_prev = m_ref[...]
        m_new = jnp.maximum(m_prev, jnp.max(s, axis=1, keepdims=True))
        alpha = jnp.exp(m_prev - m_new)
        pmat = jnp.exp(s - _rep(m_new, tk // LANES))
        l_ref[...] = alpha * l_ref[...] + jnp.sum(pmat, axis=1, keepdims=True)
        m_ref[...] = m_new
        pb = pmat.astype(BF16)
        alpha2 = _rep(alpha, 2)
        for pp in range(A_HEADS):
            rs = slice(2 * pp * lq, (2 * pp + 2) * lq)
            acc_ref[rs, :] = alpha2[rs] * acc_ref[rs, :] + _dot(pb[rs], vb_ref[0:tk, pp * 2 * dh:(pp + 1) * 2 * dh])

    @pl.when(j < n_steps)
    def _past_pages():
        if n_steps == 0:
            return
        t = pl.program_id(0) * n_steps + j
        slot = t % SATTN_SLOTS

        @pl.when(t == 0)
        def _prime():
            start_group(0)
            if total > 1:
                start_group(1)

        @pl.when(t + 2 < total)
        def _prefetch():
            start_group(t + 2)

        for cp in group_copies(t, slot):
            cp.wait()
        vrings = (vring0, vring1)
        for g in range(pg):
            ts = slice(g * PAGE_SIZE, (g + 1) * PAGE_SIZE)
            for hd in range(n_qk):
                kb_ref[ts, hd * dh:(hd + 1) * dh] = (
                    kring[slot, pl.ds(g * k_rows + hd, PAGE_SIZE, stride=n_qk), :].astype(BF16))
            for hv in range(A_HEADS):
                for half in range(2):
                    c0 = hv * 2 * dh + half * LANES
                    vb_ref[ts, c0:c0 + LANES] = (
                        vrings[half][slot, pl.ds(g * v_rows + hv, PAGE_SIZE, stride=A_HEADS), :].astype(BF16))
        process(pg * PAGE_SIZE, None)

    @pl.when(j == n_steps)
    def _new_keys():
        pad = jnp.zeros((PAGE_SIZE - lq, D_MODEL), F32)
        kb_ref[0:PAGE_SIZE, :] = jnp.concatenate([kn_ref[...], pad], axis=0).astype(BF16)
        vb_ref[0:PAGE_SIZE, :] = jnp.concatenate([vn_ref[...], pad], axis=0).astype(BF16)
        r = lax.broadcasted_iota(jnp.int32, (rows, LANES), 0)
        cc = lax.broadcasted_iota(jnp.int32, (rows, LANES), 1)
        process(PAGE_SIZE, cc <= jnp.bitwise_and(r, lq - 1))
        lam = _lambda_value(lam_ref, lam_init)
        for pp in range(A_HEADS):
            r1 = slice(2 * pp * lq, (2 * pp + 1) * lq)
            r2 = slice((2 * pp + 1) * lq, (2 * pp + 2) * lq)
            o1 = acc_ref[r1, :] / _rep(l_ref[r1, :], 2)
            o2 = acc_ref[r2, :] / _rep(l_ref[r2, :], 2)
            o_ref[:, pp * 2 * dh:(pp + 1) * 2 * dh] = _subln(o1 - lam * o2, sw_ref[...], lam_init).astype(o_ref.dtype)


def _small_attn(q, k, v, cache_k, cache_v, page_table, lam_p, sw, j_layer, lam_init,
                *, row0, lq, nb, n_pages, out_dtype):
    rb = row0 // lq
    pg = SATTN_PG if n_pages > 0 else 1
    assert n_pages % pg == 0
    n_steps = n_pages // pg
    rows = 2 * A_HEADS * lq
    qspec = pl.BlockSpec((lq, D_MODEL), lambda b, j, pt: (rb + b, 0))

    n_pool = cache_k.shape[0]
    k_rows = PAGE_SIZE * 2 * A_HEADS
    v_rows = PAGE_SIZE * A_HEADS
    cache_k2 = cache_k.reshape(n_pool, k_rows, A_HEAD_DIM)
    cache_v2 = cache_v.reshape(n_pool, v_rows, 2 * A_HEAD_DIM)
    grid_spec = pltpu.PrefetchScalarGridSpec(
        num_scalar_prefetch=1,
        grid=(nb, n_steps + 1),
        in_specs=[qspec, pl.BlockSpec(memory_space=pl.ANY), pl.BlockSpec(memory_space=pl.ANY),
                  qspec, qspec,
                  pl.BlockSpec((None, 4, A_HEAD_DIM), lambda b, j, pt: (j_layer, 0, 0)),
                  pl.BlockSpec((None, 1, 2 * A_HEAD_DIM), lambda b, j, pt: (j_layer, 0, 0))],
        out_specs=pl.BlockSpec((lq, D_MODEL), lambda b, j, pt: (b, 0)),
        scratch_shapes=[pltpu.VMEM((rows, D_MODEL), F32), pltpu.VMEM((rows, D_MODEL), BF16),
                        pltpu.VMEM((rows, LANES), F32), pltpu.VMEM((rows, LANES), F32),
                        pltpu.VMEM((rows, 2 * A_HEAD_DIM), F32),
                        pltpu.VMEM((pg * PAGE_SIZE, D_MODEL), BF16),
                        pltpu.VMEM((pg * PAGE_SIZE, D_MODEL), BF16),
                        pltpu.VMEM((SATTN_SLOTS, pg * k_rows, LANES), F32),
                        pltpu.VMEM((SATTN_SLOTS, pg * v_rows, LANES), F32),
                        pltpu.VMEM((SATTN_SLOTS, pg * v_rows, LANES), F32),
                        pltpu.SemaphoreType.DMA((3, SATTN_SLOTS))],
    )
    return pl.pallas_call(
        functools.partial(_sattn_body, lq=lq, nb=nb, n_steps=n_steps, pg=pg, lam_init=lam_init),
        grid_spec=grid_spec,
        out_shape=jax.ShapeDtypeStruct((nb * lq, D_MODEL), out_dtype),
        compiler_params=_cparams(2), name="small_attn",
    )(page_table, q, cache_k2, cache_v2, k, v, lam_p, sw)


def _attn_layer(h, hb, k, v, rope_tabs, cache_k, cache_v, page_table, p, layer):
    j_layer = layer - N_A_LAYERS
    lam_init = 0.8 - 0.6 * math.exp(-0.3 * layer)
    q = _matmul(hb, p['a_w_q'], layer=j_layer, n_out=D_MODEL, tn=1024, rope=rope_tabs,
                scale=A_HEAD_DIM ** -0.5)
    lam_p = p['a_lambda'].astype(F32)
    sw = p['a_subln_w'].astype(F32).reshape(-1, 1, 2 * A_HEAD_DIM)
    o_x = _flash_prompt(q, k, v, lam_p, sw, j_layer, lam_init)
    dummy_pt = jnp.zeros((BATCH, 1), jnp.int32)
    o_m = _small_attn(q, k, v, cache_k, cache_v, dummy_pt, lam_p, sw, j_layer, lam_init,
                      row0=ROW_META, lq=N_META, nb=BATCH, n_pages=0, out_dtype=BF16)
    o_s = _small_attn(q, k, v, cache_k, cache_v, page_table, lam_p, sw, j_layer, lam_init,
                      row0=ROW_SAMPLE, lq=DEC_SEQ, nb=DEC_BATCH, n_pages=page_table.shape[1],
                      out_dtype=F32)
    o = jnp.concatenate([o_x, o_m, o_s.astype(BF16)], axis=0)
    mix = _matmul(o, p['a_w_o'], layer=j_layer, n_out=D_MODEL, tn=1024)
    return _ln_residual(h, mix, p['ln_g'], p['ln_b'], layer)


def _rope_tables():
    pos = jnp.concatenate([
        jnp.tile(N_META + jnp.arange(SEQ, dtype=jnp.int32), BATCH),
        jnp.tile(jnp.arange(N_META, dtype=jnp.int32), BATCH),
        jnp.tile(PAST_LEN + jnp.arange(DEC_SEQ, dtype=jnp.int32), DEC_BATCH)])
    half = A_HEAD_DIM // 2
    inv = ROPE_THETA ** (-jnp.arange(half, dtype=F32) / half)
    ang = pos.astype(F32)[:, None] * inv[None, :]
    cos = jnp.cos(ang)
    sin = jnp.sin(ang)
    return jnp.concatenate([cos, cos], axis=1), jnp.concatenate([-sin, sin], axis=1)


def kernel(x_prompt, x_sample, cache_k, cache_v, page_table, state_ssm, state_conv, meta_tokens, m_w_in, m_conv_w, m_conv_b, m_dt_bias, m_a_log, m_d, m_norm_w, m_w_out, a_w_q, a_w_k, a_w_v, a_lambda, a_subln_w, a_w_o, moe_w_router, moe_b_router, moe_w_gu, moe_b_gu, moe_w_down, moe_b_down, ln_g, ln_b):
    p = dict(m_w_in=m_w_in, m_conv_w=m_conv_w, m_conv_b=m_conv_b, m_dt_bias=m_dt_bias, m_a_log=m_a_log,
             m_d=m_d, m_norm_w=m_norm_w, m_w_out=m_w_out, a_w_q=a_w_q, a_w_k=a_w_k, a_w_v=a_w_v,
             a_lambda=a_lambda, a_subln_w=a_subln_w, a_w_o=a_w_o, moe_w_router=moe_w_router,
             moe_b_router=moe_b_router, moe_w_gu=moe_w_gu, moe_b_gu=moe_b_gu, moe_w_down=moe_w_down,
             moe_b_down=moe_b_down, ln_g=ln_g, ln_b=ln_b)
    assert page_table.shape[1] * PAGE_SIZE == PAST_LEN
    meta = jnp.broadcast_to(meta_tokens[None].astype(F32), (BATCH, N_META, D_MODEL))
    h = jnp.concatenate([x_prompt.reshape(ROWS_X, D_MODEL), meta.reshape(ROWS_META, D_MODEL),
                         x_sample.reshape(ROWS_SAMPLE, D_MODEL)], axis=0)
    hb = h.astype(BF16)
    rope_tabs = _rope_tables()

    ssm_p, conv_p, ssm_s, conv_s = [], [], [], []
    k = v = None
    for layer in range(DEPTH):
        if layer < N_A_LAYERS:
            h, hp, (h_x, c_x, h_s, c_s) = _mamba_layer(h, hb, state_ssm, state_conv, p, layer)
            ssm_p.append(h_x)
            conv_p.append(c_x)
            ssm_s.append(h_s)
            conv_s.append(c_s)
        else:
            if layer == N_A_LAYERS:
                k = _matmul(hb, a_w_k, n_out=D_MODEL, tn=1024, rope=rope_tabs)
                v = _matmul(hb, a_w_v, n_out=D_MODEL, tn=1024)
            h, hp = _attn_layer(h, hb, k, v, rope_tabs, cache_k, cache_v, page_table, p, layer)
        h, hb = _moe_layer(h, hp, p, layer, with_bf16=layer + 1 < DEPTH)

    def prompt_rows(a):
        return jnp.concatenate([a[ROW_META:ROW_SAMPLE].reshape(BATCH, N_META, D_MODEL),
                                a[:ROWS_X].reshape(BATCH, SEQ, D_MODEL)], axis=1)

    y_prompt = h[:ROWS_X].reshape(BATCH, SEQ, D_MODEL)
    y_sample = h[ROW_SAMPLE:].reshape(DEC_BATCH, DEC_SEQ, D_MODEL)
    k_prompt = prompt_rows(k).reshape(BATCH, SEQ + N_META, 2 * A_HEADS, A_HEAD_DIM)
    v_prompt = prompt_rows(v).reshape(BATCH, SEQ + N_META, A_HEADS, 2 * A_HEAD_DIM)
    k_sample = k[ROW_SAMPLE:].reshape(DEC_BATCH, DEC_SEQ, 2 * A_HEADS, A_HEAD_DIM)
    v_sample = v[ROW_SAMPLE:].reshape(DEC_BATCH, DEC_SEQ, A_HEADS, 2 * A_HEAD_DIM)
    return (y_prompt, y_sample, k_prompt, v_prompt, jnp.stack(ssm_p), jnp.stack(conv_p),
            k_sample, v_sample, jnp.stack(ssm_s), jnp.stack(conv_s))
```
